```python
import math
import jax, jax.numpy as jnp
from jax import lax
import numpy as np

D_MODEL = 1024
BATCH = 2
SEQ = 16384
DEPTH = 2
DEC_BATCH = 4
DEC_SEQ = 4096
PAST_LEN = 128

GRID_W = 64
HEAD_DIM = 64
NA_HEADS = 4
NA_WIN_ROWS = 8
NA_WIN_COLS = 16
NA_QCOLS = 16
NA_KCOLS = 32
DA_HEADS = 4
DA_CONFIGS = ((128, 1), (512, 4), (2048, 16))
DA_QBLOCK = 64
N_BUCKETS = 32
MAX_DISTANCE = 1024
HG_HEADS = 4
HG_DK = 128
HG_DV = 128
HG_CHUNK = 64

NA_WIDTH = NA_HEADS * HEAD_DIM
DA_WIDTH = DA_HEADS * HEAD_DIM
HG_FDIM = HG_HEADS * HG_DK
HG_WIDTH = HG_HEADS * HG_DV
MIX_WIDTH = NA_WIDTH + DA_WIDTH + HG_WIDTH
IN_SPLITS = (NA_WIDTH,) * 3 + (DA_WIDTH,) * 3 + (HG_FDIM,) * 3 + (HG_WIDTH,) * 2
N_IN = sum(IN_SPLITS)
D_FF = 2816
NORM_EPS = 1e-6
NEG_INF = -1e30

kernel_name = "hybrid_na_dilated_hgrn2_encoder"


def rmsnorm(x, g):
    x32 = x.astype(jnp.float32)
    y = x32 * lax.rsqrt(jnp.mean(x32 * x32, axis=-1, keepdims=True) + NORM_EPS)
    return (y * g.astype(jnp.float32)).astype(x.dtype)


def t5_bucket_np(rel):
    nb = N_BUCKETS // 2
    max_exact = nb // 2
    ret = np.where(rel > 0, nb, 0)
    n = np.abs(rel)
    large = max_exact + (np.log(np.maximum(n, 1) / max_exact)
                         / np.log(MAX_DISTANCE / max_exact) * (nb - max_exact)).astype(np.int32)
    large = np.minimum(large, nb - 1)
    return (ret + np.where(n < max_exact, n, large)).astype(np.int32)


def neighbourhood_attention(q, k, v, rpb):
    B, T, H, hd = q.shape
    rows = T // GRID_W
    wr = min(NA_WIN_ROWS, rows)
    ncb = GRID_W // NA_QCOLS
    r = np.arange(rows)
    r0 = np.clip(r - wr // 2, 0, rows - wr)
    ri = r0[:, None] + np.arange(wr)[None, :]
    c = np.arange(GRID_W).reshape(ncb, NA_QCOLS)
    c0 = np.clip(c - NA_WIN_COLS // 2, 0, GRID_W - NA_WIN_COLS)
    kb0 = np.clip(np.arange(ncb) * NA_QCOLS - NA_WIN_COLS // 2, 0, GRID_W - NA_KCOLS)
    ci = kb0[:, None] + np.arange(NA_KCOLS)[None, :]
    col_ok = (ci[:, None, :] >= c0[:, :, None]) & (ci[:, None, :] < c0[:, :, None] + NA_WIN_COLS)
    dr = ri - r[:, None] + NA_WIN_ROWS - 1
    dc = np.clip(ci[:, None, :] - c[:, :, None], -(NA_WIN_COLS - 1), NA_WIN_COLS - 1) + NA_WIN_COLS - 1
    bias = rpb[:, dr[:, None, None, :, None], dc[None, :, :, None, :]].astype(jnp.float32)
    bias = jnp.where(col_ok[None, None, :, :, None, :], bias, NEG_INF)
    qg = q.reshape(B, rows, ncb, NA_QCOLS, H, hd)
    rsel = ri[:, :, None, None]
    csel = ci[None, None, :, :]
    kg = k.reshape(B, rows, GRID_W, H, hd)[:, rsel, csel]
    vg = v.reshape(B, rows, GRID_W, H, hd)[:, rsel, csel]
    s = jnp.einsum('brnqhd,brwnkhd->bhrnqwk', qg, kg).astype(jnp.float32) + bias[None]
    p = jax.nn.softmax(s.reshape(s.shape[:5] + (-1,)), axis=-1).reshape(s.shape).astype(v.dtype)
    o = jnp.einsum('bhrnqwk,brwnkhd->brnqhd', p, vg)
    return o.reshape(B, T, H * hd).astype(jnp.float32)


def dilated_branch(q, k, v, bias_table, window, dilation):
    B, T, H, hd = q.shape
    d = dilation
    L = T // d
    hw = window // (2 * d)
    nblk = -(-L // DA_QBLOCK)
    lp = nblk * DA_QBLOCK
    kb = DA_QBLOCK + 2 * hw

    def to_sub(x):
        return x.reshape(B, L, d, H, hd).transpose(0, 2, 3, 1, 4).reshape(B * d, H, L, hd)

    qs = jnp.pad(to_sub(q), ((0, 0), (0, 0), (0, lp - L), (0, 0))).reshape(B * d, H, nblk, DA_QBLOCK, hd)
    kidx = np.arange(nblk)[:, None] * DA_QBLOCK + np.arange(kb)[None, :]
    kpad = ((0, 0), (0, 0), (hw, lp - L + hw), (0, 0))
    ks = jnp.pad(to_sub(k), kpad)[:, :, kidx]
    vs = jnp.pad(to_sub(v), kpad)[:, :, kidx]
    rel = np.arange(kb)[None, :] - np.arange(DA_QBLOCK)[:, None] - hw
    pos = kidx - hw
    ok = (np.abs(rel) <= hw)[None] & ((pos >= 0) & (pos < L))[:, None, :]
    bias = bias_table[t5_bucket_np(rel * d)].transpose(2, 0, 1).astype(jnp.float32)
    s = jnp.einsum('zhcqd,zhckd->zhcqk', qs, ks).astype(jnp.float32) + bias[None, :, None]
    s = jnp.where(ok[None, None], s, NEG_INF)
    lse = jax.nn.logsumexp(s, axis=-1)
    p = jnp.exp(s - lse[..., None]).astype(v.dtype)
    o = jnp.einsum('zhcqk,zhckd->zhcqd', p, vs).reshape(B * d, H, lp, hd)[:, :, :L]
    o = o.reshape(B, d, H, L, hd).transpose(0, 3, 1, 2, 4).reshape(B, T, H, hd)
    lse = lse.reshape(B * d, H, lp)[:, :, :L].reshape(B, d, H, L).transpose(0, 3, 1, 2).reshape(B, T, H)
    return o.astype(jnp.float32), lse


def dilated_attention(q, k, v, bias_table):
    outs, lses = [], []
    for window, dilation in DA_CONFIGS:
        o, lse = dilated_branch(q, k, v, bias_table, window, dilation)
        outs.append(o)
        lses.append(lse)
    w = jax.nn.softmax(jnp.stack(lses), axis=0)
    o = jnp.sum(w[..., None] * jnp.stack(outs), axis=0)
    B, T = q.shape[:2]
    return o.reshape(B, T, DA_WIDTH)


def hgrn_lower_bounds(lb_logits):
    p = jax.nn.softmax(lb_logits.astype(jnp.float32), axis=1)
    c = jnp.cumsum(p, axis=1)
    return c - c[:, :1]


def gla_chunk_scan(q, k, v, logf):
    B, T, H, dk = q.shape
    dv = v.shape[-1]
    C = HG_CHUNK
    nc = T // C

    def chunks(x):
        return x.reshape(B, nc, C, H, x.shape[-1]).transpose(1, 0, 3, 2, 4)

    tri = np.tril(np.ones((C, C), dtype=bool))[:, :, None]

    def step(S, inp):
        qc, kc, vc, gc = inp
        b = jnp.cumsum(gc, axis=-2)
        dec = jnp.exp(jnp.where(tri, b[..., :, None, :] - b[..., None, :, :], -jnp.inf))
        attn = jnp.einsum('bhtk,bhsk,bhtsk->bhts', qc, kc, dec)
        o = jnp.einsum('bhts,bhsv->bhtv', attn, vc) + jnp.einsum('bhtk,bhkv->bhtv', qc * jnp.exp(b), S)
        b_last = b[..., -1:, :]
        S = jnp.exp(b_last[..., 0, :])[..., None] * S + jnp.einsum('bhsk,bhsv->bhkv', kc * jnp.exp(b_last - b), vc)
        return S, o

    S0 = jnp.zeros((B, H, dk, dv), jnp.float32)
    _, o = lax.scan(step, S0, (chunks(q), chunks(k), chunks(v), chunks(logf)))
    return o.transpose(1, 0, 3, 2, 4).reshape(B, T, H, dv)


def hgrn2_mixer(cq, cff, cfb, ci, cg, lb_f, lb_b, norm_g):
    B, T, _ = cq.shape
    q = jax.nn.silu(cq.astype(jnp.float32)).reshape(B, T, HG_HEADS, HG_DK)
    v = ci.astype(jnp.float32).reshape(B, T, HG_HEADS, HG_DV)

    def gates(z, lb):
        z = z.astype(jnp.float32).reshape(B, T, HG_HEADS, HG_DK)
        lb = lb.reshape(HG_HEADS, HG_DK)
        logf = jnp.logaddexp(jnp.log(lb), jnp.log1p(-lb) + jax.nn.log_sigmoid(z))
        kk = (1.0 - lb) * jax.nn.sigmoid(-z)
        return logf, kk

    lf_f, k_f = gates(cff, lb_f)
    lf_b, k_b = gates(cfb, lb_b)
    o_f = gla_chunk_scan(q, k_f, v, lf_f)
    rev = lambda a: jnp.flip(a, axis=1)
    o_b = rev(gla_chunk_scan(rev(q), rev(k_b), rev(v), rev(lf_b)))
    o = rmsnorm(o_f + o_b, norm_g) * jax.nn.silu(cg.astype(jnp.float32)).reshape(B, T, HG_HEADS, HG_DV)
    return o.reshape(B, T, HG_WIDTH)


def encoder_layer(x, ln_mix_g, w_in, na_q_g, na_k_g, na_rpb, da_q_g, da_k_g, t5_bias,
                  lb_f, lb_b, hg_norm_g, w_out, ln_ffn_g, w_up, conv_w, conv_b, w_down):
    B, T, _ = x.shape
    scale = HEAD_DIM ** -0.5
    h = rmsnorm(x, ln_mix_g)
    proj = h @ w_in
    (aq, ak, av, bq, bk, bv, cq, cff, cfb, ci, cg) = jnp.split(proj, np.cumsum(IN_SPLITS)[:-1].tolist(), axis=-1)
    heads = lambda a, n: a.reshape(B, T, n, HEAD_DIM)
    qa = rmsnorm(heads(aq, NA_HEADS), na_q_g) * scale
    ka = rmsnorm(heads(ak, NA_HEADS), na_k_g)
    o_a = neighbourhood_attention(qa, ka, heads(av, NA_HEADS), na_rpb)
    qb = rmsnorm(heads(bq, DA_HEADS), da_q_g) * scale
    kb = rmsnorm(heads(bk, DA_HEADS), da_k_g)
    o_b = dilated_attention(qb, kb, heads(bv, DA_HEADS), t5_bias)
    o_c = hgrn2_mixer(cq, cff, cfb, ci, cg, lb_f, lb_b, hg_norm_g)
    mix = jnp.concatenate([o_a, o_b, o_c], axis=-1).astype(x.dtype) @ w_out
    x = x + mix.astype(x.dtype)
    h = rmsnorm(x, ln_ffn_g)
    gate, up = jnp.split(h @ w_up, 2, axis=-1)
    gp = jnp.pad(gate, ((0, 0), (1, 1), (0, 0)))
    gate = gp[:, :-2] * conv_w[0] + gp[:, 1:-1] * conv_w[1] + gp[:, 2:] * conv_w[2] + conv_b
    y = (jax.nn.gelu(gate) * up) @ w_down
    return x + y.astype(x.dtype)


def setup_inputs(seed: int = 0) -> dict:
    key = jax.random.key(seed)
    ks = jax.random.split(key, 20)
    f32 = jnp.float32
    nrm = lambda k, shape, s: jax.random.normal(k, shape, f32) * s
    return {
        "x_prompt": nrm(ks[0], (BATCH, SEQ, D_MODEL), 1.0),
        "x_sample": nrm(ks[1], (DEC_BATCH, DEC_SEQ, D_MODEL), 1.0),
        "ln_mix_g": 1.0 + nrm(ks[2], (DEPTH, D_MODEL), 0.1),
        "w_in": nrm(ks[3], (DEPTH, D_MODEL, N_IN), D_MODEL ** -0.5),
        "na_q_g": 1.0 + nrm(ks[4], (DEPTH, HEAD_DIM), 0.1),
        "na_k_g": 1.0 + nrm(ks[5], (DEPTH, HEAD_DIM), 0.1),
        "na_rpb": nrm(ks[6], (DEPTH, NA_HEADS, 2 * NA_WIN_ROWS - 1, 2 * NA_WIN_COLS - 1), 0.5),
        "da_q_g": 1.0 + nrm(ks[7], (DEPTH, HEAD_DIM), 0.1),
        "da_k_g": 1.0 + nrm(ks[8], (DEPTH, HEAD_DIM), 0.1),
        "t5_bias": nrm(ks[9], (N_BUCKETS, DA_HEADS), 0.5),
        "hg_lb_logits": nrm(ks[10], (2, DEPTH, HG_FDIM), 1.0),
        "hg_norm_g": 1.0 + nrm(ks[11], (DEPTH, HG_DV), 0.1),
        "w_out": nrm(ks[12], (DEPTH, MIX_WIDTH, D_MODEL), MIX_WIDTH ** -0.5),
        "ln_ffn_g": 1.0 + nrm(ks[13], (DEPTH, D_MODEL), 0.1),
        "w_up": nrm(ks[14], (DEPTH, D_MODEL, 2 * D_FF), D_MODEL ** -0.5),
        "conv_w": nrm(ks[15], (DEPTH, 3, D_FF), 3 ** -0.5),
        "conv_b": nrm(ks[16], (DEPTH, D_FF), 0.01),
        "w_down": nrm(ks[17], (DEPTH, D_FF, D_MODEL), D_FF ** -0.5),
    }


def reference(x_prompt, x_sample, ln_mix_g, w_in, na_q_g, na_k_g, na_rpb, da_q_g, da_k_g, t5_bias,
              hg_lb_logits, hg_norm_g, w_out, ln_ffn_g, w_up, conv_w, conv_b, w_down):
    lb = hgrn_lower_bounds(hg_lb_logits)

    def trunk(x):
        for l in range(DEPTH):
            x = encoder_layer(x, ln_mix_g[l], w_in[l], na_q_g[l], na_k_g[l], na_rpb[l],
                              da_q_g[l], da_k_g[l], t5_bias, lb[0, l], lb[1, l], hg_norm_g[l],
                              w_out[l], ln_ffn_g[l], w_up[l], conv_w[l], conv_b[l], w_down[l])
        return x

    y_prompt = trunk(x_prompt)
    y_sample = trunk(x_sample)
    return (y_prompt, y_sample)
```

```python
import functools
import math

import numpy as np
import jax
import jax.numpy as jnp
from jax import lax
from jax.experimental import pallas as pl
from jax.experimental.pallas import tpu as pltpu

F32 = jnp.float32
BF16 = jnp.bfloat16

D_MODEL = 1024
GRID_W = 64
HEAD_DIM = 64
NA_HEADS = 4
NA_WIN_ROWS = 8
NA_WIN_COLS = 16
DA_HEADS = 4
DA_CONFIGS = ((128, 1), (512, 4), (2048, 16))
DA_HALO = 64
N_BUCKETS = 32
MAX_DISTANCE = 1024
HG_HEADS = 4
HG_DK = 128
HG_CHUNK = 64
N_IN = 4096
D_FF = 2816
NORM_EPS = 1e-6
NEG_INF = -1e30
SEG = 256
FF_SEG = 256
VMEM_LIMIT = 56 * 1024 * 1024


def _cparams(*sem):
    return pltpu.CompilerParams(dimension_semantics=sem, vmem_limit_bytes=VMEM_LIMIT)


def _dot(a, b):
    return jnp.dot(a, b, preferred_element_type=F32)


def _dot_nt(a, b):
    return lax.dot_general(a, b, (((1,), (1,)), ((), ())), preferred_element_type=F32)


def _dot_tn(a, b):
    return lax.dot_general(a, b, (((0,), (0,)), ((), ())), preferred_element_type=F32)


def _inproj_kernel(x_ref, g_ref, w_ref, qg_ref, gm_ref, o_ref):
    x = x_ref[...]
    ms = jnp.mean(x * x, axis=-1, keepdims=True)
    h = (x * lax.rsqrt(ms + NORM_EPS) * g_ref[...]).astype(BF16)
    normed = {0: 0, 1: 1, 3: 2, 4: 3}
    for seg in range(N_IN // SEG):
        y = _dot(h, w_ref[:, seg * SEG:(seg + 1) * SEG])
        if seg in normed:
            ss = _dot((y * y).astype(BF16), gm_ref[...])
            r = normed[seg]
            y = y * lax.rsqrt(ss + NORM_EPS) * qg_ref[r:r + 1, :]
        o_ref[:, seg * SEG:(seg + 1) * SEG] = y.astype(BF16)


def _in_proj(x, ln_g, w_in, qg, gm, tm):
    n = x.shape[0]
    return pl.pallas_call(
        _inproj_kernel,
        grid=(n // tm,),
        in_specs=[
            pl.BlockSpec((tm, D_MODEL), lambda i: (i, 0)),
            pl.BlockSpec((1, D_MODEL), lambda i: (0, 0)),
            pl.BlockSpec((D_MODEL, N_IN), lambda i: (0, 0)),
            pl.BlockSpec((4, SEG), lambda i: (0, 0)),
            pl.BlockSpec((SEG, SEG), lambda i: (0, 0)),
        ],
        out_specs=pl.BlockSpec((tm, N_IN), lambda i: (i, 0)),
        out_shape=jax.ShapeDtypeStruct((n, N_IN), BF16),
        compiler_params=_cparams("parallel"),
        name="in_proj",
    )(x, ln_g, w_in, qg, gm)


NA_RB = 8
NA_TOK = NA_RB * GRID_W
NA_KEYS = NA_WIN_ROWS * GRID_W


def _na_bias_table(rpb):
    cq = np.arange(GRID_W)
    ck = np.arange(GRID_W)
    c0 = np.clip(cq - NA_WIN_COLS // 2, 0, GRID_W - NA_WIN_COLS)
    ok = (ck[None, :] >= c0[:, None]) & (ck[None, :] < c0[:, None] + NA_WIN_COLS)
    dc = np.clip(ck[None, :] - cq[:, None], -(NA_WIN_COLS - 1), NA_WIN_COLS - 1) + NA_WIN_COLS - 1
    dr = -np.arange(NA_WIN_ROWS)[:, None] + np.arange(NA_WIN_ROWS)[None, :] + NA_WIN_ROWS - 1
    bias = rpb[:, dr[:, None, :, None], dc[None, :, None, :]].astype(F32)
    bias = jnp.where(ok[None, None, :, None, :], bias, NEG_INF)
    return bias.transpose(1, 0, 2, 3, 4).reshape(NA_WIN_ROWS, NA_HEADS, GRID_W, NA_KEYS)


def _na_kernel(q_ref, kp_ref, kc_ref, kn_ref, vp_ref, vc_ref, vn_ref, bias_ref, o_ref, kbuf, vbuf, *, rows):
    i = pl.program_id(1)
    kbuf[0:NA_TOK] = kp_ref[...]
    kbuf[NA_TOK:2 * NA_TOK] = kc_ref[...]
    kbuf[2 * NA_TOK:3 * NA_TOK] = kn_ref[...]
    vbuf[0:NA_TOK] = vp_ref[...]
    vbuf[NA_TOK:2 * NA_TOK] = vc_ref[...]
    vbuf[2 * NA_TOK:3 * NA_TOK] = vn_ref[...]
    low = lax.broadcasted_iota(jnp.int32, (GRID_W, 128), 1) < HEAD_DIM
    for j in range(NA_RB):
        r = i * NA_RB + j
        r0 = jnp.clip(r - NA_WIN_ROWS // 2, 0, rows - NA_WIN_ROWS)
        e = r - r0
        off = pl.multiple_of((r0 - i * NA_RB + NA_RB) * GRID_W, GRID_W)
        q = q_ref[j * GRID_W:(j + 1) * GRID_W, :]
        for a in range(2):
            qa = q[:, a * 128:(a + 1) * 128]
            ka = kbuf[pl.ds(off, NA_KEYS), a * 128:(a + 1) * 128]
            va = vbuf[pl.ds(off, NA_KEYS), a * 128:(a + 1) * 128]
            outs = []
            for half in range(2):
                qm = jnp.where(low if half == 0 else ~low, qa, jnp.zeros_like(qa))
                s = _dot_nt(qm, ka) + bias_ref[e, 2 * a + half]
                m = jnp.max(s, axis=-1, keepdims=True)
                p = jnp.exp(s - m)
                l = jnp.sum(p, axis=-1, keepdims=True)
                outs.append(_dot(p.astype(BF16), va) / l)
            o_ref[j * GRID_W:(j + 1) * GRID_W, a * 128:(a + 1) * 128] = jnp.where(low, outs[0], outs[1]).astype(BF16)


def _na_call(proj, bias, b, t, row_base):
    rows = t // GRID_W
    nrb = rows // NA_RB
    base = row_base // NA_TOK

    def spec(col, shift):
        def imap(bi, i):
            return (base + bi * nrb + jnp.clip(i + shift, 0, nrb - 1), col)
        return pl.BlockSpec((NA_TOK, SEG), imap)

    return pl.pallas_call(
        functools.partial(_na_kernel, rows=rows),
        grid=(b, nrb),
        in_specs=[spec(0, 0), spec(1, -1), spec(1, 0), spec(1, 1), spec(2, -1), spec(2, 0), spec(2, 1),
                  pl.BlockSpec((NA_WIN_ROWS, NA_HEADS, GRID_W, NA_KEYS), lambda bi, i: (0, 0, 0, 0))],
        out_specs=pl.BlockSpec((NA_TOK, SEG), lambda bi, i: (bi * nrb + i, 0)),
        out_shape=jax.ShapeDtypeStruct((b * t, SEG), BF16),
        scratch_shapes=[pltpu.VMEM((3 * NA_TOK, SEG), BF16), pltpu.VMEM((3 * NA_TOK, SEG), BF16)],
        compiler_params=_cparams("parallel", "parallel"),
        name="na",
    )(proj, proj, proj, proj, proj, proj, proj, bias)


def _t5_bucket(rel):
    nb = N_BUCKETS // 2
    max_exact = nb // 2
    ret = np.where(rel > 0, nb, 0)
    n = np.abs(rel)
    large = max_exact + (np.log(np.maximum(n, 1) / max_exact)
                         / np.log(MAX_DISTANCE / max_exact) * (nb - max_exact)).astype(np.int32)
    large = np.minimum(large, nb - 1)
    return (ret + np.where(n < max_exact, n, large)).astype(np.int32)


def _da_bias_table(t5_bias, dilation, qb):
    kb = qb + 2 * DA_HALO
    rel = np.arange(kb)[None, :] - np.arange(qb)[:, None] - DA_HALO
    ok = np.abs(rel) <= DA_HALO
    bias = t5_bias[_t5_bucket(rel * dilation)].transpose(2, 0, 1).astype(F32)
    return jnp.where(ok[None], bias, NEG_INF)


def _da_kernel(*refs, qb, length, final):
    if final:
        (q_ref, kl_ref, km_ref, kr_ref, vl_ref, vm_ref, vr_ref, bias_ref,
         o1_ref, l1_ref, o2_ref, l2_ref, o_ref, kbuf, vbuf) = refs
    else:
        (q_ref, kl_ref, km_ref, kr_ref, vl_ref, vm_ref, vr_ref, bias_ref, o_ref, lse_ref, kbuf, vbuf) = refs
    c = pl.program_id(2)
    kb = qb + 2 * DA_HALO
    kbuf[0:DA_HALO] = kl_ref[...]
    kbuf[DA_HALO:DA_HALO + qb] = km_ref[...]
    kbuf[DA_HALO + qb:kb] = kr_ref[...]
    vbuf[0:DA_HALO] = vl_ref[...]
    vbuf[DA_HALO:DA_HALO + qb] = vm_ref[...]
    vbuf[DA_HALO + qb:kb] = vr_ref[...]
    pos = c * qb - DA_HALO + lax.broadcasted_iota(jnp.int32, (qb, kb), 1)
    pos_ok = (pos >= 0) & (pos < length)
    low = lax.broadcasted_iota(jnp.int32, (qb, 128), 1) < HEAD_DIM
    q = q_ref[...]
    for a in range(2):
        qa = q[:, a * 128:(a + 1) * 128]
        ka = kbuf[:, a * 128:(a + 1) * 128]
        va = vbuf[:, a * 128:(a + 1) * 128]
        outs, lses = [], []
        for half in range(2):
            qm = jnp.where(low if half == 0 else ~low, qa, jnp.zeros_like(qa))
            s = jnp.where(pos_ok, _dot_nt(qm, ka) + bias_ref[2 * a + half], NEG_INF)
            m = jnp.max(s, axis=-1, keepdims=True)
            p = jnp.exp(s - m)
            l = jnp.sum(p, axis=-1, keepdims=True)
            outs.append(_dot(p.astype(BF16), va) / l)
            lses.append(jnp.broadcast_to(m + jnp.log(l), (qb, 128)))
        o = jnp.where(low, outs[0], outs[1])
        lse = jnp.where(low, lses[0], lses[1])
        cols = slice(a * 128, (a + 1) * 128)
        if final:
            la, lb = l1_ref[:, cols], l2_ref[:, cols]
            mx = jnp.maximum(jnp.maximum(la, lb), lse)
            wa, wb, wc = jnp.exp(la - mx), jnp.exp(lb - mx), jnp.exp(lse - mx)
            mix = (wa * o1_ref[:, cols] + wb * o2_ref[:, cols] + wc * o) / (wa + wb + wc)
            o_ref[:, cols] = mix.astype(BF16)
        else:
            o_ref[:, cols] = o
            lse_ref[:, cols] = lse


def _da_call(proj, bias, b, t, row_base, dilation, prev=None):
    n = proj.shape[0]
    d = dilation
    length = t // d
    qb = min(128, length)
    nqb = length // qb
    kb = qb + 2 * DA_HALO
    ncol = N_IN // SEG
    view = proj.reshape(n // d, d * N_IN)
    ubase = row_base // d
    hpb = qb // DA_HALO
    nhalo = (n // d) // DA_HALO
    final = prev is not None

    def main(col):
        return pl.BlockSpec((qb, SEG), lambda bi, r, c: ((ubase + bi * length) // qb + c, r * ncol + col))

    def halo(col, right):
        def imap(bi, r, c):
            blk = (ubase + bi * length) // DA_HALO + (c + 1) * hpb if right else \
                (ubase + bi * length) // DA_HALO + c * hpb - 1
            return (jnp.clip(blk, 0, nhalo - 1), r * ncol + col)
        return pl.BlockSpec((DA_HALO, SEG), imap)

    in_specs = [main(3), halo(4, False), main(4), halo(4, True), halo(5, False), main(5), halo(5, True),
                pl.BlockSpec((DA_HEADS, qb, kb), lambda bi, r, c: (0, 0, 0))]
    args = [view] * 7 + [bias]
    out_block = pl.BlockSpec((qb, SEG), lambda bi, r, c: (bi * nqb + c, r))
    if final:
        assert d == 1
        in_specs += [out_block] * 4
        args += list(prev)
        out_specs = out_block
        out_shape = jax.ShapeDtypeStruct((b * t, SEG), BF16)
    else:
        out_specs = [out_block, out_block]
        out_shape = [jax.ShapeDtypeStruct((b * t // d, d * SEG), F32)] * 2
    res = pl.pallas_call(
        functools.partial(_da_kernel, qb=qb, length=length, final=final),
        grid=(b, d, nqb),
        in_specs=in_specs,
        out_specs=out_specs,
        out_shape=out_shape,
        scratch_shapes=[pltpu.VMEM((kb, SEG), BF16), pltpu.VMEM((kb, SEG), BF16)],
        compiler_params=_cparams("parallel", "parallel", "parallel"),
        name="da_d%d" % d,
    )(*args)
    if final:
        return res
    return [r.reshape(b * t, SEG) for r in res]


HG_LEVELS = (32, 16, 8, 4, 2, 1)
HG_NSEC = 2 + len(HG_LEVELS)


def _hg_constants(rev):
    c = HG_CHUNK
    t = np.arange(c)[:, None]
    j = np.arange(c)[None, :]
    secs = []
    if not rev:
        secs.append(j <= t)
        secs.append(j > t)
    else:
        secs.append(j >= t)
        secs.append(j < t)
    masks = [np.eye(c, dtype=bool)]
    s = np.arange(c)[None, :]
    for m in HG_LEVELS:
        p0 = (t // (2 * m)) * (2 * m)
        upper = (t & m) != 0
        if not rev:
            mid = p0 + m - 1
            sec = np.where(upper, (j > mid) & (j <= t), (j > t) & (j <= mid))
            pair = ((t // (2 * m)) == (s // (2 * m))) & upper & ((s & m) == 0)
        else:
            mid = p0 + m
            sec = np.where(upper, (j >= mid) & (j < t), (j >= t) & (j < mid))
            pair = ((t // (2 * m)) == (s // (2 * m))) & (~upper) & ((s & m) != 0)
        secs.append(sec)
        masks.append(pair)
    w = np.concatenate(secs, axis=0).astype(np.float32)
    return jnp.asarray(w, BF16), jnp.asarray(np.stack(masks).astype(np.float32))


def _hg_kernel(*refs, rev, nchunks):
    if rev:
        cq_ref, cf_ref, ci_ref, lb_ref, w_ref, mask_ref, o_ref, st_ref = refs
    else:
        cq_ref, cf_ref, ci_ref, cg_ref, ob_ref, ng_ref, lb_ref, w_ref, mask_ref, o_ref, st_ref = refs
    c_sz = HG_CHUNK

    @pl.when(pl.program_id(1) == 0)
    def _():
        st_ref[...] = jnp.zeros_like(st_ref)

    lb = lb_ref[...]
    log_lb = jnp.log(lb)
    log_1m = jnp.log1p(-lb)
    row = lax.broadcasted_iota(jnp.int32, (c_sz, HG_DK), 0)
    edge = 0 if rev else c_sz - 1

    def body(step, carry):
        ci_ = (nchunks - 1 - step) if rev else step
        rows = pl.ds(pl.multiple_of(ci_ * c_sz, c_sz), c_sz)
        z = cf_ref[rows, :].astype(F32)
        log_sig = jnp.minimum(z, 0.0) - jnp.log1p(jnp.exp(-jnp.abs(z)))
        cc = log_1m + log_sig
        logf = jnp.maximum(log_lb, cc) + jnp.log1p(jnp.exp(-jnp.abs(log_lb - cc)))
        kk = (1.0 - lb) * jnp.exp(log_sig - z)
        x = cq_ref[rows, :].astype(F32)
        q = x / (1.0 + jnp.exp(-x))
        v = ci_ref[rows, :]
        hi = logf.astype(BF16)
        lo = (logf - hi.astype(F32)).astype(BF16)
        dec = jnp.exp(_dot(w_ref[...], hi) + _dot(w_ref[...], lo))
        for h in range(HG_HEADS):
            cols = slice(h * HG_DK, (h + 1) * HG_DK)
            qh, kh, vh = q[:, cols], kk[:, cols], v[:, cols]
            a = mask_ref[0] * _dot_nt(qh.astype(BF16), kh.astype(BF16))
            for lvl, m in enumerate(HG_LEVELS):
                is_q = ((row & m) == 0) if rev else ((row & m) != 0)
                sec = dec[(2 + lvl) * c_sz:(3 + lvl) * c_sz, cols]
                y = (jnp.where(is_q, qh, kh) * sec).astype(BF16)
                a = a + mask_ref[1 + lvl] * _dot_nt(y, y)
            st = st_ref[h]
            qd = (qh * dec[0:c_sz, cols]).astype(BF16)
            o = _dot(a.astype(BF16), vh) + _dot_nt(qd, st.astype(BF16))
            kd = (kh * dec[c_sz:2 * c_sz, cols]).astype(BF16)
            st_ref[h] = st * dec[edge:edge + 1, cols] + _dot_tn(vh, kd)
            if rev:
                o_ref[rows, cols] = o
            else:
                tot = o + ob_ref[rows, cols]
                ms = jnp.mean(tot * tot, axis=-1, keepdims=True)
                g = cg_ref[rows, cols].astype(F32)
                o_ref[rows, cols] = (tot * lax.rsqrt(ms + NORM_EPS) * ng_ref[...] * (g / (1.0 + jnp.exp(-g)))).astype(BF16)
        return carry

    lax.fori_loop(0, nchunks, body, 0)


def _hg_call(proj, lb, consts, b, t, row_base, rev, o_bwd=None, norm_g=None):
    tb = min(256, t)
    nt = t // tb
    base = row_base // tb
    width = HG_HEADS * HG_DK
    w, masks = consts

    def pspec(col):
        def imap(bi, i):
            return (base + bi * nt + (nt - 1 - i if rev else i), col)
        return pl.BlockSpec((tb, width), imap)

    def ospec():
        return pl.BlockSpec((tb, width), lambda bi, i: (bi * nt + (nt - 1 - i if rev else i), 0))

    const_specs = [pl.BlockSpec((1, width), lambda bi, i: (0, 0)),
                   pl.BlockSpec(w.shape, lambda bi, i: (0, 0)),
                   pl.BlockSpec(masks.shape, lambda bi, i: (0, 0, 0))]
    if rev:
        in_specs = [pspec(3), pspec(5), pspec(6)] + const_specs
        args = [proj, proj, proj, lb, w, masks]
        out_dtype = F32
    else:
        in_specs = [pspec(3), pspec(4), pspec(6), pspec(7), ospec(),
                    pl.BlockSpec((1, HG_DK), lambda bi, i: (0, 0))] + const_specs
        args = [proj, proj, proj, proj, o_bwd, norm_g, lb, w, masks]
        out_dtype = BF16
    return pl.pallas_call(
        functools.partial(_hg_kernel, rev=rev, nchunks=tb // HG_CHUNK),
        grid=(b, nt),
        in_specs=in_specs,
        out_specs=ospec(),
        out_shape=jax.ShapeDtypeStruct((b * t, width), out_dtype),
        scratch_shapes=[pltpu.VMEM((HG_HEADS, HG_DK, HG_DK), F32)],
        compiler_params=_cparams("parallel", "arbitrary"),
        name="hgrn_bwd" if rev else "hgrn_fwd",
    )(*args)


def _outproj_kernel(oa_ref, ob_ref, oc_ref, x_ref, w_ref, g_ref, x1_ref, h_ref):
    y = _dot(oa_ref[...], w_ref[0:SEG, :]) + _dot(ob_ref[...], w_ref[SEG:2 * SEG, :]) \
        + _dot(oc_ref[...], w_ref[2 * SEG:, :])
    x1 = x_ref[...] + y
    x1_ref[...] = x1
    ms = jnp.mean(x1 * x1, axis=-1, keepdims=True)
    h_ref[...] = (x1 * lax.rsqrt(ms + NORM_EPS) * g_ref[...]).astype(BF16)


def _out_proj(oa, ob, oc, x, w_out, ln_g, tm):
    n = x.shape[0]
    row = lambda i: (i, 0)
    return pl.pallas_call(
        _outproj_kernel,
        grid=(n // tm,),
        in_specs=[pl.BlockSpec((tm, SEG), row), pl.BlockSpec((tm, SEG), row), pl.BlockSpec((tm, 2 * SEG), row),
                  pl.BlockSpec((tm, D_MODEL), row),
                  pl.BlockSpec((D_MODEL, D_MODEL), lambda i: (0, 0)),
                  pl.BlockSpec((1, D_MODEL), lambda i: (0, 0))],
        out_specs=[pl.BlockSpec((tm, D_MODEL), row), pl.BlockSpec((tm, D_MODEL), row)],
        out_shape=[jax.ShapeDtypeStruct((n, D_MODEL), F32), jax.ShapeDtypeStruct((n, D_MODEL), BF16)],
        compiler_params=_cparams("parallel"),
        name="out_proj",
    )(oa, ob, oc, x, w_out, ln_g)


FF_HALO = 16


def _ffn_kernel(x_ref, h_ref, hp_ref, hn_ref, wup_ref, cw_ref, cb_ref, wdn_ref, o_ref, *, tm, starts, ends):
    row0 = pl.program_id(0) * tm
    at_start = functools.reduce(jnp.logical_or, [row0 == s for s in starts])
    at_end = functools.reduce(jnp.logical_or, [row0 + tm == e for e in ends])
    keep_prev = jnp.where(at_start, 0.0, 1.0)
    keep_next = jnp.where(at_end, 0.0, 1.0)
    h = h_ref[...]
    hp = hp_ref[...]
    hn = hn_ref[...]
    rid = lax.broadcasted_iota(jnp.int32, (tm, FF_SEG), 0)
    o_ref[...] = x_ref[...]
    for j in range(D_FF // FF_SEG):
        cols = slice(j * FF_SEG, (j + 1) * FF_SEG)
        wg = wup_ref[:, cols]
        g = _dot(h, wg)
        g_prev = _dot(hp, wg)[FF_HALO - 1:FF_HALO, :] * keep_prev
        g_next = _dot(hn, wg)[0:1, :] * keep_next
        g_dn = jnp.where(rid == 0, g_prev, pltpu.roll(g, 1, 0))
        g_up = jnp.where(rid == tm - 1, g_next, pltpu.roll(g, tm - 1, 0))
        gate = g_dn * cw_ref[0:1, cols] + g * cw_ref[1:2, cols] + g_up * cw_ref[2:3, cols] + cb_ref[:, cols]
        up = _dot(h, wup_ref[:, D_FF + j * FF_SEG:D_FF + (j + 1) * FF_SEG])
        gelu = 0.5 * gate * (1.0 + jnp.tanh(math.sqrt(2.0 / math.pi) * (gate + 0.044715 * (gate * gate * gate))))
        o_ref[...] += _dot((gelu * up).astype(BF16), wdn_ref[cols, :])


def _ffn(x1, h2, w_up, conv_w, conv_b, w_down, tm, starts, ends):
    n = x1.shape[0]
    hpb = tm // FF_HALO
    nh = n // FF_HALO
    row = lambda i: (i, 0)
    const = lambda i: (0, 0)
    return pl.pallas_call(
        functools.partial(_ffn_kernel, tm=tm, starts=starts, ends=ends),
        grid=(n // tm,),
        in_specs=[pl.BlockSpec((tm, D_MODEL), row), pl.BlockSpec((tm, D_MODEL), row),
                  pl.BlockSpec((FF_HALO, D_MODEL), lambda i: (jnp.maximum(i * hpb - 1, 0), 0)),
                  pl.BlockSpec((FF_HALO, D_MODEL), lambda i: (jnp.minimum((i + 1) * hpb, nh - 1), 0)),
                  pl.BlockSpec((D_MODEL, 2 * D_FF), const),
                  pl.BlockSpec((3, D_FF), const), pl.BlockSpec((1, D_FF), const),
                  pl.BlockSpec((D_FF, D_MODEL), const)],
        out_specs=pl.BlockSpec((tm, D_MODEL), row),
        out_shape=jax.ShapeDtypeStruct((n, D_MODEL), F32),
        compiler_params=_cparams("parallel"),
        name="ffn",
    )(x1, h2, h2, h2, w_up, conv_w, conv_b, w_down)


def _lower_bounds(lb_logits):
    p = jax.nn.softmax(lb_logits.astype(F32), axis=1)
    c = jnp.cumsum(p, axis=1)
    return c - c[:, :1]


def _row_tile(groups):
    tm = 512
    while any(t % tm for _, t, _ in groups):
        tm //= 2
    return tm


def kernel(x_prompt, x_sample, ln_mix_g, w_in, na_q_g, na_k_g, na_rpb, da_q_g, da_k_g, t5_bias, hg_lb_logits,
           hg_norm_g, w_out, ln_ffn_g, w_up, conv_w, conv_b, w_down):
    depth = w_in.shape[0]
    groups = []
    base = 0
    for xg in (x_prompt, x_sample):
        b, t, _ = xg.shape
        groups.append((b, t, base))
        base += b * t
    starts = tuple(rb + i * t for b, t, rb in groups for i in range(b))
    ends = tuple(rb + (i + 1) * t for b, t, rb in groups for i in range(b))
    tm = _row_tile(groups)
    x = jnp.concatenate([x_prompt.reshape(-1, D_MODEL), x_sample.reshape(-1, D_MODEL)], axis=0)

    lb = _lower_bounds(hg_lb_logits)
    scale = HEAD_DIM ** -0.5
    tile4 = lambda g: jnp.tile(g.astype(F32), NA_HEADS)
    gm = jnp.asarray(np.kron(np.eye(SEG // HEAD_DIM), np.ones((HEAD_DIM, HEAD_DIM))) / HEAD_DIM, BF16)
    hg_fwd_consts = _hg_constants(False)
    hg_bwd_consts = _hg_constants(True)
    da_bias = {}

    for l in range(depth):
        qg = jnp.stack([tile4(na_q_g[l]) * scale, tile4(na_k_g[l]), tile4(da_q_g[l]) * scale, tile4(da_k_g[l])])
        proj = _in_proj(x, ln_mix_g[l].reshape(1, -1), w_in[l].astype(BF16), qg, gm, tm)
        na_bias = _na_bias_table(na_rpb[l])
        oa, ob, oc = [], [], []
        for b, t, rb in groups:
            oa.append(_na_call(proj, na_bias, b, t, rb))
            prev = []
            for window, d in DA_CONFIGS[:0:-1]:
                qb = min(128, t // d)
                if (d, qb) not in da_bias:
                    da_bias[(d, qb)] = _da_bias_table(t5_bias, d, qb)
                prev += _da_call(proj, da_bias[(d, qb)], b, t, rb, d)
            qb = min(128, t)
            if (1, qb) not in da_bias:
                da_bias[(1, qb)] = _da_bias_table(t5_bias, 1, qb)
            ob.append(_da_call(proj, da_bias[(1, qb)], b, t, rb, 1, prev=prev))
            o_bwd = _hg_call(proj, lb[1, l].reshape(1, -1), hg_bwd_consts, b, t, rb, True)
            oc.append(_hg_call(proj, lb[0, l].reshape(1, -1), hg_fwd_consts, b, t, rb, False,
                               o_bwd=o_bwd, norm_g=hg_norm_g[l].reshape(1, -1).astype(F32)))
        x1, h2 = _out_proj(jnp.concatenate(oa), jnp.concatenate(ob), jnp.concatenate(oc), x,
                           w_out[l].astype(BF16), ln_ffn_g[l].reshape(1, -1), tm)
        x = _ffn(x1, h2, w_up[l].astype(BF16), conv_w[l], conv_b[l].reshape(1, -1), w_down[l].astype(BF16),
                 tm, starts, ends)

    outs = []
    for (b, t, rb), xg in zip(groups, (x_prompt, x_sample)):
        outs.append(x[rb:rb + b * t].reshape(xg.shape))
    return tuple(outs)
```

```python
import functools
import math

import numpy as np
import jax
import jax.numpy as jnp
from jax import lax
from jax.experimental import pallas as pl
from jax.experimental.pallas import tpu as pltpu

F32 = jnp.float32
BF16 = jnp.bfloat16

D_MODEL = 1024
GRID_W = 64
HEAD_DIM = 64
NA_HEADS = 4
NA_WIN_ROWS = 8
NA_WIN_COLS = 16
DA_HEADS = 4
DA_CONFIGS = ((128, 1), (512, 4), (2048, 16))
DA_HALO = 64
N_BUCKETS = 32
MAX_DISTANCE = 1024
HG_HEADS = 4
HG_DK = 128
HG_CHUNK = 64
N_IN = 4096
D_FF = 2816
NORM_EPS = 1e-6
NEG_INF = -1e30
SEG = 256
FF_SEG = 256
VMEM_LIMIT = 56 * 1024 * 1024


def _cparams(*sem):
    return pltpu.CompilerParams(dimension_semantics=sem, vmem_limit_bytes=VMEM_LIMIT)


def _dot(a, b):
    return jnp.dot(a, b, preferred_element_type=F32)


def _dot_nt(a, b):
    return lax.dot_general(a, b, (((1,), (1,)), ((), ())), preferred_element_type=F32)


def _dot_tn(a, b):
    return lax.dot_general(a, b, (((0,), (0,)), ((), ())), preferred_element_type=F32)


DA_SEGS = (3, 4, 5)
DA_W = len(DA_SEGS) * SEG
DA_DILATED = tuple(d for _, d in DA_CONFIGS if d > 1)


def _inproj_kernel(x_ref, g_ref, w_ref, qg_ref, gm_ref, o_ref, *rest, tm):
    dil_refs, ybuf = rest[:-1], rest[-1]
    x = x_ref[...]
    ms = jnp.mean(x * x, axis=-1, keepdims=True)
    h = (x * lax.rsqrt(ms + NORM_EPS) * g_ref[...]).astype(BF16)
    normed = {0: 0, 1: 1, 3: 2, 4: 3}
    for seg in range(N_IN // SEG):
        y = _dot(h, w_ref[:, seg * SEG:(seg + 1) * SEG])
        if seg in normed:
            ss = _dot((y * y).astype(BF16), gm_ref[...])
            r = normed[seg]
            y = y * lax.rsqrt(ss + NORM_EPS) * qg_ref[r:r + 1, :]
        if seg in DA_SEGS:
            k = DA_SEGS.index(seg)
            for j in range(SEG // 128):
                ybuf[k * (SEG // 128) + j] = y[:, j * 128:(j + 1) * 128]
        o_ref[:, seg * SEG:(seg + 1) * SEG] = y.astype(BF16)
    for d, ref in zip(DA_DILATED, dil_refs):
        for rho in range(d):
            for j in range(DA_W // 128):
                ref[:, rho * DA_W + j * 128:rho * DA_W + (j + 1) * 128] = \
                    ybuf[j, pl.ds(rho, tm // d, stride=d), :].astype(BF16)


def _in_proj(x, ln_g, w_in, qg, gm, tm):
    n = x.shape[0]
    row = lambda i: (i, 0)
    const = lambda i: (0, 0)
    return pl.pallas_call(
        functools.partial(_inproj_kernel, tm=tm),
        grid=(n // tm,),
        in_specs=[
            pl.BlockSpec((tm, D_MODEL), row),
            pl.BlockSpec((1, D_MODEL), const),
            pl.BlockSpec((D_MODEL, N_IN), const),
            pl.BlockSpec((4, SEG), const),
            pl.BlockSpec((SEG, SEG), const),
        ],
        out_specs=[pl.BlockSpec((tm, N_IN), row)] + [pl.BlockSpec((tm // d, d * DA_W), row) for d in DA_DILATED],
        out_shape=[jax.ShapeDtypeStruct((n, N_IN), BF16)]
        + [jax.ShapeDtypeStruct((n // d, d * DA_W), BF16) for d in DA_DILATED],
        scratch_shapes=[pltpu.VMEM((DA_W // 128, tm, 128), F32)],
        compiler_params=_cparams("parallel"),
        name="in_proj",
    )(x, ln_g, w_in, qg, gm)


NA_RB = 8
NA_TOK = NA_RB * GRID_W
NA_KEYS = NA_WIN_ROWS * GRID_W


NA_NDR = 2 * NA_WIN_ROWS - 1


def _na_bias_table(rpb):
    cq = np.arange(GRID_W)
    ck = np.arange(GRID_W)
    c0 = np.clip(cq - NA_WIN_COLS // 2, 0, GRID_W - NA_WIN_COLS)
    ok = (ck[None, :] >= c0[:, None]) & (ck[None, :] < c0[:, None] + NA_WIN_COLS)
    dc = np.clip(ck[None, :] - cq[:, None], -(NA_WIN_COLS - 1), NA_WIN_COLS - 1) + NA_WIN_COLS - 1
    onehot = (dc[:, :, None] == np.arange(2 * NA_WIN_COLS - 1)).astype(np.float32)
    t = jnp.einsum('hab,qkb->haqk', rpb.astype(F32), onehot, precision=lax.Precision.HIGHEST)
    t = jnp.where(ok[None, None], t, NEG_INF)
    return jnp.concatenate([t[:, :-1], t[:, 1:]], axis=-1)


def _na_kernel(q_ref, kp_ref, kc_ref, kn_ref, vp_ref, vc_ref, vn_ref, bias_ref, o_ref, kbuf, vbuf, *, rows):
    i = pl.program_id(1)
    kbuf[0:NA_TOK] = kp_ref[...]
    kbuf[NA_TOK:2 * NA_TOK] = kc_ref[...]
    kbuf[2 * NA_TOK:3 * NA_TOK] = kn_ref[...]
    vbuf[0:NA_TOK] = vp_ref[...]
    vbuf[NA_TOK:2 * NA_TOK] = vc_ref[...]
    vbuf[2 * NA_TOK:3 * NA_TOK] = vn_ref[...]
    low = lax.broadcasted_iota(jnp.int32, (GRID_W, 128), 1) < HEAD_DIM
    for j in range(NA_RB):
        r = i * NA_RB + j
        r0 = jnp.clip(r - NA_WIN_ROWS // 2, 0, rows - NA_WIN_ROWS)
        e = r - r0
        off = pl.multiple_of((r0 - i * NA_RB + NA_RB) * GRID_W, GRID_W)
        q = q_ref[j * GRID_W:(j + 1) * GRID_W, :]
        for a in range(2):
            qa = q[:, a * 128:(a + 1) * 128]
            ka = kbuf[pl.ds(off, NA_KEYS), a * 128:(a + 1) * 128]
            va = vbuf[pl.ds(off, NA_KEYS), a * 128:(a + 1) * 128]
            outs = []
            for half in range(2):
                qm = jnp.where(low if half == 0 else ~low, qa, jnp.zeros_like(qa))
                bias = jnp.concatenate(
                    [bias_ref[2 * a + half, 2 * w2 + NA_WIN_ROWS - 1 - e] for w2 in range(NA_WIN_ROWS // 2)], axis=-1)
                s = _dot_nt(qm, ka) + bias
                m = jnp.max(s, axis=-1, keepdims=True)
                p = jnp.exp(s - m)
                l = jnp.sum(p, axis=-1, keepdims=True)
                outs.append(_dot(p.astype(BF16), va) / l)
            o_ref[j * GRID_W:(j + 1) * GRID_W, a * 128:(a + 1) * 128] = jnp.where(low, outs[0], outs[1]).astype(BF16)


def _na_call(proj, bias, b, t, row_base):
    rows = t // GRID_W
    nrb = rows // NA_RB
    base = row_base // NA_TOK

    def spec(col, shift):
        def imap(bi, i):
            return (base + bi * nrb + jnp.clip(i + shift, 0, nrb - 1), col)
        return pl.BlockSpec((NA_TOK, SEG), imap)

    return pl.pallas_call(
        functools.partial(_na_kernel, rows=rows),
        grid=(b, nrb),
        in_specs=[spec(0, 0), spec(1, -1), spec(1, 0), spec(1, 1), spec(2, -1), spec(2, 0), spec(2, 1),
                  pl.BlockSpec((NA_HEADS, NA_NDR - 1, GRID_W, 2 * GRID_W), lambda bi, i: (0, 0, 0, 0))],
        out_specs=pl.BlockSpec((NA_TOK, SEG), lambda bi, i: (bi * nrb + i, 0)),
        out_shape=jax.ShapeDtypeStruct((b * t, SEG), BF16),
        scratch_shapes=[pltpu.VMEM((3 * NA_TOK, SEG), BF16), pltpu.VMEM((3 * NA_TOK, SEG), BF16)],
        compiler_params=_cparams("parallel", "parallel"),
        name="na",
    )(proj, proj, proj, proj, proj, proj, proj, bias)


def _t5_bucket(rel):
    nb = N_BUCKETS // 2
    max_exact = nb // 2
    ret = np.where(rel > 0, nb, 0)
    n = np.abs(rel)
    large = max_exact + (np.log(np.maximum(n, 1) / max_exact)
                         / np.log(MAX_DISTANCE / max_exact) * (nb - max_exact)).astype(np.int32)
    large = np.minimum(large, nb - 1)
    return (ret + np.where(n < max_exact, n, large)).astype(np.int32)


def _da_bias_table(t5_bias, dilation, qb):
    kb = qb + 2 * DA_HALO
    rel = np.arange(kb)[None, :] - np.arange(qb)[:, None] - DA_HALO
    ok = np.abs(rel) <= DA_HALO
    onehot = (_t5_bucket(rel * dilation)[:, :, None] == np.arange(N_BUCKETS)).astype(np.float32)
    bias = jnp.einsum('qkb,bh->hqk', onehot, t5_bias.astype(F32), precision=lax.Precision.HIGHEST)
    return jnp.where(ok[None], bias, NEG_INF)


def _da_kernel(*refs, qb, length, prev_dils):
    final = bool(prev_dils)
    (q_ref, kl_ref, km_ref, kr_ref, vl_ref, vm_ref, vr_ref, bias_ref), refs = refs[:8], refs[8:]
    if final:
        npv = 2 * len(prev_dils)
        prev_refs, o_ref, kbuf, vbuf, und = refs[:npv], refs[npv], refs[npv + 1], refs[npv + 2], refs[npv + 3:]
    else:
        o_ref, lse_ref, kbuf, vbuf = refs
    c = pl.program_id(2)
    kb = qb + 2 * DA_HALO
    kbuf[0:DA_HALO] = kl_ref[...]
    kbuf[DA_HALO:DA_HALO + qb] = km_ref[...]
    kbuf[DA_HALO + qb:kb] = kr_ref[...]
    vbuf[0:DA_HALO] = vl_ref[...]
    vbuf[DA_HALO:DA_HALO + qb] = vm_ref[...]
    vbuf[DA_HALO + qb:kb] = vr_ref[...]
    if final:
        for k, d in enumerate(prev_dils):
            for part in range(2):
                src, dst = prev_refs[2 * k + part], und[2 * k + part]
                for rho in range(d):
                    for j in range(SEG // 128):
                        dst[j, pl.ds(rho, qb // d, stride=d), :] = \
                            src[:, rho * SEG + j * 128:rho * SEG + (j + 1) * 128]
    pos = c * qb - DA_HALO + lax.broadcasted_iota(jnp.int32, (qb, kb), 1)
    pos_ok = (pos >= 0) & (pos < length)
    low = lax.broadcasted_iota(jnp.int32, (qb, 128), 1) < HEAD_DIM
    q = q_ref[...]
    for a in range(2):
        qa = q[:, a * 128:(a + 1) * 128]
        ka = kbuf[:, a * 128:(a + 1) * 128]
        va = vbuf[:, a * 128:(a + 1) * 128]
        outs, lses = [], []
        for half in range(2):
            qm = jnp.where(low if half == 0 else ~low, qa, jnp.zeros_like(qa))
            s = jnp.where(pos_ok, _dot_nt(qm, ka) + bias_ref[2 * a + half], NEG_INF)
            m = jnp.max(s, axis=-1, keepdims=True)
            p = jnp.exp(s - m)
            l = jnp.sum(p, axis=-1, keepdims=True)
            outs.append(_dot(p.astype(BF16), va) / l)
            lses.append(jnp.broadcast_to(m + jnp.log(l), (qb, 128)))
        o = jnp.where(low, outs[0], outs[1])
        lse = jnp.where(low, lses[0], lses[1])
        cols = slice(a * 128, (a + 1) * 128)
        if final:
            mx = lse
            for k in range(len(prev_dils)):
                mx = jnp.maximum(mx, und[2 * k + 1][a])
            wsum = jnp.exp(lse - mx)
            mix = wsum * o
            for k in range(len(prev_dils)):
                wk = jnp.exp(und[2 * k + 1][a] - mx)
                wsum = wsum + wk
                mix = mix + wk * und[2 * k][a]
            o_ref[:, cols] = (mix / wsum).astype(BF16)
        else:
            o_ref[:, cols] = o
            lse_ref[:, cols] = lse


def _da_call(view, ncol, cols, bias, b, t, row_base, dilation, prev=None, prev_dils=()):
    d = dilation
    length = t // d
    qb = min(128, length)
    nqb = length // qb
    kb = qb + 2 * DA_HALO
    ubase = row_base // d
    hpb = qb // DA_HALO
    nhalo = view.shape[0] // DA_HALO
    final = prev is not None

    def main(col):
        return pl.BlockSpec((qb, SEG), lambda bi, r, c: ((ubase + bi * length) // qb + c, r * ncol + col))

    def halo(col, right):
        def imap(bi, r, c):
            blk = (ubase + bi * length) // DA_HALO + (c + 1) * hpb if right else \
                (ubase + bi * length) // DA_HALO + c * hpb - 1
            return (jnp.clip(blk, 0, nhalo - 1), r * ncol + col)
        return pl.BlockSpec((DA_HALO, SEG), imap)

    qc, kc, vc = cols
    in_specs = [main(qc), halo(kc, False), main(kc), halo(kc, True), halo(vc, False), main(vc), halo(vc, True),
                pl.BlockSpec((DA_HEADS, qb, kb), lambda bi, r, c: (0, 0, 0))]
    args = [view] * 7 + [bias]
    out_block = pl.BlockSpec((qb, SEG), lambda bi, r, c: (bi * nqb + c, r))
    scratch = [pltpu.VMEM((kb, SEG), BF16), pltpu.VMEM((kb, SEG), BF16)]
    if final:
        assert d == 1
        for pd in prev_dils:
            in_specs += [pl.BlockSpec((qb // pd, pd * SEG), lambda bi, r, c: (bi * nqb + c, 0))] * 2
            scratch += [pltpu.VMEM((SEG // 128, qb, 128), F32)] * 2
        args += list(prev)
        out_specs = out_block
        out_shape = jax.ShapeDtypeStruct((b * t, SEG), BF16)
    else:
        out_specs = [out_block, out_block]
        out_shape = [jax.ShapeDtypeStruct((b * t // d, d * SEG), F32)] * 2
    return pl.pallas_call(
        functools.partial(_da_kernel, qb=qb, length=length, prev_dils=tuple(prev_dils) if final else ()),
        grid=(b, d, nqb),
        in_specs=in_specs,
        out_specs=out_specs,
        out_shape=out_shape,
        scratch_shapes=scratch,
        compiler_params=_cparams("parallel", "parallel", "parallel"),
        name="da_d%d" % d,
    )(*args)


HG_LEVELS = (32, 16, 8, 4, 2, 1)
HG_NSEC = 2 + len(HG_LEVELS)


def _hg_constants(rev):
    c = HG_CHUNK
    t = np.arange(c)[:, None]
    j = np.arange(c)[None, :]
    secs = []
    if not rev:
        secs.append(j <= t)
        secs.append(j > t)
    else:
        secs.append(j >= t)
        secs.append(j < t)
    masks = [np.eye(c, dtype=bool)]
    s = np.arange(c)[None, :]
    for m in HG_LEVELS:
        p0 = (t // (2 * m)) * (2 * m)
        upper = (t & m) != 0
        if not rev:
            mid = p0 + m - 1
            sec = np.where(upper, (j > mid) & (j <= t), (j > t) & (j <= mid))
            pair = ((t // (2 * m)) == (s // (2 * m))) & upper & ((s & m) == 0)
        else:
            mid = p0 + m
            sec = np.where(upper, (j >= mid) & (j < t), (j >= t) & (j < mid))
            pair = ((t // (2 * m)) == (s // (2 * m))) & (~upper) & ((s & m) != 0)
        secs.append(sec)
        masks.append(pair)
    w = np.concatenate(secs, axis=0).astype(np.float32)
    return jnp.asarray(w, BF16), jnp.asarray(np.stack(masks).astype(np.float32))


def _hg_kernel(*refs, rev, nchunks):
    if rev:
        cq_ref, cf_ref, ci_ref, lb_ref, w_ref, mask_ref, o_ref, st_ref = refs
    else:
        cq_ref, cf_ref, ci_ref, cg_ref, ob_ref, ng_ref, lb_ref, w_ref, mask_ref, o_ref, st_ref = refs
    c_sz = HG_CHUNK

    @pl.when(pl.program_id(1) == 0)
    def _():
        st_ref[...] = jnp.zeros_like(st_ref)

    lb = lb_ref[...]
    log_lb = jnp.log(lb)
    log_1m = jnp.log1p(-lb)
    row = lax.broadcasted_iota(jnp.int32, (c_sz, HG_DK), 0)
    edge = 0 if rev else c_sz - 1

    def body(step, carry):
        ci_ = (nchunks - 1 - step) if rev else step
        rows = pl.ds(pl.multiple_of(ci_ * c_sz, c_sz), c_sz)
        z = cf_ref[rows, :].astype(F32)
        log_sig = jnp.minimum(z, 0.0) - jnp.log1p(jnp.exp(-jnp.abs(z)))
        cc = log_1m + log_sig
        logf = jnp.maximum(log_lb, cc) + jnp.log1p(jnp.exp(-jnp.abs(log_lb - cc)))
        kk = (1.0 - lb) * jnp.exp(log_sig - z)
        x = cq_ref[rows, :].astype(F32)
        q = x / (1.0 + jnp.exp(-x))
        v = ci_ref[rows, :]
        hi = logf.astype(BF16)
        lo = (logf - hi.astype(F32)).astype(BF16)
        dec = jnp.exp(_dot(w_ref[...], hi) + _dot(w_ref[...], lo))
        for h in range(HG_HEADS):
            cols = slice(h * HG_DK, (h + 1) * HG_DK)
            qh, kh, vh = q[:, cols], kk[:, cols], v[:, cols]
            a = mask_ref[0] * _dot_nt(qh.astype(BF16), kh.astype(BF16))
            for lvl, m in enumerate(HG_LEVELS):
                is_q = ((row & m) == 0) if rev else ((row & m) != 0)
                sec = dec[(2 + lvl) * c_sz:(3 + lvl) * c_sz, cols]
                y = (jnp.where(is_q, qh, kh) * sec).astype(BF16)
                a = a + mask_ref[1 + lvl] * _dot_nt(y, y)
            st = st_ref[h]
            qd = (qh * dec[0:c_sz, cols]).astype(BF16)
            o = _dot(a.astype(BF16), vh) + _dot_nt(qd, st.astype(BF16))
            kd = (kh * dec[c_sz:2 * c_sz, cols]).astype(BF16)
            st_ref[h] = st * dec[edge:edge + 1, cols] + _dot_tn(vh, kd)
            if rev:
                o_ref[rows, cols] = o
            else:
                tot = o + ob_ref[rows, cols]
                ms = jnp.mean(tot * tot, axis=-1, keepdims=True)
                g = cg_ref[rows, cols].astype(F32)
                o_ref[rows, cols] = (tot * lax.rsqrt(ms + NORM_EPS) * ng_ref[...] * (g / (1.0 + jnp.exp(-g)))).astype(BF16)
        return carry

    lax.fori_loop(0, nchunks, body, 0)


def _hg_call(proj, lb, consts, b, t, row_base, rev, o_bwd=None, norm_g=None):
    tb = min(256, t)
    nt = t // tb
    base = row_base // tb
    width = HG_HEADS * HG_DK
    w, masks = consts

    def pspec(col):
        def imap(bi, i):
            return (base + bi * nt + (nt - 1 - i if rev else i), col)
        return pl.BlockSpec((tb, width), imap)

    def ospec():
        return pl.BlockSpec((tb, width), lambda bi, i: (bi * nt + (nt - 1 - i if rev else i), 0))

    const_specs = [pl.BlockSpec((1, width), lambda bi, i: (0, 0)),
                   pl.BlockSpec(w.shape, lambda bi, i: (0, 0)),
                   pl.BlockSpec(masks.shape, lambda bi, i: (0, 0, 0))]
    if rev:
        in_specs = [pspec(3), pspec(5), pspec(6)] + const_specs
        args = [proj, proj, proj, lb, w, masks]
        out_dtype = F32
    else:
        in_specs = [pspec(3), pspec(4), pspec(6), pspec(7), ospec(),
                    pl.BlockSpec((1, HG_DK), lambda bi, i: (0, 0))] + const_specs
        args = [proj, proj, proj, proj, o_bwd, norm_g, lb, w, masks]
        out_dtype = BF16
    return pl.pallas_call(
        functools.partial(_hg_kernel, rev=rev, nchunks=tb // HG_CHUNK),
        grid=(b, nt),
        in_specs=in_specs,
        out_specs=ospec(),
        out_shape=jax.ShapeDtypeStruct((b * t, width), out_dtype),
        scratch_shapes=[pltpu.VMEM((HG_HEADS, HG_DK, HG_DK), F32)],
        compiler_params=_cparams("parallel", "arbitrary"),
        name="hgrn_bwd" if rev else "hgrn_fwd",
    )(*args)


def _outproj_kernel(oa_ref, ob_ref, oc_ref, x_ref, w_ref, g_ref, x1_ref, h_ref):
    y = _dot(oa_ref[...], w_ref[0:SEG, :]) + _dot(ob_ref[...], w_ref[SEG:2 * SEG, :]) \
        + _dot(oc_ref[...], w_ref[2 * SEG:, :])
    x1 = x_ref[...] + y
    x1_ref[...] = x1
    ms = jnp.mean(x1 * x1, axis=-1, keepdims=True)
    h_ref[...] = (x1 * lax.rsqrt(ms + NORM_EPS) * g_ref[...]).astype(BF16)


def _out_proj(oa, ob, oc, x, w_out, ln_g, tm):
    n = x.shape[0]
    row = lambda i: (i, 0)
    return pl.pallas_call(
        _outproj_kernel,
        grid=(n // tm,),
        in_specs=[pl.BlockSpec((tm, SEG), row), pl.BlockSpec((tm, SEG), row), pl.BlockSpec((tm, 2 * SEG), row),
                  pl.BlockSpec((tm, D_MODEL), row),
                  pl.BlockSpec((D_MODEL, D_MODEL), lambda i: (0, 0)),
                  pl.BlockSpec((1, D_MODEL), lambda i: (0, 0))],
        out_specs=[pl.BlockSpec((tm, D_MODEL), row), pl.BlockSpec((tm, D_MODEL), row)],
        out_shape=[jax.ShapeDtypeStruct((n, D_MODEL), F32), jax.ShapeDtypeStruct((n, D_MODEL), BF16)],
        compiler_params=_cparams("parallel"),
        name="out_proj",
    )(oa, ob, oc, x, w_out, ln_g)


FF_HALO = 16


def _ffn_kernel(x_ref, h_ref, hp_ref, hn_ref, wup_ref, cw_ref, cb_ref, wdn_ref, o_ref, *, tm, starts, ends):
    row0 = pl.program_id(0) * tm
    at_start = functools.reduce(jnp.logical_or, [row0 == s for s in starts])
    at_end = functools.reduce(jnp.logical_or, [row0 + tm == e for e in ends])
    keep_prev = jnp.where(at_start, 0.0, 1.0)
    keep_next = jnp.where(at_end, 0.0, 1.0)
    h = h_ref[...]
    hp = hp_ref[...]
    hn = hn_ref[...]
    rid = lax.broadcasted_iota(jnp.int32, (tm, FF_SEG), 0)
    o_ref[...] = x_ref[...]
    for j in range(D_FF // FF_SEG):
        cols = slice(j * FF_SEG, (j + 1) * FF_SEG)
        wg = wup_ref[:, cols]
        g = _dot(h, wg)
        g_prev = _dot(hp, wg)[FF_HALO - 1:FF_HALO, :] * keep_prev
        g_next = _dot(hn, wg)[0:1, :] * keep_next
        g_dn = jnp.where(rid == 0, g_prev, pltpu.roll(g, 1, 0))
        g_up = jnp.where(rid == tm - 1, g_next, pltpu.roll(g, tm - 1, 0))
        gate = g_dn * cw_ref[0:1, cols] + g * cw_ref[1:2, cols] + g_up * cw_ref[2:3, cols] + cb_ref[:, cols]
        up = _dot(h, wup_ref[:, D_FF + j * FF_SEG:D_FF + (j + 1) * FF_SEG])
        gelu = 0.5 * gate * (1.0 + jnp.tanh(math.sqrt(2.0 / math.pi) * (gate + 0.044715 * (gate * gate * gate))))
        o_ref[...] += _dot((gelu * up).astype(BF16), wdn_ref[cols, :])


def _ffn(x1, h2, w_up, conv_w, conv_b, w_down, tm, starts, ends):
    n = x1.shape[0]
    hpb = tm // FF_HALO
    nh = n // FF_HALO
    row = lambda i: (i, 0)
    const = lambda i: (0, 0)
    return pl.pallas_call(
        functools.partial(_ffn_kernel, tm=tm, starts=starts, ends=ends),
        grid=(n // tm,),
        in_specs=[pl.BlockSpec((tm, D_MODEL), row), pl.BlockSpec((tm, D_MODEL), row),
                  pl.BlockSpec((FF_HALO, D_MODEL), lambda i: (jnp.maximum(i * hpb - 1, 0), 0)),
                  pl.BlockSpec((FF_HALO, D_MODEL), lambda i: (jnp.minimum((i + 1) * hpb, nh - 1), 0)),
                  pl.BlockSpec((D_MODEL, 2 * D_FF), const),
                  pl.BlockSpec((3, D_FF), const), pl.BlockSpec((1, D_FF), const),
                  pl.BlockSpec((D_FF, D_MODEL), const)],
        out_specs=pl.BlockSpec((tm, D_MODEL), row),
        out_shape=jax.ShapeDtypeStruct((n, D_MODEL), F32),
        compiler_params=_cparams("parallel"),
        name="ffn",
    )(x1, h2, h2, h2, w_up, conv_w, conv_b, w_down)


def _lower_bounds(lb_logits):
    p = jax.nn.softmax(lb_logits.astype(F32), axis=1)
    c = jnp.cumsum(p, axis=1)
    return c - c[:, :1]


def _row_tile(groups):
    tm = 512
    while any(t % tm for _, t, _ in groups):
        tm //= 2
    return tm


def kernel(x_prompt, x_sample, ln_mix_g, w_in, na_q_g, na_k_g, na_rpb, da_q_g, da_k_g, t5_bias, hg_lb_logits,
           hg_norm_g, w_out, ln_ffn_g, w_up, conv_w, conv_b, w_down):
    depth = w_in.shape[0]
    groups = []
    base = 0
    for xg in (x_prompt, x_sample):
        b, t, _ = xg.shape
        groups.append((b, t, base))
        base += b * t
    starts = tuple(rb + i * t for b, t, rb in groups for i in range(b))
    ends = tuple(rb + (i + 1) * t for b, t, rb in groups for i in range(b))
    tm = _row_tile(groups)
    x = jnp.concatenate([x_prompt.reshape(-1, D_MODEL), x_sample.reshape(-1, D_MODEL)], axis=0)

    lb = _lower_bounds(hg_lb_logits)
    scale = HEAD_DIM ** -0.5
    tile4 = lambda g: jnp.tile(g.astype(F32), NA_HEADS)
    gm = jnp.asarray(np.kron(np.eye(SEG // HEAD_DIM), np.ones((HEAD_DIM, HEAD_DIM))) / HEAD_DIM, BF16)
    hg_fwd_consts = _hg_constants(False)
    hg_bwd_consts = _hg_constants(True)
    da_bias = {}

    def da_bias_for(d, qb):
        if (d, qb) not in da_bias:
            da_bias[(d, qb)] = _da_bias_table(t5_bias, d, qb)
        return da_bias[(d, qb)]

    for l in range(depth):
        qg = jnp.stack([tile4(na_q_g[l]) * scale, tile4(na_k_g[l]), tile4(da_q_g[l]) * scale, tile4(da_k_g[l])])
        proj, *dil_views = _in_proj(x, ln_mix_g[l].reshape(1, -1), w_in[l].astype(BF16), qg, gm, tm)
        na_bias = _na_bias_table(na_rpb[l])
        oa, ob, oc = [], [], []
        for b, t, rb in groups:
            oa.append(_na_call(proj, na_bias, b, t, rb))
            prev = []
            for d, view in zip(DA_DILATED, dil_views):
                prev += _da_call(view, len(DA_SEGS), (0, 1, 2), da_bias_for(d, min(128, t // d)), b, t, rb, d)
            ob.append(_da_call(proj, N_IN // SEG, DA_SEGS, da_bias_for(1, min(128, t)), b, t, rb, 1,
                               prev=prev, prev_dils=DA_DILATED))
            o_bwd = _hg_call(proj, lb[1, l].reshape(1, -1), hg_bwd_consts, b, t, rb, True)
            oc.append(_hg_call(proj, lb[0, l].reshape(1, -1), hg_fwd_consts, b, t, rb, False,
                               o_bwd=o_bwd, norm_g=hg_norm_g[l].reshape(1, -1).astype(F32)))
        x1, h2 = _out_proj(jnp.concatenate(oa), jnp.concatenate(ob), jnp.concatenate(oc), x,
                           w_out[l].astype(BF16), ln_ffn_g[l].reshape(1, -1), tm)
        x = _ffn(x1, h2, w_up[l].astype(BF16), conv_w[l], conv_b[l].reshape(1, -1), w_down[l].astype(BF16),
                 tm, starts, ends)

    outs = []
    for (b, t, rb), xg in zip(groups, (x_prompt, x_sample)):
        outs.append(x[rb:rb + b * t].reshape(xg.shape))
    return tuple(outs)
```

```python
import functools
import math

import numpy as np
import jax
import jax.numpy as jnp
from jax import lax
from jax.experimental import pallas as pl
from jax.experimental.pallas import tpu as pltpu

F32 = jnp.float32
BF16 = jnp.bfloat16

D_MODEL = 1024
GRID_W = 64
HEAD_DIM = 64
NA_HEADS = 4
NA_WIN_ROWS = 8
NA_WIN_COLS = 16
DA_HEADS = 4
DA_CONFIGS = ((128, 1), (512, 4), (2048, 16))
DA_HALO = 64
N_BUCKETS = 32
MAX_DISTANCE = 1024
HG_HEADS = 4
HG_DK = 128
HG_CHUNK = 64
N_IN = 4096
D_FF = 2816
NORM_EPS = 1e-6
NEG_INF = -1e30
SEG = 256
FF_SEG = 256
VMEM_LIMIT = 56 * 1024 * 1024


def _cparams(*sem):
    return pltpu.CompilerParams(dimension_semantics=sem, vmem_limit_bytes=VMEM_LIMIT)


def _dot(a, b):
    return jnp.dot(a, b, preferred_element_type=F32)


def _dot_nt(a, b):
    return lax.dot_general(a, b, (((1,), (1,)), ((), ())), preferred_element_type=F32)


def _dot_tn(a, b):
    return lax.dot_general(a, b, (((0,), (0,)), ((), ())), preferred_element_type=F32)


DA_SEGS = (3, 4, 5)
DA_W = len(DA_SEGS) * SEG
DA_DILATED = tuple(d for _, d in DA_CONFIGS if d > 1)


def _inproj_kernel(x_ref, g_ref, w_ref, qg_ref, gm_ref, o_ref, *rest, tm):
    dil_refs, ybuf = rest[:-1], rest[-1]
    x = x_ref[...]
    ms = jnp.mean(x * x, axis=-1, keepdims=True)
    h = (x * lax.rsqrt(ms + NORM_EPS) * g_ref[...]).astype(BF16)
    normed = {0: 0, 1: 1, 3: 2, 4: 3}
    for seg in range(N_IN // SEG):
        y = _dot(h, w_ref[:, seg * SEG:(seg + 1) * SEG])
        if seg in normed:
            ss = _dot((y * y).astype(BF16), gm_ref[...])
            r = normed[seg]
            y = y * lax.rsqrt(ss + NORM_EPS) * qg_ref[r:r + 1, :]
        if seg in DA_SEGS:
            k = DA_SEGS.index(seg)
            for j in range(SEG // 128):
                ybuf[k * (SEG // 128) + j] = y[:, j * 128:(j + 1) * 128]
        o_ref[:, seg * SEG:(seg + 1) * SEG] = y.astype(BF16)
    for d, ref in zip(DA_DILATED, dil_refs):
        for rho in range(d):
            for j in range(DA_W // 128):
                ref[:, rho * DA_W + j * 128:rho * DA_W + (j + 1) * 128] = \
                    ybuf[j, pl.ds(rho, tm // d, stride=d), :].astype(BF16)


def _in_proj(x, ln_g, w_in, qg, gm, tm):
    n = x.shape[0]
    row = lambda i: (i, 0)
    const = lambda i: (0, 0)
    return pl.pallas_call(
        functools.partial(_inproj_kernel, tm=tm),
        grid=(n // tm,),
        in_specs=[
            pl.BlockSpec((tm, D_MODEL), row),
            pl.BlockSpec((1, D_MODEL), const),
            pl.BlockSpec((D_MODEL, N_IN), const),
            pl.BlockSpec((4, SEG), const),
            pl.BlockSpec((SEG, SEG), const),
        ],
        out_specs=[pl.BlockSpec((tm, N_IN), row)] + [pl.BlockSpec((tm // d, d * DA_W), row) for d in DA_DILATED],
        out_shape=[jax.ShapeDtypeStruct((n, N_IN), BF16)]
        + [jax.ShapeDtypeStruct((n // d, d * DA_W), BF16) for d in DA_DILATED],
        scratch_shapes=[pltpu.VMEM((DA_W // 128, tm, 128), F32)],
        compiler_params=_cparams("parallel"),
        name="in_proj",
    )(x, ln_g, w_in, qg, gm)


NA_RB = 8
NA_TOK = NA_RB * GRID_W
NA_KEYS = NA_WIN_ROWS * GRID_W
NA_NDR = 2 * NA_WIN_ROWS - 1


def _na_bias_table(rpb):
    cq = np.arange(GRID_W)
    ck = np.arange(GRID_W)
    c0 = np.clip(cq - NA_WIN_COLS // 2, 0, GRID_W - NA_WIN_COLS)
    ok = (ck[None, :] >= c0[:, None]) & (ck[None, :] < c0[:, None] + NA_WIN_COLS)
    dc = np.clip(ck[None, :] - cq[:, None], -(NA_WIN_COLS - 1), NA_WIN_COLS - 1) + NA_WIN_COLS - 1
    onehot = (dc[:, :, None] == np.arange(2 * NA_WIN_COLS - 1)).astype(np.float32)
    t = jnp.einsum('hab,qkb->haqk', rpb.astype(F32), onehot, precision=lax.Precision.HIGHEST)
    t = jnp.where(ok[None, None], t, NEG_INF)
    return jnp.concatenate([t[:, :-1], t[:, 1:]], axis=-1)


def _na_kernel(q_ref, kp_ref, kc_ref, kn_ref, vp_ref, vc_ref, vn_ref, bias_ref, o_ref, kbuf, vbuf, *, rows):
    i = pl.program_id(1)
    kbuf[0:NA_TOK] = kp_ref[...]
    kbuf[NA_TOK:2 * NA_TOK] = kc_ref[...]
    kbuf[2 * NA_TOK:3 * NA_TOK] = kn_ref[...]
    vbuf[0:NA_TOK] = vp_ref[...]
    vbuf[NA_TOK:2 * NA_TOK] = vc_ref[...]
    vbuf[2 * NA_TOK:3 * NA_TOK] = vn_ref[...]
    low = lax.broadcasted_iota(jnp.int32, (GRID_W, 128), 1) < HEAD_DIM
    for j in range(NA_RB):
        r = i * NA_RB + j
        r0 = jnp.clip(r - NA_WIN_ROWS // 2, 0, rows - NA_WIN_ROWS)
        e = r - r0
        off = pl.multiple_of((r0 - i * NA_RB + NA_RB) * GRID_W, GRID_W)
        q = q_ref[j * GRID_W:(j + 1) * GRID_W, :]
        for a in range(2):
            qa = q[:, a * 128:(a + 1) * 128]
            ka = kbuf[pl.ds(off, NA_KEYS), a * 128:(a + 1) * 128]
            va = vbuf[pl.ds(off, NA_KEYS), a * 128:(a + 1) * 128]
            outs = []
            for half in range(2):
                qm = jnp.where(low if half == 0 else ~low, qa, jnp.zeros_like(qa))
                bias = jnp.concatenate(
                    [bias_ref[2 * a + half, 2 * w2 + NA_WIN_ROWS - 1 - e] for w2 in range(NA_WIN_ROWS // 2)], axis=-1)
                s = _dot_nt(qm, ka) + bias
                m = jnp.max(s, axis=-1, keepdims=True)
                p = jnp.exp(s - m)
                l = jnp.sum(p, axis=-1, keepdims=True)
                outs.append(_dot(p.astype(BF16), va) / l)
            o_ref[j * GRID_W:(j + 1) * GRID_W, a * 128:(a + 1) * 128] = jnp.where(low, outs[0], outs[1]).astype(BF16)


def _na_call(proj, bias, b, t, row_base):
    rows = t // GRID_W
    nrb = rows // NA_RB
    base = row_base // NA_TOK

    def spec(col, shift):
        def imap(bi, i):
            return (base + bi * nrb + jnp.clip(i + shift, 0, nrb - 1), col)
        return pl.BlockSpec((NA_TOK, SEG), imap)

    return pl.pallas_call(
        functools.partial(_na_kernel, rows=rows),
        grid=(b, nrb),
        in_specs=[spec(0, 0), spec(1, -1), spec(1, 0), spec(1, 1), spec(2, -1), spec(2, 0), spec(2, 1),
                  pl.BlockSpec((NA_HEADS, NA_NDR - 1, GRID_W, 2 * GRID_W), lambda bi, i: (0, 0, 0, 0))],
        out_specs=pl.BlockSpec((NA_TOK, SEG), lambda bi, i: (bi * nrb + i, 0)),
        out_shape=jax.ShapeDtypeStruct((b * t, SEG), BF16),
        scratch_shapes=[pltpu.VMEM((3 * NA_TOK, SEG), BF16), pltpu.VMEM((3 * NA_TOK, SEG), BF16)],
        compiler_params=_cparams("parallel", "parallel"),
        name="na",
    )(proj, proj, proj, proj, proj, proj, proj, bias)


def _t5_bucket(rel):
    nb = N_BUCKETS // 2
    max_exact = nb // 2
    ret = np.where(rel > 0, nb, 0)
    n = np.abs(rel)
    large = max_exact + (np.log(np.maximum(n, 1) / max_exact)
                         / np.log(MAX_DISTANCE / max_exact) * (nb - max_exact)).astype(np.int32)
    large = np.minimum(large, nb - 1)
    return (ret + np.where(n < max_exact, n, large)).astype(np.int32)


def _da_bias_table(t5_bias, dilation, qb):
    kb = qb + 2 * DA_HALO
    rel = np.arange(kb)[None, :] - np.arange(qb)[:, None] - DA_HALO
    ok = np.abs(rel) <= DA_HALO
    onehot = (_t5_bucket(rel * dilation)[:, :, None] == np.arange(N_BUCKETS)).astype(np.float32)
    bias = jnp.einsum('qkb,bh->hqk', onehot, t5_bias.astype(F32), precision=lax.Precision.HIGHEST)
    return jnp.where(ok[None], bias, NEG_INF)


def _da_kernel(*refs, qb, length, prev_dils):
    final = bool(prev_dils)
    (q_ref, kl_ref, km_ref, kr_ref, vl_ref, vm_ref, vr_ref, bias_ref), refs = refs[:8], refs[8:]
    if final:
        npv = 2 * len(prev_dils)
        prev_refs, o_ref, kbuf, vbuf, und = refs[:npv], refs[npv], refs[npv + 1], refs[npv + 2], refs[npv + 3:]
    else:
        o_ref, lse_ref, kbuf, vbuf = refs
    c = pl.program_id(2)
    kb = qb + 2 * DA_HALO
    kbuf[0:DA_HALO] = kl_ref[...]
    kbuf[DA_HALO:DA_HALO + qb] = km_ref[...]
    kbuf[DA_HALO + qb:kb] = kr_ref[...]
    vbuf[0:DA_HALO] = vl_ref[...]
    vbuf[DA_HALO:DA_HALO + qb] = vm_ref[...]
    vbuf[DA_HALO + qb:kb] = vr_ref[...]
    if final:
        for k, d in enumerate(prev_dils):
            for part in range(2):
                src, dst = prev_refs[2 * k + part], und[2 * k + part]
                for rho in range(d):
                    for j in range(SEG // 128):
                        dst[j, pl.ds(rho, qb // d, stride=d), :] = \
                            src[:, rho * SEG + j * 128:rho * SEG + (j + 1) * 128]
    pos = c * qb - DA_HALO + lax.broadcasted_iota(jnp.int32, (qb, kb), 1)
    pos_ok = (pos >= 0) & (pos < length)
    low = lax.broadcasted_iota(jnp.int32, (qb, 128), 1) < HEAD_DIM
    q = q_ref[...]
    for a in range(2):
        qa = q[:, a * 128:(a + 1) * 128]
        ka = kbuf[:, a * 128:(a + 1) * 128]
        va = vbuf[:, a * 128:(a + 1) * 128]
        outs, lses = [], []
        for half in range(2):
            qm = jnp.where(low if half == 0 else ~low, qa, jnp.zeros_like(qa))
            s = jnp.where(pos_ok, _dot_nt(qm, ka) + bias_ref[2 * a + half], NEG_INF)
            m = jnp.max(s, axis=-1, keepdims=True)
            p = jnp.exp(s - m)
            l = jnp.sum(p, axis=-1, keepdims=True)
            outs.append(_dot(p.astype(BF16), va) / l)
            lses.append(jnp.broadcast_to(m + jnp.log(l), (qb, 128)))
        o = jnp.where(low, outs[0], outs[1])
        lse = jnp.where(low, lses[0], lses[1])
        cols = slice(a * 128, (a + 1) * 128)
        if final:
            mx = lse
            for k in range(len(prev_dils)):
                mx = jnp.maximum(mx, und[2 * k + 1][a])
            wsum = jnp.exp(lse - mx)
            mix = wsum * o
            for k in range(len(prev_dils)):
                wk = jnp.exp(und[2 * k + 1][a] - mx)
                wsum = wsum + wk
                mix = mix + wk * und[2 * k][a]
            o_ref[:, cols] = (mix / wsum).astype(BF16)
        else:
            o_ref[:, cols] = o
            lse_ref[:, cols] = lse


def _da_call(view, ncol, cols, bias, b, t, row_base, dilation, prev=None, prev_dils=()):
    d = dilation
    length = t // d
    qb = min(128, length)
    nqb = length // qb
    kb = qb + 2 * DA_HALO
    ubase = row_base // d
    hpb = qb // DA_HALO
    nhalo = view.shape[0] // DA_HALO
    final = prev is not None

    def main(col):
        return pl.BlockSpec((qb, SEG), lambda bi, r, c: ((ubase + bi * length) // qb + c, r * ncol + col))

    def halo(col, right):
        def imap(bi, r, c):
            blk = (ubase + bi * length) // DA_HALO + (c + 1) * hpb if right else \
                (ubase + bi * length) // DA_HALO + c * hpb - 1
            return (jnp.clip(blk, 0, nhalo - 1), r * ncol + col)
        return pl.BlockSpec((DA_HALO, SEG), imap)

    qc, kc, vc = cols
    in_specs = [main(qc), halo(kc, False), main(kc), halo(kc, True), halo(vc, False), main(vc), halo(vc, True),
                pl.BlockSpec((DA_HEADS, qb, kb), lambda bi, r, c: (0, 0, 0))]
    args = [view] * 7 + [bias]
    out_block = pl.BlockSpec((qb, SEG), lambda bi, r, c: (bi * nqb + c, r))
    scratch = [pltpu.VMEM((kb, SEG), BF16), pltpu.VMEM((kb, SEG), BF16)]
    if final:
        assert d == 1
        for pd in prev_dils:
            in_specs += [pl.BlockSpec((qb // pd, pd * SEG), lambda bi, r, c: (bi * nqb + c, 0))] * 2
            scratch += [pltpu.VMEM((SEG // 128, qb, 128), F32)] * 2
        args += list(prev)
        out_specs = out_block
        out_shape = jax.ShapeDtypeStruct((b * t, SEG), BF16)
    else:
        out_specs = [out_block, out_block]
        out_shape = [jax.ShapeDtypeStruct((b * t // d, d * SEG), F32)] * 2
    return pl.pallas_call(
        functools.partial(_da_kernel, qb=qb, length=length, prev_dils=tuple(prev_dils) if final else ()),
        grid=(b, d, nqb),
        in_specs=in_specs,
        out_specs=out_specs,
        out_shape=out_shape,
        scratch_shapes=scratch,
        compiler_params=_cparams("parallel", "parallel", "parallel"),
        name="da_d%d" % d,
    )(*args)


HG_LEVELS = (32, 16, 8, 4, 2, 1)
HG_NSEC = 2 + len(HG_LEVELS)
HG_W = HG_HEADS * HG_DK
HG_PAIR = 2 * HG_DK
LOG2_E = math.log2(math.e)


def _hg_constants(rev):
    c = HG_CHUNK
    t = np.arange(c)[:, None]
    j = np.arange(c)[None, :]
    secs = []
    if not rev:
        secs.append(j <= t)
        secs.append(j > t)
    else:
        secs.append(j >= t)
        secs.append(j < t)
    masks = [np.eye(c, dtype=bool)]
    s = np.arange(c)[None, :]
    for m in HG_LEVELS:
        p0 = (t // (2 * m)) * (2 * m)
        upper = (t & m) != 0
        if not rev:
            mid = p0 + m - 1
            sec = np.where(upper, (j > mid) & (j <= t), (j > t) & (j <= mid))
            pair = ((t // (2 * m)) == (s // (2 * m))) & upper & ((s & m) == 0)
        else:
            mid = p0 + m
            sec = np.where(upper, (j >= mid) & (j < t), (j >= t) & (j < mid))
            pair = ((t // (2 * m)) == (s // (2 * m))) & (~upper) & ((s & m) != 0)
        secs.append(sec)
        masks.append(pair)
    w = np.concatenate(secs, axis=0).astype(np.float32)
    masks = np.tile(np.stack(masks).astype(np.float32), (1, 1, 2))
    return jnp.asarray(w, BF16), jnp.asarray(masks)


def _bdiag(h0, h1):
    zero = jnp.zeros_like(h0)
    return jnp.concatenate([jnp.concatenate([h0, zero], axis=1), jnp.concatenate([zero, h1], axis=1)], axis=0)


def _hg_pair_operand(a, p):
    return _bdiag(a[:, (2 * p) * HG_DK:(2 * p + 1) * HG_DK], a[:, (2 * p + 1) * HG_DK:(2 * p + 2) * HG_DK])


def _hg_chunk(rev, z, x, v, lb, w_ref, mask_ref, st_ref):
    c_sz = HG_CHUNK
    log_lb = jnp.log(lb)
    log_1m = jnp.log1p(-lb)
    log_sig = jnp.minimum(z, 0.0) - jnp.log(1.0 + jnp.exp(-jnp.abs(z)))
    cc = log_1m + log_sig
    logf = jnp.maximum(log_lb, cc) + jnp.log(1.0 + jnp.exp(-jnp.abs(log_lb - cc)))
    kk = (1.0 - lb) * jnp.exp(log_sig - z)
    q = x / (1.0 + jnp.exp(-x))
    log2f = logf * LOG2_E
    hi = log2f.astype(BF16)
    lo = (log2f - hi.astype(F32)).astype(BF16)
    dec = jnp.exp2(_dot(w_ref[...], hi) + _dot(w_ref[...], lo))
    sub = lax.broadcasted_iota(jnp.int32, (1, 8, HG_W), 1)
    ys = []
    for lvl, m in enumerate(HG_LEVELS):
        if m >= 8:
            sel = jnp.concatenate(
                [(q if (((8 * r) & m) != 0) != rev else kk)[8 * r:8 * r + 8] for r in range(c_sz // 8)], axis=0)
        else:
            is_q = ((sub & m) == 0) if rev else ((sub & m) != 0)
            sel = jnp.where(is_q, q.reshape(c_sz // 8, 8, HG_W), kk.reshape(c_sz // 8, 8, HG_W)).reshape(c_sz, HG_W)
        ys.append((sel * dec[(2 + lvl) * c_sz:(3 + lvl) * c_sz]).astype(BF16))
    q16, k16 = q.astype(BF16), kk.astype(BF16)
    qd = (q * dec[0:c_sz]).astype(BF16)
    kd = (kk * dec[c_sz:2 * c_sz]).astype(BF16)
    edge = 0 if rev else c_sz - 1
    outs = []
    for p in range(HG_HEADS // 2):
        cols = slice(p * HG_PAIR, (p + 1) * HG_PAIR)
        a2 = mask_ref[0] * _dot_nt(q16[:, cols], _hg_pair_operand(k16, p))
        for lvl in range(len(HG_LEVELS)):
            a2 = a2 + mask_ref[1 + lvl] * _dot_nt(ys[lvl][:, cols], _hg_pair_operand(ys[lvl], p))
        st = _bdiag(st_ref[2 * p].astype(BF16), st_ref[2 * p + 1].astype(BF16))
        outs.append(_dot(a2.astype(BF16), _hg_pair_operand(v, p)) + _dot_nt(qd[:, cols], st))
        for h in (2 * p, 2 * p + 1):
            hc = slice(h * HG_DK, (h + 1) * HG_DK)
            st_ref[h] = st_ref[h] * dec[edge:edge + 1, hc] + _dot_tn(v[:, hc], kd[:, hc])
    return jnp.concatenate(outs, axis=1)


def _hg_kernel(cqf_ref, cff_ref, cif_ref, cqb_ref, cfb_ref, cib_ref, lb_ref, wf_ref, wb_ref, mf_ref, mb_ref,
               of_ref, ob_ref, st_ref, *, nchunks):
    @pl.when(pl.program_id(1) == 0)
    def _():
        st_ref[...] = jnp.zeros_like(st_ref)

    def body(step, carry):
        rf = pl.ds(pl.multiple_of(step * HG_CHUNK, HG_CHUNK), HG_CHUNK)
        rb = pl.ds(pl.multiple_of((nchunks - 1 - step) * HG_CHUNK, HG_CHUNK), HG_CHUNK)
        of_ref[rf, :] = _hg_chunk(False, cff_ref[rf, :].astype(F32), cqf_ref[rf, :].astype(F32), cif_ref[rf, :],
                                  lb_ref[0:1, :], wf_ref, mf_ref, st_ref.at[0]).astype(BF16)
        ob_ref[rb, :] = _hg_chunk(True, cfb_ref[rb, :].astype(F32), cqb_ref[rb, :].astype(F32), cib_ref[rb, :],
                                  lb_ref[1:2, :], wb_ref, mb_ref, st_ref.at[1]).astype(BF16)
        return carry

    lax.fori_loop(0, nchunks, body, 0, unroll=True)


def _hg_call(proj, lb, consts_f, consts_b, b, t, row_base):
    tb = min(256, t)
    nt = t // tb
    base = row_base // tb

    def pspec(col, rev):
        return pl.BlockSpec((tb, HG_W), lambda bi, i: (base + bi * nt + (nt - 1 - i if rev else i), col))

    def ospec(rev):
        return pl.BlockSpec((tb, HG_W), lambda bi, i: (bi * nt + (nt - 1 - i if rev else i), 0))

    def const(a):
        return pl.BlockSpec(a.shape, lambda bi, i: (0,) * a.ndim)

    (wf, mf), (wb, mb) = consts_f, consts_b
    return pl.pallas_call(
        functools.partial(_hg_kernel, nchunks=tb // HG_CHUNK),
        grid=(b, nt),
        in_specs=[pspec(3, False), pspec(4, False), pspec(6, False), pspec(3, True), pspec(5, True), pspec(6, True),
                  const(lb), const(wf), const(wb), const(mf), const(mb)],
        out_specs=[ospec(False), ospec(True)],
        out_shape=[jax.ShapeDtypeStruct((b * t, HG_W), BF16)] * 2,
        scratch_shapes=[pltpu.VMEM((2, HG_HEADS, HG_DK, HG_DK), F32)],
        compiler_params=_cparams("parallel", "arbitrary"),
        name="hgrn",
    )(proj, proj, proj, proj, proj, proj, lb, wf, wb, mf, mb)


def _outproj_kernel(oa_ref, ob_ref, cf_ref, cb_ref, cg_ref, ng_ref, x_ref, w_ref, g_ref, x1_ref, h_ref):
    tot = cf_ref[...].astype(F32) + cb_ref[...].astype(F32)
    gate = cg_ref[...].astype(F32)
    parts = []
    for h in range(HG_HEADS):
        th = tot[:, h * HG_DK:(h + 1) * HG_DK]
        ms = jnp.mean(th * th, axis=-1, keepdims=True)
        parts.append(th * lax.rsqrt(ms + NORM_EPS) * ng_ref[...])
    oc = (jnp.concatenate(parts, axis=1) * (gate / (1.0 + jnp.exp(-gate)))).astype(BF16)
    y = _dot(oa_ref[...], w_ref[0:SEG, :]) + _dot(ob_ref[...], w_ref[SEG:2 * SEG, :]) + _dot(oc, w_ref[2 * SEG:, :])
    x1 = x_ref[...] + y
    x1_ref[...] = x1
    ms = jnp.mean(x1 * x1, axis=-1, keepdims=True)
    h_ref[...] = (x1 * lax.rsqrt(ms + NORM_EPS) * g_ref[...]).astype(BF16)


def _out_proj(oa, ob, ocf, ocb, proj, norm_g, x, w_out, ln_g, tm):
    n = x.shape[0]
    row = lambda i: (i, 0)
    const = lambda i: (0, 0)
    return pl.pallas_call(
        _outproj_kernel,
        grid=(n // tm,),
        in_specs=[pl.BlockSpec((tm, SEG), row), pl.BlockSpec((tm, SEG), row),
                  pl.BlockSpec((tm, HG_W), row), pl.BlockSpec((tm, HG_W), row),
                  pl.BlockSpec((tm, HG_W), lambda i: (i, N_IN // HG_W - 1)),
                  pl.BlockSpec((1, HG_DK), const),
                  pl.BlockSpec((tm, D_MODEL), row),
                  pl.BlockSpec((D_MODEL, D_MODEL), const),
                  pl.BlockSpec((1, D_MODEL), const)],
        out_specs=[pl.BlockSpec((tm, D_MODEL), row), pl.BlockSpec((tm, D_MODEL), row)],
        out_shape=[jax.ShapeDtypeStruct((n, D_MODEL), F32), jax.ShapeDtypeStruct((n, D_MODEL), BF16)],
        compiler_params=_cparams("parallel"),
        name="out_proj",
    )(oa, ob, ocf, ocb, proj, norm_g, x, w_out, ln_g)


FF_HALO = 16


def _ffn_kernel(x_ref, h_ref, hp_ref, hn_ref, wup_ref, cw_ref, cb_ref, wdn_ref, o_ref, *, tm, starts, ends):
    row0 = pl.program_id(0) * tm
    at_start = functools.reduce(jnp.logical_or, [row0 == s for s in starts])
    at_end = functools.reduce(jnp.logical_or, [row0 + tm == e for e in ends])
    keep_prev = jnp.where(at_start, 0.0, 1.0)
    keep_next = jnp.where(at_end, 0.0, 1.0)
    h = h_ref[...]
    hp = hp_ref[...]
    hn = hn_ref[...]
    rid = lax.broadcasted_iota(jnp.int32, (tm, FF_SEG), 0)
    o_ref[...] = x_ref[...]
    for j in range(D_FF // FF_SEG):
        cols = slice(j * FF_SEG, (j + 1) * FF_SEG)
        wg = wup_ref[:, cols]
        g = _dot(h, wg)
        g_prev = _dot(hp, wg)[FF_HALO - 1:FF_HALO, :] * keep_prev
        g_next = _dot(hn, wg)[0:1, :] * keep_next
        g_dn = jnp.where(rid == 0, g_prev, pltpu.roll(g, 1, 0))
        g_up = jnp.where(rid == tm - 1, g_next, pltpu.roll(g, tm - 1, 0))
        gate = g_dn * cw_ref[0:1, cols] + g * cw_ref[1:2, cols] + g_up * cw_ref[2:3, cols] + cb_ref[:, cols]
        up = _dot(h, wup_ref[:, D_FF + j * FF_SEG:D_FF + (j + 1) * FF_SEG])
        gelu = 0.5 * gate * (1.0 + jnp.tanh(math.sqrt(2.0 / math.pi) * (gate + 0.044715 * (gate * gate * gate))))
        o_ref[...] += _dot((gelu * up).astype(BF16), wdn_ref[cols, :])


def _ffn(x1, h2, w_up, conv_w, conv_b, w_down, tm, starts, ends):
    n = x1.shape[0]
    hpb = tm // FF_HALO
    nh = n // FF_HALO
    row = lambda i: (i, 0)
    const = lambda i: (0, 0)
    return pl.pallas_call(
        functools.partial(_ffn_kernel, tm=tm, starts=starts, ends=ends),
        grid=(n // tm,),
        in_specs=[pl.BlockSpec((tm, D_MODEL), row), pl.BlockSpec((tm, D_MODEL), row),
                  pl.BlockSpec((FF_HALO, D_MODEL), lambda i: (jnp.maximum(i * hpb - 1, 0), 0)),
                  pl.BlockSpec((FF_HALO, D_MODEL), lambda i: (jnp.minimum((i + 1) * hpb, nh - 1), 0)),
                  pl.BlockSpec((D_MODEL, 2 * D_FF), const),
                  pl.BlockSpec((3, D_FF), const), pl.BlockSpec((1, D_FF), const),
                  pl.BlockSpec((D_FF, D_MODEL), const)],
        out_specs=pl.BlockSpec((tm, D_MODEL), row),
        out_shape=jax.ShapeDtypeStruct((n, D_MODEL), F32),
        compiler_params=_cparams("parallel"),
        name="ffn",
    )(x1, h2, h2, h2, w_up, conv_w, conv_b, w_down)


def _lower_bounds(lb_logits):
    p = jax.nn.softmax(lb_logits.astype(F32), axis=1)
    c = jnp.cumsum(p, axis=1)
    return c - c[:, :1]


def _row_tile(groups):
    tm = 512
    while any(t % tm for _, t, _ in groups):
        tm //= 2
    return tm


def kernel(x_prompt, x_sample, ln_mix_g, w_in, na_q_g, na_k_g, na_rpb, da_q_g, da_k_g, t5_bias, hg_lb_logits,
           hg_norm_g, w_out, ln_ffn_g, w_up, conv_w, conv_b, w_down):
    depth = w_in.shape[0]
    groups = []
    base = 0
    for xg in (x_prompt, x_sample):
        b, t, _ = xg.shape
        groups.append((b, t, base))
        base += b * t
    starts = tuple(rb + i * t for b, t, rb in groups for i in range(b))
    ends = tuple(rb + (i + 1) * t for b, t, rb in groups for i in range(b))
    tm = _row_tile(groups)
    x = jnp.concatenate([x_prompt.reshape(-1, D_MODEL), x_sample.reshape(-1, D_MODEL)], axis=0)

    lb = _lower_bounds(hg_lb_logits)
    scale = HEAD_DIM ** -0.5
    tile4 = lambda g: jnp.tile(g.astype(F32), NA_HEADS)
    gm = jnp.asarray(np.kron(np.eye(SEG // HEAD_DIM), np.ones((HEAD_DIM, HEAD_DIM))) / HEAD_DIM, BF16)
    hg_fwd_consts = _hg_constants(False)
    hg_bwd_consts = _hg_constants(True)
    da_bias = {}

    def da_bias_for(d, qb):
        if (d, qb) not in da_bias:
            da_bias[(d, qb)] = _da_bias_table(t5_bias, d, qb)
        return da_bias[(d, qb)]

    for l in range(depth):
        qg = jnp.stack([tile4(na_q_g[l]) * scale, tile4(na_k_g[l]), tile4(da_q_g[l]) * scale, tile4(da_k_g[l])])
        proj, *dil_views = _in_proj(x, ln_mix_g[l].reshape(1, -1), w_in[l].astype(BF16), qg, gm, tm)
        na_bias = _na_bias_table(na_rpb[l])
        oa, ob, ocf, ocb = [], [], [], []
        for b, t, rb in groups:
            oa.append(_na_call(proj, na_bias, b, t, rb))
            prev = []
            for d, view in zip(DA_DILATED, dil_views):
                prev += _da_call(view, len(DA_SEGS), (0, 1, 2), da_bias_for(d, min(128, t // d)), b, t, rb, d)
            ob.append(_da_call(proj, N_IN // SEG, DA_SEGS, da_bias_for(1, min(128, t)), b, t, rb, 1,
                               prev=prev, prev_dils=DA_DILATED))
            cf, cb = _hg_call(proj, lb[:, l], hg_fwd_consts, hg_bwd_consts, b, t, rb)
            ocf.append(cf)
            ocb.append(cb)
        x1, h2 = _out_proj(jnp.concatenate(oa), jnp.concatenate(ob), jnp.concatenate(ocf), jnp.concatenate(ocb),
                           proj, hg_norm_g[l].reshape(1, -1).astype(F32), x, w_out[l].astype(BF16),
                           ln_ffn_g[l].reshape(1, -1), tm)
        x = _ffn(x1, h2, w_up[l].astype(BF16), conv_w[l], conv_b[l].reshape(1, -1), w_down[l].astype(BF16),
                 tm, starts, ends)

    outs = []
    for (b, t, rb), xg in zip(groups, (x_prompt, x_sample)):
        outs.append(x[rb:rb + b * t].reshape(xg.shape))
    return tuple(outs)
```

```python
import functools
import math

import numpy as np
import jax
import jax.numpy as jnp
from jax import lax
from jax.experimental import pallas as pl
from jax.experimental.pallas import tpu as pltpu

F32 = jnp.float32
BF16 = jnp.bfloat16

D_MODEL = 1024
GRID_W = 64
HEAD_DIM = 64
NA_HEADS = 4
NA_WIN_ROWS = 8
NA_WIN_COLS = 16
DA_HEADS = 4
DA_CONFIGS = ((128, 1), (512, 4), (2048, 16))
DA_HALO = 64
N_BUCKETS = 32
MAX_DISTANCE = 1024
HG_HEADS = 4
HG_DK = 128
HG_CHUNK = 64
N_IN = 4096
D_FF = 2816
NORM_EPS = 1e-6
NEG_INF = -1e30
SEG = 256
FF_SEG = 256
VMEM_LIMIT = 56 * 1024 * 1024


def _cparams(*sem):
    return pltpu.CompilerParams(dimension_semantics=sem, vmem_limit_bytes=VMEM_LIMIT)


def _dot(a, b):
    return jnp.dot(a, b, preferred_element_type=F32)


def _dot_nt(a, b):
    return lax.dot_general(a, b, (((1,), (1,)), ((), ())), preferred_element_type=F32)


def _dot_tn(a, b):
    return lax.dot_general(a, b, (((0,), (0,)), ((), ())), preferred_element_type=F32)


DA_SEGS = (3, 4, 5)
DA_W = len(DA_SEGS) * SEG
DA_DILATED = tuple(d for _, d in DA_CONFIGS if d > 1)


def _inproj_kernel(x_ref, g_ref, w_ref, qg_ref, gm_ref, o_ref, *rest, tm):
    dil_refs, ybuf = rest[:-1], rest[-1]
    x = x_ref[...]
    ms = jnp.mean(x * x, axis=-1, keepdims=True)
    h = (x * lax.rsqrt(ms + NORM_EPS) * g_ref[...]).astype(BF16)
    normed = {0: 0, 1: 1, 3: 2, 4: 3}
    nseg = N_IN // SEG
    y_next = _dot(h, w_ref[:, 0:SEG])
    for seg in range(nseg):
        y = y_next
        if seg + 1 < nseg:
            y_next = _dot(h, w_ref[:, (seg + 1) * SEG:(seg + 2) * SEG])
        if seg in normed:
            ss = _dot((y * y).astype(BF16), gm_ref[...])
            r = normed[seg]
            y = y * lax.rsqrt(ss + NORM_EPS) * qg_ref[r:r + 1, :]
        if seg in DA_SEGS:
            k = DA_SEGS.index(seg)
            for j in range(SEG // 128):
                ybuf[k * (SEG // 128) + j] = y[:, j * 128:(j + 1) * 128]
        o_ref[:, seg * SEG:(seg + 1) * SEG] = y.astype(BF16)
    for d, ref in zip(DA_DILATED, dil_refs):
        for rho in range(d):
            for j in range(DA_W // 128):
                ref[:, rho * DA_W + j * 128:rho * DA_W + (j + 1) * 128] = \
                    ybuf[j, pl.ds(rho, tm // d, stride=d), :].astype(BF16)


def _in_proj(x, ln_g, w_in, qg, gm, tm):
    n = x.shape[0]
    row = lambda i: (i, 0)
    const = lambda i: (0, 0)
    return pl.pallas_call(
        functools.partial(_inproj_kernel, tm=tm),
        grid=(n // tm,),
        in_specs=[
            pl.BlockSpec((tm, D_MODEL), row),
            pl.BlockSpec((1, D_MODEL), const),
            pl.BlockSpec((D_MODEL, N_IN), const),
            pl.BlockSpec((4, SEG), const),
            pl.BlockSpec((SEG, SEG), const),
        ],
        out_specs=[pl.BlockSpec((tm, N_IN), row)] + [pl.BlockSpec((tm // d, d * DA_W), row) for d in DA_DILATED],
        out_shape=[jax.ShapeDtypeStruct((n, N_IN), BF16)]
        + [jax.ShapeDtypeStruct((n // d, d * DA_W), BF16) for d in DA_DILATED],
        scratch_shapes=[pltpu.VMEM((DA_W // 128, tm, 128), F32)],
        compiler_params=_cparams("parallel"),
        name="in_proj",
    )(x, ln_g, w_in, qg, gm)


NA_RB = 8
NA_TOK = NA_RB * GRID_W
NA_KEYS = NA_WIN_ROWS * GRID_W
NA_NDR = 2 * NA_WIN_ROWS - 1


def _na_bias_table(rpb):
    cq = np.arange(GRID_W)
    ck = np.arange(GRID_W)
    c0 = np.clip(cq - NA_WIN_COLS // 2, 0, GRID_W - NA_WIN_COLS)
    ok = (ck[None, :] >= c0[:, None]) & (ck[None, :] < c0[:, None] + NA_WIN_COLS)
    dc = np.clip(ck[None, :] - cq[:, None], -(NA_WIN_COLS - 1), NA_WIN_COLS - 1) + NA_WIN_COLS - 1
    onehot = (dc[:, :, None] == np.arange(2 * NA_WIN_COLS - 1)).astype(np.float32)
    t = jnp.einsum('hab,qkb->haqk', rpb.astype(F32), onehot, precision=lax.Precision.HIGHEST)
    t = jnp.where(ok[None, None], t, NEG_INF)
    return jnp.concatenate([t[:, :-1], t[:, 1:]], axis=-1)


def _na_kernel(q_ref, kp_ref, kc_ref, kn_ref, vp_ref, vc_ref, vn_ref, bias_ref, o_ref, kbuf, vbuf, *, rows):
    i = pl.program_id(1)
    kbuf[0:NA_TOK] = kp_ref[...]
    kbuf[NA_TOK:2 * NA_TOK] = kc_ref[...]
    kbuf[2 * NA_TOK:3 * NA_TOK] = kn_ref[...]
    vbuf[0:NA_TOK] = vp_ref[...]
    vbuf[NA_TOK:2 * NA_TOK] = vc_ref[...]
    vbuf[2 * NA_TOK:3 * NA_TOK] = vn_ref[...]
    low = lax.broadcasted_iota(jnp.int32, (GRID_W, 128), 1) < HEAD_DIM
    for j in range(NA_RB):
        r = i * NA_RB + j
        r0 = jnp.clip(r - NA_WIN_ROWS // 2, 0, rows - NA_WIN_ROWS)
        e = r - r0
        off = pl.multiple_of((r0 - i * NA_RB + NA_RB) * GRID_W, GRID_W)
        q = q_ref[j * GRID_W:(j + 1) * GRID_W, :]
        for a in range(2):
            qa = q[:, a * 128:(a + 1) * 128]
            ka = kbuf[pl.ds(off, NA_KEYS), a * 128:(a + 1) * 128]
            va = vbuf[pl.ds(off, NA_KEYS), a * 128:(a + 1) * 128]
            outs = []
            for half in range(2):
                qm = jnp.where(low if half == 0 else ~low, qa, jnp.zeros_like(qa))
                bias = jnp.concatenate(
                    [bias_ref[2 * a + half, 2 * w2 + NA_WIN_ROWS - 1 - e] for w2 in range(NA_WIN_ROWS // 2)], axis=-1)
                s = _dot_nt(qm, ka) + bias
                m = jnp.max(s, axis=-1, keepdims=True)
                p = jnp.exp(s - m)
                l = jnp.sum(p, axis=-1, keepdims=True)
                outs.append(_dot(p.astype(BF16), va) / l)
            o_ref[j * GRID_W:(j + 1) * GRID_W, a * 128:(a + 1) * 128] = jnp.where(low, outs[0], outs[1]).astype(BF16)


def _na_call(proj, bias, b, t, row_base):
    rows = t // GRID_W
    nrb = rows // NA_RB
    base = row_base // NA_TOK

    def spec(col, shift):
        def imap(bi, i):
            return (base + bi * nrb + jnp.clip(i + shift, 0, nrb - 1), col)
        return pl.BlockSpec((NA_TOK, SEG), imap)

    return pl.pallas_call(
        functools.partial(_na_kernel, rows=rows),
        grid=(b, nrb),
        in_specs=[spec(0, 0), spec(1, -1), spec(1, 0), spec(1, 1), spec(2, -1), spec(2, 0), spec(2, 1),
                  pl.BlockSpec((NA_HEADS, NA_NDR - 1, GRID_W, 2 * GRID_W), lambda bi, i: (0, 0, 0, 0))],
        out_specs=pl.BlockSpec((NA_TOK, SEG), lambda bi, i: (bi * nrb + i, 0)),
        out_shape=jax.ShapeDtypeStruct((b * t, SEG), BF16),
        scratch_shapes=[pltpu.VMEM((3 * NA_TOK, SEG), BF16), pltpu.VMEM((3 * NA_TOK, SEG), BF16)],
        compiler_params=_cparams("parallel", "parallel"),
        name="na",
    )(proj, proj, proj, proj, proj, proj, proj, bias)


def _t5_bucket(rel):
    nb = N_BUCKETS // 2
    max_exact = nb // 2
    ret = np.where(rel > 0, nb, 0)
    n = np.abs(rel)
    large = max_exact + (np.log(np.maximum(n, 1) / max_exact)
                         / np.log(MAX_DISTANCE / max_exact) * (nb - max_exact)).astype(np.int32)
    large = np.minimum(large, nb - 1)
    return (ret + np.where(n < max_exact, n, large)).astype(np.int32)


def _da_bias_table(t5_bias, dilation, qb):
    kb = qb + 2 * DA_HALO
    rel = np.arange(kb)[None, :] - np.arange(qb)[:, None] - DA_HALO
    ok = np.abs(rel) <= DA_HALO
    onehot = (_t5_bucket(rel * dilation)[:, :, None] == np.arange(N_BUCKETS)).astype(np.float32)
    bias = jnp.einsum('qkb,bh->hqk', onehot, t5_bias.astype(F32), precision=lax.Precision.HIGHEST)
    return jnp.where(ok[None], bias, NEG_INF)


def _da_kernel(*refs, qb, nb, d, length, prev_dils):
    final = bool(prev_dils)
    (main_ref, left_ref, right_ref, bias_ref), refs = refs[:4], refs[4:]
    if final:
        npv = 2 * len(prev_dils)
        prev_refs, o_ref, buf, und = refs[:npv], refs[npv], refs[npv + 1], refs[npv + 2:]
    else:
        o_ref, lse_ref, buf = refs
    c = pl.program_id(1)
    tr = qb * nb
    kb = qb + 2 * DA_HALO
    buf[0:DA_HALO] = left_ref[...]
    buf[DA_HALO:DA_HALO + tr] = main_ref[...]
    buf[DA_HALO + tr:tr + 2 * DA_HALO] = right_ref[...]
    if final:
        for k, pd in enumerate(prev_dils):
            for part in range(2):
                src, dst = prev_refs[2 * k + part], und[2 * k + part]
                for rho in range(pd):
                    for j in range(SEG // 128):
                        dst[j, pl.ds(rho, tr // pd, stride=pd), :] = \
                            src[:, rho * SEG + j * 128:rho * SEG + (j + 1) * 128]
    low = lax.broadcasted_iota(jnp.int32, (qb, 128), 1) < HEAD_DIM
    for blk in range(nb):
        rows = slice(blk * qb, (blk + 1) * qb)
        krows = slice(blk * qb, blk * qb + kb)
        pos = (c * nb + blk) * qb - DA_HALO + lax.broadcasted_iota(jnp.int32, (qb, kb), 1)
        pos_ok = (pos >= 0) & (pos < length)
        for rho in range(d):
            for a in range(2):
                base = rho * DA_W + a * 128
                qa = main_ref[rows, base:base + 128]
                ka = buf[krows, base + SEG:base + SEG + 128]
                va = buf[krows, base + 2 * SEG:base + 2 * SEG + 128]
                outs, lses = [], []
                for half in range(2):
                    qm = jnp.where(low if half == 0 else ~low, qa, jnp.zeros_like(qa))
                    s = jnp.where(pos_ok, _dot_nt(qm, ka) + bias_ref[2 * a + half], NEG_INF)
                    m = jnp.max(s, axis=-1, keepdims=True)
                    p = jnp.exp(s - m)
                    l = jnp.sum(p, axis=-1, keepdims=True)
                    outs.append(_dot(p.astype(BF16), va) / l)
                    lses.append(jnp.broadcast_to(m + jnp.log(l), (qb, 128)))
                o = jnp.where(low, outs[0], outs[1])
                lse = jnp.where(low, lses[0], lses[1])
                cols = slice(rho * SEG + a * 128, rho * SEG + (a + 1) * 128)
                if final:
                    mx = lse
                    for k in range(len(prev_dils)):
                        mx = jnp.maximum(mx, und[2 * k + 1][a, rows, :])
                    wsum = jnp.exp(lse - mx)
                    mix = wsum * o
                    for k in range(len(prev_dils)):
                        wk = jnp.exp(und[2 * k + 1][a, rows, :] - mx)
                        wsum = wsum + wk
                        mix = mix + wk * und[2 * k][a, rows, :]
                    o_ref[rows, cols] = (mix / wsum).astype(BF16)
                else:
                    o_ref[rows, cols] = o
                    lse_ref[rows, cols] = lse


DA_BLOCKS = {1: 8, 4: 4, 16: 1}


def _da_call(view, colblk, bias, b, t, row_base, dilation, prev=None, prev_dils=()):
    d = dilation
    length = t // d
    qb = min(128, length)
    nb = min(DA_BLOCKS[d], length // qb)
    tr = qb * nb
    nt = length // tr
    kb = qb + 2 * DA_HALO
    ubase = row_base // d
    hpt = tr // DA_HALO
    nhalo = view.shape[0] // DA_HALO
    final = prev is not None

    def halo(right):
        def imap(bi, c):
            blk = (ubase + bi * length) // DA_HALO + ((c + 1) * hpt if right else c * hpt - 1)
            return (jnp.clip(blk, 0, nhalo - 1), colblk)
        return pl.BlockSpec((DA_HALO, d * DA_W), imap)

    in_specs = [pl.BlockSpec((tr, d * DA_W), lambda bi, c: ((ubase + bi * length) // tr + c, colblk)),
                halo(False), halo(True),
                pl.BlockSpec((DA_HEADS, qb, kb), lambda bi, c: (0, 0, 0))]
    args = [view] * 3 + [bias]
    out_block = pl.BlockSpec((tr, d * SEG), lambda bi, c: (bi * nt + c, 0))
    scratch = [pltpu.VMEM((tr + 2 * DA_HALO, d * DA_W), BF16)]
    if final:
        assert d == 1
        for pd in prev_dils:
            in_specs += [pl.BlockSpec((tr // pd, pd * SEG), lambda bi, c: (bi * nt + c, 0))] * 2
            scratch += [pltpu.VMEM((SEG // 128, tr, 128), F32)] * 2
        args += list(prev)
        out_specs = out_block
        out_shape = jax.ShapeDtypeStruct((b * t, SEG), BF16)
    else:
        out_specs = [out_block, out_block]
        out_shape = [jax.ShapeDtypeStruct((b * t // d, d * SEG), F32)] * 2
    return pl.pallas_call(
        functools.partial(_da_kernel, qb=qb, nb=nb, d=d, length=length,
                          prev_dils=tuple(prev_dils) if final else ()),
        grid=(b, nt),
        in_specs=in_specs,
        out_specs=out_specs,
        out_shape=out_shape,
        scratch_shapes=scratch,
        compiler_params=_cparams("parallel", "parallel"),
        name="da_d%d" % d,
    )(*args)


HG_LEVELS = (32, 16, 8, 4, 2, 1)
HG_NSEC = 2 + len(HG_LEVELS)
HG_W = HG_HEADS * HG_DK
HG_PAIR = 2 * HG_DK
LOG2_E = math.log2(math.e)


def _hg_constants(rev):
    c = HG_CHUNK
    t = np.arange(c)[:, None]
    j = np.arange(c)[None, :]
    secs = []
    if not rev:
        secs.append(j <= t)
        secs.append(j > t)
    else:
        secs.append(j >= t)
        secs.append(j < t)
    masks = [np.eye(c, dtype=bool)]
    s = np.arange(c)[None, :]
    for m in HG_LEVELS:
        p0 = (t // (2 * m)) * (2 * m)
        upper = (t & m) != 0
        if not rev:
            mid = p0 + m - 1
            sec = np.where(upper, (j > mid) & (j <= t), (j > t) & (j <= mid))
            pair = ((t // (2 * m)) == (s // (2 * m))) & upper & ((s & m) == 0)
        else:
            mid = p0 + m
            sec = np.where(upper, (j >= mid) & (j < t), (j >= t) & (j < mid))
            pair = ((t // (2 * m)) == (s // (2 * m))) & (~upper) & ((s & m) != 0)
        secs.append(sec)
        masks.append(pair)
    w = np.concatenate(secs, axis=0).astype(np.float32)
    masks = np.tile(np.stack(masks).astype(np.float32), (1, 1, 2))
    return jnp.asarray(w, BF16), jnp.asarray(masks)


def _bdiag(h0, h1):
    zero = jnp.zeros_like(h0)
    return jnp.concatenate([jnp.concatenate([h0, zero], axis=1), jnp.concatenate([zero, h1], axis=1)], axis=0)


def _hg_pair_operand(a, p):
    return _bdiag(a[:, (2 * p) * HG_DK:(2 * p + 1) * HG_DK], a[:, (2 * p + 1) * HG_DK:(2 * p + 2) * HG_DK])


def _hg_chunk(rev, z, x, v, lb, w_ref, mask_ref, st_ref):
    c_sz = HG_CHUNK
    log_lb = jnp.log(lb)
    log_1m = jnp.log1p(-lb)
    log_sig = jnp.minimum(z, 0.0) - jnp.log(1.0 + jnp.exp(-jnp.abs(z)))
    cc = log_1m + log_sig
    logf = jnp.maximum(log_lb, cc) + jnp.log(1.0 + jnp.exp(-jnp.abs(log_lb - cc)))
    kk = (1.0 - lb) * jnp.exp(log_sig - z)
    q = x / (1.0 + jnp.exp(-x))
    log2f = logf * LOG2_E
    hi = log2f.astype(BF16)
    lo = (log2f - hi.astype(F32)).astype(BF16)
    dec = jnp.exp2(_dot(w_ref[...], hi) + _dot(w_ref[...], lo))
    sub = lax.broadcasted_iota(jnp.int32, (1, 8, HG_W), 1)
    ys = []
    for lvl, m in enumerate(HG_LEVELS):
        if m >= 8:
            sel = jnp.concatenate(
                [(q if (((8 * r) & m) != 0) != rev else kk)[8 * r:8 * r + 8] for r in range(c_sz // 8)], axis=0)
        else:
            is_q = ((sub & m) == 0) if rev else ((sub & m) != 0)
            sel = jnp.where(is_q, q.reshape(c_sz // 8, 8, HG_W), kk.reshape(c_sz // 8, 8, HG_W)).reshape(c_sz, HG_W)
        ys.append((sel * dec[(2 + lvl) * c_sz:(3 + lvl) * c_sz]).astype(BF16))
    q16, k16 = q.astype(BF16), kk.astype(BF16)
    qd = (q * dec[0:c_sz]).astype(BF16)
    kd = (kk * dec[c_sz:2 * c_sz]).astype(BF16)
    edge = 0 if rev else c_sz - 1
    outs = []
    for p in range(HG_HEADS // 2):
        cols = slice(p * HG_PAIR, (p + 1) * HG_PAIR)
        a2 = mask_ref[0] * _dot_nt(q16[:, cols], _hg_pair_operand(k16, p))
        for lvl in range(len(HG_LEVELS)):
            a2 = a2 + mask_ref[1 + lvl] * _dot_nt(ys[lvl][:, cols], _hg_pair_operand(ys[lvl], p))
        st = _bdiag(st_ref[2 * p].astype(BF16), st_ref[2 * p + 1].astype(BF16))
        outs.append(_dot(a2.astype(BF16), _hg_pair_operand(v, p)) + _dot_nt(qd[:, cols], st))
        for h in (2 * p, 2 * p + 1):
            hc = slice(h * HG_DK, (h + 1) * HG_DK)
            st_ref[h] = st_ref[h] * dec[edge:edge + 1, hc] + _dot_tn(v[:, hc], kd[:, hc])
    return jnp.concatenate(outs, axis=1)


def _hg_kernel(cqf_ref, cff_ref, cif_ref, cqb_ref, cfb_ref, cib_ref, lb_ref, wf_ref, wb_ref, mf_ref, mb_ref,
               of_ref, ob_ref, st_ref, *, nchunks):
    @pl.when(pl.program_id(1) == 0)
    def _():
        st_ref[...] = jnp.zeros_like(st_ref)

    def body(step, carry):
        rf = pl.ds(pl.multiple_of(step * HG_CHUNK, HG_CHUNK), HG_CHUNK)
        rb = pl.ds(pl.multiple_of((nchunks - 1 - step) * HG_CHUNK, HG_CHUNK), HG_CHUNK)
        of_ref[rf, :] = _hg_chunk(False, cff_ref[rf, :].astype(F32), cqf_ref[rf, :].astype(F32), cif_ref[rf, :],
                                  lb_ref[0:1, :], wf_ref, mf_ref, st_ref.at[0]).astype(BF16)
        ob_ref[rb, :] = _hg_chunk(True, cfb_ref[rb, :].astype(F32), cqb_ref[rb, :].astype(F32), cib_ref[rb, :],
                                  lb_ref[1:2, :], wb_ref, mb_ref, st_ref.at[1]).astype(BF16)
        return carry

    lax.fori_loop(0, nchunks, body, 0, unroll=True)


def _hg_call(proj, lb, consts_f, consts_b, b, t, row_base):
    tb = min(256, t)
    nt = t // tb
    base = row_base // tb

    def pspec(col, rev):
        return pl.BlockSpec((tb, HG_W), lambda bi, i: (base + bi * nt + (nt - 1 - i if rev else i), col))

    def ospec(rev):
        return pl.BlockSpec((tb, HG_W), lambda bi, i: (bi * nt + (nt - 1 - i if rev else i), 0))

    def const(a):
        return pl.BlockSpec(a.shape, lambda bi, i: (0,) * a.ndim)

    (wf, mf), (wb, mb) = consts_f, consts_b
    return pl.pallas_call(
        functools.partial(_hg_kernel, nchunks=tb // HG_CHUNK),
        grid=(b, nt),
        in_specs=[pspec(3, False), pspec(4, False), pspec(6, False), pspec(3, True), pspec(5, True), pspec(6, True),
                  const(lb), const(wf), const(wb), const(mf), const(mb)],
        out_specs=[ospec(False), ospec(True)],
        out_shape=[jax.ShapeDtypeStruct((b * t, HG_W), BF16)] * 2,
        scratch_shapes=[pltpu.VMEM((2, HG_HEADS, HG_DK, HG_DK), F32)],
        compiler_params=_cparams("parallel", "arbitrary"),
        name="hgrn",
    )(proj, proj, proj, proj, proj, proj, lb, wf, wb, mf, mb)


def _outproj_kernel(oa_ref, ob_ref, cf_ref, cb_ref, cg_ref, ng_ref, x_ref, w_ref, g_ref, x1_ref, h_ref):
    tot = cf_ref[...].astype(F32) + cb_ref[...].astype(F32)
    gate = cg_ref[...].astype(F32)
    parts = []
    for h in range(HG_HEADS):
        th = tot[:, h * HG_DK:(h + 1) * HG_DK]
        ms = jnp.mean(th * th, axis=-1, keepdims=True)
        parts.append(th * lax.rsqrt(ms + NORM_EPS) * ng_ref[...])
    oc = (jnp.concatenate(parts, axis=1) * (gate / (1.0 + jnp.exp(-gate)))).astype(BF16)
    y = _dot(oa_ref[...], w_ref[0:SEG, :]) + _dot(ob_ref[...], w_ref[SEG:2 * SEG, :]) + _dot(oc, w_ref[2 * SEG:, :])
    x1 = x_ref[...] + y
    x1_ref[...] = x1
    ms = jnp.mean(x1 * x1, axis=-1, keepdims=True)
    h_ref[...] = (x1 * lax.rsqrt(ms + NORM_EPS) * g_ref[...]).astype(BF16)


def _out_proj(oa, ob, ocf, ocb, proj, norm_g, x, w_out, ln_g, tm):
    n = x.shape[0]
    row = lambda i: (i, 0)
    const = lambda i: (0, 0)
    return pl.pallas_call(
        _outproj_kernel,
        grid=(n // tm,),
        in_specs=[pl.BlockSpec((tm, SEG), row), pl.BlockSpec((tm, SEG), row),
                  pl.BlockSpec((tm, HG_W), row), pl.BlockSpec((tm, HG_W), row),
                  pl.BlockSpec((tm, HG_W), lambda i: (i, N_IN // HG_W - 1)),
                  pl.BlockSpec((1, HG_DK), const),
                  pl.BlockSpec((tm, D_MODEL), row),
                  pl.BlockSpec((D_MODEL, D_MODEL), const),
                  pl.BlockSpec((1, D_MODEL), const)],
        out_specs=[pl.BlockSpec((tm, D_MODEL), row), pl.BlockSpec((tm, D_MODEL), row)],
        out_shape=[jax.ShapeDtypeStruct((n, D_MODEL), F32), jax.ShapeDtypeStruct((n, D_MODEL), BF16)],
        compiler_params=_cparams("parallel"),
        name="out_proj",
    )(oa, ob, ocf, ocb, proj, norm_g, x, w_out, ln_g)


FF_HALO = 16


def _ffn_kernel(x_ref, h_ref, hp_ref, hn_ref, wup_ref, cw_ref, cb_ref, wdn_ref, o_ref, hbuf, *, tm, starts, ends):
    row0 = pl.program_id(0) * tm
    at_start = functools.reduce(jnp.logical_or, [row0 == s for s in starts])
    at_end = functools.reduce(jnp.logical_or, [row0 + tm == e for e in ends])
    keep_prev = jnp.where(at_start, 0.0, 1.0)
    keep_next = jnp.where(at_end, 0.0, 1.0)
    hbuf[0:tm] = h_ref[...]
    hbuf[tm:tm + FF_HALO] = hp_ref[...]
    hbuf[tm + FF_HALO:tm + 2 * FF_HALO] = hn_ref[...]
    first = lax.broadcasted_iota(jnp.int32, (8, FF_SEG), 0) == 0
    last = lax.broadcasted_iota(jnp.int32, (8, FF_SEG), 0) == 7
    c_tanh = math.sqrt(2.0 / math.pi)
    o_ref[...] = x_ref[...]
    nseg = D_FF // FF_SEG

    def up_proj(j):
        return _dot(hbuf[...], wup_ref[:, 2 * j * FF_SEG:2 * (j + 1) * FF_SEG])

    y_next = up_proj(0)
    for j in range(nseg):
        cols = slice(j * FF_SEG, (j + 1) * FF_SEG)
        y = y_next
        if j + 1 < nseg:
            y_next = up_proj(j + 1)
        g, up = y[0:tm, 0:FF_SEG], y[0:tm, FF_SEG:]
        g_prev = y[tm + FF_HALO - 1:tm + FF_HALO, 0:FF_SEG] * keep_prev
        g_next = y[tm + FF_HALO:tm + FF_HALO + 1, 0:FF_SEG] * keep_next
        cw0, cw1, cw2 = cw_ref[0:1, cols], cw_ref[1:2, cols], cw_ref[2:3, cols]
        gate = pltpu.roll(g, 1, 0) * cw0 + g * cw1 + pltpu.roll(g, tm - 1, 0) * cw2 + cb_ref[:, cols]
        top = gate[0:8] + jnp.where(first, (g_prev - g[tm - 1:tm]) * cw0, 0.0)
        bot = gate[tm - 8:tm] + jnp.where(last, (g_next - g[0:1]) * cw2, 0.0)
        gate = jnp.concatenate([top, gate[8:tm - 8], bot], axis=0)
        inner = gate * (c_tanh + (c_tanh * 0.044715) * (gate * gate))
        act = (gate * up) * (1.0 + jnp.tanh(inner))
        o_ref[...] += _dot(act.astype(BF16), wdn_ref[cols, :])


def _ffn(x1, h2, w_up, conv_w, conv_b, w_down, tm, starts, ends):
    n = x1.shape[0]
    hpb = tm // FF_HALO
    nh = n // FF_HALO
    row = lambda i: (i, 0)
    const = lambda i: (0, 0)
    nseg = D_FF // FF_SEG
    w_up = w_up.reshape(D_MODEL, 2, nseg, FF_SEG).transpose(0, 2, 1, 3).reshape(D_MODEL, 2 * D_FF)
    w_down = w_down * 0.5
    return pl.pallas_call(
        functools.partial(_ffn_kernel, tm=tm, starts=starts, ends=ends),
        grid=(n // tm,),
        in_specs=[pl.BlockSpec((tm, D_MODEL), row), pl.BlockSpec((tm, D_MODEL), row),
                  pl.BlockSpec((FF_HALO, D_MODEL), lambda i: (jnp.maximum(i * hpb - 1, 0), 0)),
                  pl.BlockSpec((FF_HALO, D_MODEL), lambda i: (jnp.minimum((i + 1) * hpb, nh - 1), 0)),
                  pl.BlockSpec((D_MODEL, 2 * D_FF), const),
                  pl.BlockSpec((3, D_FF), const), pl.BlockSpec((1, D_FF), const),
                  pl.BlockSpec((D_FF, D_MODEL), const)],
        out_specs=pl.BlockSpec((tm, D_MODEL), row),
        out_shape=jax.ShapeDtypeStruct((n, D_MODEL), F32),
        scratch_shapes=[pltpu.VMEM((tm + 2 * FF_HALO, D_MODEL), BF16)],
        compiler_params=_cparams("parallel"),
        name="ffn",
    )(x1, h2, h2, h2, w_up, conv_w, conv_b, w_down)


def _lower_bounds(lb_logits):
    p = jax.nn.softmax(lb_logits.astype(F32), axis=1)
    c = jnp.cumsum(p, axis=1)
    return c - c[:, :1]


def _row_tile(groups):
    tm = 512
    while any(t % tm for _, t, _ in groups):
        tm //= 2
    return tm


def kernel(x_prompt, x_sample, ln_mix_g, w_in, na_q_g, na_k_g, na_rpb, da_q_g, da_k_g, t5_bias, hg_lb_logits,
           hg_norm_g, w_out, ln_ffn_g, w_up, conv_w, conv_b, w_down):
    depth = w_in.shape[0]
    groups = []
    base = 0
    for xg in (x_prompt, x_sample):
        b, t, _ = xg.shape
        groups.append((b, t, base))
        base += b * t
    starts = tuple(rb + i * t for b, t, rb in groups for i in range(b))
    ends = tuple(rb + (i + 1) * t for b, t, rb in groups for i in range(b))
    tm = _row_tile(groups)
    x = jnp.concatenate([x_prompt.reshape(-1, D_MODEL), x_sample.reshape(-1, D_MODEL)], axis=0)

    lb = _lower_bounds(hg_lb_logits)
    scale = HEAD_DIM ** -0.5
    tile4 = lambda g: jnp.tile(g.astype(F32), NA_HEADS)
    gm = jnp.asarray(np.kron(np.eye(SEG // HEAD_DIM), np.ones((HEAD_DIM, HEAD_DIM))) / HEAD_DIM, BF16)
    hg_fwd_consts = _hg_constants(False)
    hg_bwd_consts = _hg_constants(True)
    da_bias = {}

    def da_bias_for(d, qb):
        if (d, qb) not in da_bias:
            da_bias[(d, qb)] = _da_bias_table(t5_bias, d, qb)
        return da_bias[(d, qb)]

    for l in range(depth):
        qg = jnp.stack([tile4(na_q_g[l]) * scale, tile4(na_k_g[l]), tile4(da_q_g[l]) * scale, tile4(da_k_g[l])])
        proj, *dil_views = _in_proj(x, ln_mix_g[l].reshape(1, -1), w_in[l].astype(BF16), qg, gm, tm)
        na_bias = _na_bias_table(na_rpb[l])
        oa, ob, ocf, ocb = [], [], [], []
        for b, t, rb in groups:
            oa.append(_na_call(proj, na_bias, b, t, rb))
            prev = []
            for d, view in zip(DA_DILATED, dil_views):
                prev += _da_call(view, 0, da_bias_for(d, min(128, t // d)), b, t, rb, d)
            ob.append(_da_call(proj, DA_SEGS[0] * SEG // DA_W, da_bias_for(1, min(128, t)), b, t, rb, 1,
                               prev=prev, prev_dils=DA_DILATED))
            cf, cb = _hg_call(proj, lb[:, l], hg_fwd_consts, hg_bwd_consts, b, t, rb)
            ocf.append(cf)
            ocb.append(cb)
        x1, h2 = _out_proj(jnp.concatenate(oa), jnp.concatenate(ob), jnp.concatenate(ocf), jnp.concatenate(ocb),
                           proj, hg_norm_g[l].reshape(1, -1).astype(F32), x, w_out[l].astype(BF16),
                           ln_ffn_g[l].reshape(1, -1), tm)
        x = _ffn(x1, h2, w_up[l].astype(BF16), conv_w[l], conv_b[l].reshape(1, -1), w_down[l].astype(BF16),
                 tm, starts, ends)

    outs = []
    for (b, t, rb), xg in zip(groups, (x_prompt, x_sample)):
        outs.append(x[rb:rb + b * t].reshape(xg.shape))
    return tuple(outs)
```

```python
import functools
import math

import numpy as np
import jax
import jax.numpy as jnp
from jax import lax
from jax.experimental import pallas as pl
from jax.experimental.pallas import tpu as pltpu

F32 = jnp.float32
BF16 = jnp.bfloat16

D_MODEL = 1024
GRID_W = 64
HEAD_DIM = 64
NA_HEADS = 4
NA_WIN_ROWS = 8
NA_WIN_COLS = 16
DA_HEADS = 4
DA_CONFIGS = ((128, 1), (512, 4), (2048, 16))
DA_HALO = 64
N_BUCKETS = 32
MAX_DISTANCE = 1024
HG_HEADS = 4
HG_DK = 128
HG_CHUNK = 64
N_IN = 4096
D_FF = 2816
NORM_EPS = 1e-6
NEG_INF = -1e30
SEG = 256
FF_SEG = 256
VMEM_LIMIT = 56 * 1024 * 1024


def _cparams(*sem):
    return pltpu.CompilerParams(dimension_semantics=sem, vmem_limit_bytes=VMEM_LIMIT)


def _dot(a, b):
    return jnp.dot(a, b, preferred_element_type=F32)


def _dot_nt(a, b):
    return lax.dot_general(a, b, (((1,), (1,)), ((), ())), preferred_element_type=F32)


def _dot_tn(a, b):
    return lax.dot_general(a, b, (((0,), (0,)), ((), ())), preferred_element_type=F32)


DA_SEGS = (3, 4, 5)
DA_W = len(DA_SEGS) * SEG
DA_DILATED = tuple(d for _, d in DA_CONFIGS if d > 1)


def _inproj_kernel(x_ref, g_ref, w_ref, qg_ref, gm_ref, o_ref, *rest, tm):
    dil_refs, ybuf = rest[:-1], rest[-1]
    x = x_ref[...]
    ms = jnp.mean(x * x, axis=-1, keepdims=True)
    h = (x * lax.rsqrt(ms + NORM_EPS) * g_ref[...]).astype(BF16)
    normed = {0: 0, 1: 1, 3: 2, 4: 3}
    nseg = N_IN // SEG
    y_next = _dot(h, w_ref[:, 0:SEG])
    for seg in range(nseg):
        y = y_next
        if seg + 1 < nseg:
            y_next = _dot(h, w_ref[:, (seg + 1) * SEG:(seg + 2) * SEG])
        if seg in normed:
            ss = _dot((y * y).astype(BF16), gm_ref[...])
            r = normed[seg]
            y = y * lax.rsqrt(ss + NORM_EPS) * qg_ref[r:r + 1, :]
        if seg in DA_SEGS:
            k = DA_SEGS.index(seg)
            for j in range(SEG // 128):
                ybuf[k * (SEG // 128) + j] = y[:, j * 128:(j + 1) * 128]
        o_ref[:, seg * SEG:(seg + 1) * SEG] = y.astype(BF16)
    for d, ref in zip(DA_DILATED, dil_refs):
        for rho in range(d):
            for j in range(DA_W // 128):
                ref[:, rho * DA_W + j * 128:rho * DA_W + (j + 1) * 128] = \
                    ybuf[j, pl.ds(rho, tm // d, stride=d), :].astype(BF16)


def _in_proj(x, ln_g, w_in, qg, gm, tm):
    n = x.shape[0]
    row = lambda i: (i, 0)
    const = lambda i: (0, 0)
    return pl.pallas_call(
        functools.partial(_inproj_kernel, tm=tm),
        grid=(n // tm,),
        in_specs=[
            pl.BlockSpec((tm, D_MODEL), row),
            pl.BlockSpec((1, D_MODEL), const),
            pl.BlockSpec((D_MODEL, N_IN), const),
            pl.BlockSpec((4, SEG), const),
            pl.BlockSpec((SEG, SEG), const),
        ],
        out_specs=[pl.BlockSpec((tm, N_IN), row)] + [pl.BlockSpec((tm // d, d * DA_W), row) for d in DA_DILATED],
        out_shape=[jax.ShapeDtypeStruct((n, N_IN), BF16)]
        + [jax.ShapeDtypeStruct((n // d, d * DA_W), BF16) for d in DA_DILATED],
        scratch_shapes=[pltpu.VMEM((DA_W // 128, tm, 128), F32)],
        compiler_params=_cparams("parallel"),
        name="in_proj",
    )(x, ln_g, w_in, qg, gm)


NA_RB = 8
NA_TOK = NA_RB * GRID_W
NA_GR = 4
NA_GQ = NA_GR * GRID_W
NA_KR = NA_GR + NA_WIN_ROWS
NA_GK = NA_KR * GRID_W
NA_NDR = 2 * NA_WIN_ROWS - 1


def _na_bias_table(rpb):
    cq = np.arange(GRID_W)
    ck = np.arange(GRID_W)
    c0 = np.clip(cq - NA_WIN_COLS // 2, 0, GRID_W - NA_WIN_COLS)
    col_ok = (ck[None, :] >= c0[:, None]) & (ck[None, :] < c0[:, None] + NA_WIN_COLS)
    dc = np.clip(ck[None, :] - cq[:, None], -(NA_WIN_COLS - 1), NA_WIN_COLS - 1) + NA_WIN_COLS - 1
    col_onehot = (dc[:, :, None] == np.arange(2 * NA_WIN_COLS - 1)).astype(np.float32)
    jq = np.arange(NA_GR)
    half = NA_WIN_ROWS // 2
    dq = np.stack([jq, half + jq, NA_WIN_ROWS + jq])
    dw0 = np.stack([0 * jq, jq, half + 0 * jq])
    w = np.arange(NA_KR)
    row_ok = (w[None, None, :] >= dw0[:, :, None]) & (w[None, None, :] < dw0[:, :, None] + NA_WIN_ROWS)
    dr = w[None, None, :] - dq[:, :, None] + NA_WIN_ROWS - 1
    row_onehot = ((dr[..., None] == np.arange(NA_NDR)) & row_ok[..., None]).astype(np.float32)
    t = jnp.einsum('hab,qkb->haqk', rpb.astype(F32), col_onehot, precision=lax.Precision.HIGHEST)
    bias = jnp.einsum('cjwa,haqk->chjqwk', row_onehot, t, precision=lax.Precision.HIGHEST)
    ok = row_ok[:, None, :, None, :, None] & col_ok[None, None, None, :, None, :]
    return jnp.where(ok, bias, NEG_INF).reshape(3, NA_HEADS, NA_GQ, NA_GK)


def _na_kernel(q_ref, kp_ref, kc_ref, kn_ref, vp_ref, vc_ref, vn_ref, bias0_ref, bias1_ref, o_ref, kbuf, vbuf, *,
               rows):
    i = pl.program_id(1)
    kbuf[0:NA_TOK] = kp_ref[...]
    kbuf[NA_TOK:2 * NA_TOK] = kc_ref[...]
    kbuf[2 * NA_TOK:3 * NA_TOK] = kn_ref[...]
    vbuf[0:NA_TOK] = vp_ref[...]
    vbuf[NA_TOK:2 * NA_TOK] = vc_ref[...]
    vbuf[2 * NA_TOK:3 * NA_TOK] = vn_ref[...]
    low = lax.broadcasted_iota(jnp.int32, (NA_GQ, 128), 1) < HEAD_DIM
    for g, bias_ref in enumerate((bias0_ref, bias1_ref)):
        r = i * NA_RB + g * NA_GR
        kstart = jnp.clip(r - NA_WIN_ROWS // 2, 0, rows - NA_KR)
        off = pl.multiple_of((kstart - i * NA_RB + NA_RB) * GRID_W, GRID_W)
        qrows = slice(g * NA_GQ, (g + 1) * NA_GQ)
        for a in range(2):
            cols = slice(a * 128, (a + 1) * 128)
            qa = q_ref[qrows, cols]
            ka = kbuf[pl.ds(off, NA_GK), cols]
            va = vbuf[pl.ds(off, NA_GK), cols]
            q2 = jnp.concatenate([jnp.where(low, qa, jnp.zeros_like(qa)), jnp.where(low, jnp.zeros_like(qa), qa)],
                                 axis=0)
            s = _dot_nt(q2, ka) + jnp.concatenate([bias_ref[0, 2 * a], bias_ref[0, 2 * a + 1]], axis=0)
            m = jnp.max(s, axis=-1, keepdims=True)
            p = jnp.exp(s - m)
            l = jnp.sum(p, axis=-1, keepdims=True)
            pv = _dot(p.astype(BF16), va) / l
            o_ref[qrows, cols] = jnp.where(low, pv[0:NA_GQ], pv[NA_GQ:]).astype(BF16)


def _na_call(proj, bias, b, t, row_base):
    rows = t // GRID_W
    nrb = rows // NA_RB
    base = row_base // NA_TOK
    assert rows >= NA_KR and rows % NA_RB == 0

    def spec(col, shift):
        def imap(bi, i):
            return (base + bi * nrb + jnp.clip(i + shift, 0, nrb - 1), col)
        return pl.BlockSpec((NA_TOK, SEG), imap)

    bias_block = (1, NA_HEADS, NA_GQ, NA_GK)
    bias0 = pl.BlockSpec(bias_block, lambda bi, i: (jnp.where(i == 0, 0, 1), 0, 0, 0))
    bias1 = pl.BlockSpec(bias_block, lambda bi, i: (jnp.where(i == nrb - 1, 2, 1), 0, 0, 0))
    return pl.pallas_call(
        functools.partial(_na_kernel, rows=rows),
        grid=(b, nrb),
        in_specs=[spec(0, 0), spec(1, -1), spec(1, 0), spec(1, 1), spec(2, -1), spec(2, 0), spec(2, 1), bias0, bias1],
        out_specs=pl.BlockSpec((NA_TOK, SEG), lambda bi, i: (bi * nrb + i, 0)),
        out_shape=jax.ShapeDtypeStruct((b * t, SEG), BF16),
        scratch_shapes=[pltpu.VMEM((3 * NA_TOK, SEG), BF16), pltpu.VMEM((3 * NA_TOK, SEG), BF16)],
        compiler_params=_cparams("parallel", "parallel"),
        name="na",
    )(proj, proj, proj, proj, proj, proj, proj, bias, bias)


def _t5_bucket(rel):
    nb = N_BUCKETS // 2
    max_exact = nb // 2
    ret = np.where(rel > 0, nb, 0)
    n = np.abs(rel)
    large = max_exact + (np.log(np.maximum(n, 1) / max_exact)
                         / np.log(MAX_DISTANCE / max_exact) * (nb - max_exact)).astype(np.int32)
    large = np.minimum(large, nb - 1)
    return (ret + np.where(n < max_exact, n, large)).astype(np.int32)


def _da_bias_table(t5_bias, dilation, qb):
    kb = qb + 2 * DA_HALO
    rel = np.arange(kb)[None, :] - np.arange(qb)[:, None] - DA_HALO
    ok = np.abs(rel) <= DA_HALO
    onehot = (_t5_bucket(rel * dilation)[:, :, None] == np.arange(N_BUCKETS)).astype(np.float32)
    bias = jnp.einsum('qkb,bh->hqk', onehot, t5_bias.astype(F32), precision=lax.Precision.HIGHEST)
    return jnp.where(ok[None], bias, NEG_INF)


def _da_kernel(*refs, qb, nb, d, length, prev_dils):
    final = bool(prev_dils)
    (main_ref, left_ref, right_ref, bias_ref), refs = refs[:4], refs[4:]
    if final:
        npv = 2 * len(prev_dils)
        prev_refs, o_ref, buf, und = refs[:npv], refs[npv], refs[npv + 1], refs[npv + 2:]
    else:
        o_ref, lse_ref, buf = refs
    c = pl.program_id(1)
    tr = qb * nb
    kb = qb + 2 * DA_HALO
    buf[0:DA_HALO] = left_ref[...]
    buf[DA_HALO:DA_HALO + tr] = main_ref[...]
    buf[DA_HALO + tr:tr + 2 * DA_HALO] = right_ref[...]
    if final:
        for k, pd in enumerate(prev_dils):
            for part in range(2):
                src, dst = prev_refs[2 * k + part], und[2 * k + part]
                for rho in range(pd):
                    for j in range(SEG // 128):
                        dst[j, pl.ds(rho, tr // pd, stride=pd), :] = \
                            src[:, rho * SEG + j * 128:rho * SEG + (j + 1) * 128]
    low = lax.broadcasted_iota(jnp.int32, (qb, 128), 1) < HEAD_DIM
    for blk in range(nb):
        rows = slice(blk * qb, (blk + 1) * qb)
        krows = slice(blk * qb, blk * qb + kb)
        pos = (c * nb + blk) * qb - DA_HALO + lax.broadcasted_iota(jnp.int32, (qb, kb), 1)
        pos_ok = (pos >= 0) & (pos < length)
        for rho in range(d):
            for a in range(2):
                base = rho * DA_W + a * 128
                qa = main_ref[rows, base:base + 128]
                ka = buf[krows, base + SEG:base + SEG + 128]
                va = buf[krows, base + 2 * SEG:base + 2 * SEG + 128]
                outs, lses = [], []
                for half in range(2):
                    qm = jnp.where(low if half == 0 else ~low, qa, jnp.zeros_like(qa))
                    s = jnp.where(pos_ok, _dot_nt(qm, ka) + bias_ref[2 * a + half], NEG_INF)
                    m = jnp.max(s, axis=-1, keepdims=True)
                    p = jnp.exp(s - m)
                    l = jnp.sum(p, axis=-1, keepdims=True)
                    outs.append(_dot(p.astype(BF16), va) / l)
                    lses.append(jnp.broadcast_to(m + jnp.log(l), (qb, 128)))
                o = jnp.where(low, outs[0], outs[1])
                lse = jnp.where(low, lses[0], lses[1])
                cols = slice(rho * SEG + a * 128, rho * SEG + (a + 1) * 128)
                if final:
                    mx = lse
                    for k in range(len(prev_dils)):
                        mx = jnp.maximum(mx, und[2 * k + 1][a, rows, :])
                    wsum = jnp.exp(lse - mx)
                    mix = wsum * o
                    for k in range(len(prev_dils)):
                        wk = jnp.exp(und[2 * k + 1][a, rows, :] - mx)
                        wsum = wsum + wk
                        mix = mix + wk * und[2 * k][a, rows, :]
                    o_ref[rows, cols] = (mix / wsum).astype(BF16)
                else:
                    o_ref[rows, cols] = o
                    lse_ref[rows, cols] = lse


DA_BLOCKS = {1: 8, 4: 4, 16: 1}


def _da_call(view, colblk, bias, b, t, row_base, dilation, prev=None, prev_dils=()):
    d = dilation
    length = t // d
    qb = min(128, length)
    nb = min(DA_BLOCKS[d], length // qb)
    tr = qb * nb
    nt = length // tr
    kb = qb + 2 * DA_HALO
    ubase = row_base // d
    hpt = tr // DA_HALO
    nhalo = view.shape[0] // DA_HALO
    final = prev is not None

    def halo(right):
        def imap(bi, c):
            blk = (ubase + bi * length) // DA_HALO + ((c + 1) * hpt if right else c * hpt - 1)
            return (jnp.clip(blk, 0, nhalo - 1), colblk)
        return pl.BlockSpec((DA_HALO, d * DA_W), imap)

    in_specs = [pl.BlockSpec((tr, d * DA_W), lambda bi, c: ((ubase + bi * length) // tr + c, colblk)),
                halo(False), halo(True),
                pl.BlockSpec((DA_HEADS, qb, kb), lambda bi, c: (0, 0, 0))]
    args = [view] * 3 + [bias]
    out_block = pl.BlockSpec((tr, d * SEG), lambda bi, c: (bi * nt + c, 0))
    scratch = [pltpu.VMEM((tr + 2 * DA_HALO, d * DA_W), BF16)]
    if final:
        assert d == 1
        for pd in prev_dils:
            in_specs += [pl.BlockSpec((tr // pd, pd * SEG), lambda bi, c: (bi * nt + c, 0))] * 2
            scratch += [pltpu.VMEM((SEG // 128, tr, 128), F32)] * 2
        args += list(prev)
        out_specs = out_block
        out_shape = jax.ShapeDtypeStruct((b * t, SEG), BF16)
    else:
        out_specs = [out_block, out_block]
        out_shape = [jax.ShapeDtypeStruct((b * t // d, d * SEG), F32)] * 2
    return pl.pallas_call(
        functools.partial(_da_kernel, qb=qb, nb=nb, d=d, length=length,
                          prev_dils=tuple(prev_dils) if final else ()),
        grid=(b, nt),
        in_specs=in_specs,
        out_specs=out_specs,
        out_shape=out_shape,
        scratch_shapes=scratch,
        compiler_params=_cparams("parallel", "parallel"),
        name="da_d%d" % d,
    )(*args)


HG_LEVELS = (32, 16, 8, 4, 2, 1)
HG_NSEC = 2 + len(HG_LEVELS)
HG_W = HG_HEADS * HG_DK
HG_PAIR = 2 * HG_DK
LOG2_E = math.log2(math.e)


def _hg_constants(rev):
    c = HG_CHUNK
    t = np.arange(c)[:, None]
    j = np.arange(c)[None, :]
    secs = []
    if not rev:
        secs.append(j <= t)
        secs.append(j > t)
    else:
        secs.append(j >= t)
        secs.append(j < t)
    masks = [np.eye(c, dtype=bool)]
    s = np.arange(c)[None, :]
    for m in HG_LEVELS:
        p0 = (t // (2 * m)) * (2 * m)
        upper = (t & m) != 0
        if not rev:
            mid = p0 + m - 1
            sec = np.where(upper, (j > mid) & (j <= t), (j > t) & (j <= mid))
            pair = ((t // (2 * m)) == (s // (2 * m))) & upper & ((s & m) == 0)
        else:
            mid = p0 + m
            sec = np.where(upper, (j >= mid) & (j < t), (j >= t) & (j < mid))
            pair = ((t // (2 * m)) == (s // (2 * m))) & (~upper) & ((s & m) != 0)
        secs.append(sec)
        masks.append(pair)
    w = np.tile(np.concatenate(secs, axis=0).astype(np.float32), (1, 2))
    masks = np.tile(np.stack(masks).astype(np.float32), (1, 1, 2))
    return jnp.asarray(w, BF16), jnp.asarray(masks)


def _bdiag(h0, h1):
    zero = jnp.zeros_like(h0)
    return jnp.concatenate([jnp.concatenate([h0, zero], axis=1), jnp.concatenate([zero, h1], axis=1)], axis=0)


def _hg_pair_operand(a, p):
    return _bdiag(a[:, (2 * p) * HG_DK:(2 * p + 1) * HG_DK], a[:, (2 * p + 1) * HG_DK:(2 * p + 2) * HG_DK])


def _hg_prepare(rev, z, x, v, lb, w_ref, mask_ref):
    c_sz = HG_CHUNK
    log_lb = jnp.log(lb)
    log_1m = jnp.log1p(-lb)
    log_sig = jnp.minimum(z, 0.0) - jnp.log(1.0 + jnp.exp(-jnp.abs(z)))
    cc = log_1m + log_sig
    logf = jnp.maximum(log_lb, cc) + jnp.log(1.0 + jnp.exp(-jnp.abs(log_lb - cc)))
    kk = (1.0 - lb) * jnp.exp(log_sig - z)
    q = x / (1.0 + jnp.exp(-x))
    log2f = logf * LOG2_E
    hi = log2f.astype(BF16)
    lo = (log2f - hi.astype(F32)).astype(BF16)
    dec = jnp.exp2(_dot(w_ref[...], jnp.concatenate([hi, lo], axis=0)))
    sub = lax.broadcasted_iota(jnp.int32, (1, 8, HG_W), 1)
    ys = []
    for lvl, m in enumerate(HG_LEVELS):
        if m >= 8:
            sel = jnp.concatenate(
                [(q if (((8 * r) & m) != 0) != rev else kk)[8 * r:8 * r + 8] for r in range(c_sz // 8)], axis=0)
        else:
            is_q = ((sub & m) == 0) if rev else ((sub & m) != 0)
            sel = jnp.where(is_q, q.reshape(c_sz // 8, 8, HG_W), kk.reshape(c_sz // 8, 8, HG_W)).reshape(c_sz, HG_W)
        ys.append((sel * dec[(2 + lvl) * c_sz:(3 + lvl) * c_sz]).astype(BF16))
    q16, k16 = q.astype(BF16), kk.astype(BF16)
    pairs = []
    for p in range(HG_HEADS // 2):
        cols = slice(p * HG_PAIR, (p + 1) * HG_PAIR)
        a2 = mask_ref[0] * _dot_nt(q16[:, cols], _hg_pair_operand(k16, p))
        for lvl in range(len(HG_LEVELS)):
            a2 = a2 + mask_ref[1 + lvl] * _dot_nt(ys[lvl][:, cols], _hg_pair_operand(ys[lvl], p))
        pairs.append(a2.astype(BF16))
    edge = 0 if rev else c_sz - 1
    return dict(pairs=pairs, v=v, qd=(q * dec[0:c_sz]).astype(BF16), kd=(kk * dec[c_sz:2 * c_sz]).astype(BF16),
                decay=dec[edge:edge + 1])


def _hg_finish(prep, st_ref):
    v, qd, kd = prep["v"], prep["qd"], prep["kd"]
    outs = []
    for p in range(HG_HEADS // 2):
        cols = slice(p * HG_PAIR, (p + 1) * HG_PAIR)
        st = _bdiag(st_ref[2 * p].astype(BF16), st_ref[2 * p + 1].astype(BF16))
        outs.append(_dot(prep["pairs"][p], _hg_pair_operand(v, p)) + _dot_nt(qd[:, cols], st))
        for h in (2 * p, 2 * p + 1):
            hc = slice(h * HG_DK, (h + 1) * HG_DK)
            st_ref[h] = st_ref[h] * prep["decay"][:, hc] + _dot_tn(v[:, hc], kd[:, hc])
    return jnp.concatenate(outs, axis=1)


def _hg_kernel(cqf_ref, cff_ref, cif_ref, cqb_ref, cfb_ref, cib_ref, lb_ref, wf_ref, wb_ref, mf_ref, mb_ref,
               of_ref, ob_ref, st_ref, *, nchunks):
    @pl.when(pl.program_id(1) == 0)
    def _():
        st_ref[...] = jnp.zeros_like(st_ref)

    def prepare(step):
        rf = slice(step * HG_CHUNK, (step + 1) * HG_CHUNK)
        rb = slice((nchunks - 1 - step) * HG_CHUNK, (nchunks - step) * HG_CHUNK)
        fwd = _hg_prepare(False, cff_ref[rf, :].astype(F32), cqf_ref[rf, :].astype(F32), cif_ref[rf, :],
                          lb_ref[0:1, :], wf_ref, mf_ref)
        bwd = _hg_prepare(True, cfb_ref[rb, :].astype(F32), cqb_ref[rb, :].astype(F32), cib_ref[rb, :],
                          lb_ref[1:2, :], wb_ref, mb_ref)
        return rf, fwd, rb, bwd

    def finish(rf, fwd, rb, bwd):
        of_ref[rf, :] = _hg_finish(fwd, st_ref.at[0]).astype(BF16)
        ob_ref[rb, :] = _hg_finish(bwd, st_ref.at[1]).astype(BF16)

    pending = [prepare(step) for step in range(nchunks)]
    for prep in pending:
        finish(*prep)


def _hg_call(proj, lb, consts_f, consts_b, b, t, row_base):
    tb = min(256, t)
    nt = t // tb
    base = row_base // tb

    def pspec(col, rev):
        return pl.BlockSpec((tb, HG_W), lambda bi, i: (base + bi * nt + (nt - 1 - i if rev else i), col))

    def ospec(rev):
        return pl.BlockSpec((tb, HG_W), lambda bi, i: (bi * nt + (nt - 1 - i if rev else i), 0))

    def const(a):
        return pl.BlockSpec(a.shape, lambda bi, i: (0,) * a.ndim)

    (wf, mf), (wb, mb) = consts_f, consts_b
    return pl.pallas_call(
        functools.partial(_hg_kernel, nchunks=tb // HG_CHUNK),
        grid=(b, nt),
        in_specs=[pspec(3, False), pspec(4, False), pspec(6, False), pspec(3, True), pspec(5, True), pspec(6, True),
                  const(lb), const(wf), const(wb), const(mf), const(mb)],
        out_specs=[ospec(False), ospec(True)],
        out_shape=[jax.ShapeDtypeStruct((b * t, HG_W), BF16)] * 2,
        scratch_shapes=[pltpu.VMEM((2, HG_HEADS, HG_DK, HG_DK), F32)],
        compiler_params=_cparams("parallel", "arbitrary"),
        name="hgrn",
    )(proj, proj, proj, proj, proj, proj, lb, wf, wb, mf, mb)


def _outproj_kernel(oa_ref, ob_ref, cf_ref, cb_ref, cg_ref, ng_ref, x_ref, w_ref, g_ref, x1_ref, h_ref):
    tot = cf_ref[...].astype(F32) + cb_ref[...].astype(F32)
    gate = cg_ref[...].astype(F32)
    parts = []
    for h in range(HG_HEADS):
        th = tot[:, h * HG_DK:(h + 1) * HG_DK]
        ms = jnp.mean(th * th, axis=-1, keepdims=True)
        parts.append(th * lax.rsqrt(ms + NORM_EPS) * ng_ref[...])
    oc = (jnp.concatenate(parts, axis=1) * (gate / (1.0 + jnp.exp(-gate)))).astype(BF16)
    y = _dot(oa_ref[...], w_ref[0:SEG, :]) + _dot(ob_ref[...], w_ref[SEG:2 * SEG, :]) + _dot(oc, w_ref[2 * SEG:, :])
    x1 = x_ref[...] + y
    x1_ref[...] = x1
    ms = jnp.mean(x1 * x1, axis=-1, keepdims=True)
    h_ref[...] = (x1 * lax.rsqrt(ms + NORM_EPS) * g_ref[...]).astype(BF16)


def _out_proj(oa, ob, ocf, ocb, proj, norm_g, x, w_out, ln_g, tm):
    n = x.shape[0]
    row = lambda i: (i, 0)
    const = lambda i: (0, 0)
    return pl.pallas_call(
        _outproj_kernel,
        grid=(n // tm,),
        in_specs=[pl.BlockSpec((tm, SEG), row), pl.BlockSpec((tm, SEG), row),
                  pl.BlockSpec((tm, HG_W), row), pl.BlockSpec((tm, HG_W), row),
                  pl.BlockSpec((tm, HG_W), lambda i: (i, N_IN // HG_W - 1)),
                  pl.BlockSpec((1, HG_DK), const),
                  pl.BlockSpec((tm, D_MODEL), row),
                  pl.BlockSpec((D_MODEL, D_MODEL), const),
                  pl.BlockSpec((1, D_MODEL), const)],
        out_specs=[pl.BlockSpec((tm, D_MODEL), row), pl.BlockSpec((tm, D_MODEL), row)],
        out_shape=[jax.ShapeDtypeStruct((n, D_MODEL), F32), jax.ShapeDtypeStruct((n, D_MODEL), BF16)],
        compiler_params=_cparams("parallel"),
        name="out_proj",
    )(oa, ob, ocf, ocb, proj, norm_g, x, w_out, ln_g)


FF_HALO = 16


def _ffn_kernel(x_ref, h_ref, hp_ref, hn_ref, wup_ref, cw_ref, cb_ref, wdn_ref, o_ref, hbuf, *, tm, starts, ends):
    row0 = pl.program_id(0) * tm
    at_start = functools.reduce(jnp.logical_or, [row0 == s for s in starts])
    at_end = functools.reduce(jnp.logical_or, [row0 + tm == e for e in ends])
    keep_prev = jnp.where(at_start, 0.0, 1.0)
    keep_next = jnp.where(at_end, 0.0, 1.0)
    hbuf[0:tm] = h_ref[...]
    hbuf[tm:tm + FF_HALO] = hp_ref[...]
    hbuf[tm + FF_HALO:tm + 2 * FF_HALO] = hn_ref[...]
    first = lax.broadcasted_iota(jnp.int32, (8, FF_SEG), 0) == 0
    last = lax.broadcasted_iota(jnp.int32, (8, FF_SEG), 0) == 7
    c_tanh = math.sqrt(2.0 / math.pi)
    o_ref[...] = x_ref[...]
    nseg = D_FF // FF_SEG

    def up_proj(j):
        return _dot(hbuf[...], wup_ref[:, 2 * j * FF_SEG:2 * (j + 1) * FF_SEG])

    y_next = up_proj(0)
    for j in range(nseg):
        cols = slice(j * FF_SEG, (j + 1) * FF_SEG)
        y = y_next
        if j + 1 < nseg:
            y_next = up_proj(j + 1)
        g, up = y[0:tm, 0:FF_SEG], y[0:tm, FF_SEG:]
        g_prev = y[tm + FF_HALO - 1:tm + FF_HALO, 0:FF_SEG] * keep_prev
        g_next = y[tm + FF_HALO:tm + FF_HALO + 1, 0:FF_SEG] * keep_next
        cw0, cw1, cw2 = cw_ref[0:1, cols], cw_ref[1:2, cols], cw_ref[2:3, cols]
        gate = pltpu.roll(g, 1, 0) * cw0 + g * cw1 + pltpu.roll(g, tm - 1, 0) * cw2 + cb_ref[:, cols]
        top = gate[0:8] + jnp.where(first, (g_prev - g[tm - 1:tm]) * cw0, 0.0)
        bot = gate[tm - 8:tm] + jnp.where(last, (g_next - g[0:1]) * cw2, 0.0)
        gate = jnp.concatenate([top, gate[8:tm - 8], bot], axis=0)
        inner = gate * (c_tanh + (c_tanh * 0.044715) * (gate * gate))
        act = (gate * up) * (1.0 + jnp.tanh(inner))
        o_ref[...] += _dot(act.astype(BF16), wdn_ref[cols, :])


def _ffn(x1, h2, w_up, conv_w, conv_b, w_down, tm, starts, ends):
    n = x1.shape[0]
    hpb = tm // FF_HALO
    nh = n // FF_HALO
    row = lambda i: (i, 0)
    const = lambda i: (0, 0)
    nseg = D_FF // FF_SEG
    w_up = w_up.reshape(D_MODEL, 2, nseg, FF_SEG).transpose(0, 2, 1, 3).reshape(D_MODEL, 2 * D_FF)
    w_down = w_down * 0.5
    return pl.pallas_call(
        functools.partial(_ffn_kernel, tm=tm, starts=starts, ends=ends),
        grid=(n // tm,),
        in_specs=[pl.BlockSpec((tm, D_MODEL), row), pl.BlockSpec((tm, D_MODEL), row),
                  pl.BlockSpec((FF_HALO, D_MODEL), lambda i: (jnp.maximum(i * hpb - 1, 0), 0)),
                  pl.BlockSpec((FF_HALO, D_MODEL), lambda i: (jnp.minimum((i + 1) * hpb, nh - 1), 0)),
                  pl.BlockSpec((D_MODEL, 2 * D_FF), const),
                  pl.BlockSpec((3, D_FF), const), pl.BlockSpec((1, D_FF), const),
                  pl.BlockSpec((D_FF, D_MODEL), const)],
        out_specs=pl.BlockSpec((tm, D_MODEL), row),
        out_shape=jax.ShapeDtypeStruct((n, D_MODEL), F32),
        scratch_shapes=[pltpu.VMEM((tm + 2 * FF_HALO, D_MODEL), BF16)],
        compiler_params=_cparams("parallel"),
        name="ffn",
    )(x1, h2, h2, h2, w_up, conv_w, conv_b, w_down)


def _lower_bounds(lb_logits):
    p = jax.nn.softmax(lb_logits.astype(F32), axis=1)
    c = jnp.cumsum(p, axis=1)
    return c - c[:, :1]


def _row_tile(groups):
    tm = 512
    while any(t % tm for _, t, _ in groups):
        tm //= 2
    return tm


def kernel(x_prompt, x_sample, ln_mix_g, w_in, na_q_g, na_k_g, na_rpb, da_q_g, da_k_g, t5_bias, hg_lb_logits,
           hg_norm_g, w_out, ln_ffn_g, w_up, conv_w, conv_b, w_down):
    depth = w_in.shape[0]
    groups = []
    base = 0
    for xg in (x_prompt, x_sample):
        b, t, _ = xg.shape
        groups.append((b, t, base))
        base += b * t
    starts = tuple(rb + i * t for b, t, rb in groups for i in range(b))
    ends = tuple(rb + (i + 1) * t for b, t, rb in groups for i in range(b))
    tm = _row_tile(groups)
    x = jnp.concatenate([x_prompt.reshape(-1, D_MODEL), x_sample.reshape(-1, D_MODEL)], axis=0)

    lb = _lower_bounds(hg_lb_logits)
    scale = HEAD_DIM ** -0.5
    tile4 = lambda g: jnp.tile(g.astype(F32), NA_HEADS)
    gm = jnp.asarray(np.kron(np.eye(SEG // HEAD_DIM), np.ones((HEAD_DIM, HEAD_DIM))) / HEAD_DIM, BF16)
    hg_fwd_consts = _hg_constants(False)
    hg_bwd_consts = _hg_constants(True)
    da_bias = {}

    def da_bias_for(d, qb):
        if (d, qb) not in da_bias:
            da_bias[(d, qb)] = _da_bias_table(t5_bias, d, qb)
        return da_bias[(d, qb)]

    for l in range(depth):
        qg = jnp.stack([tile4(na_q_g[l]) * scale, tile4(na_k_g[l]), tile4(da_q_g[l]) * scale, tile4(da_k_g[l])])
        proj, *dil_views = _in_proj(x, ln_mix_g[l].reshape(1, -1), w_in[l].astype(BF16), qg, gm, tm)
        na_bias = _na_bias_table(na_rpb[l])
        oa, ob, ocf, ocb = [], [], [], []
        for b, t, rb in groups:
            oa.append(_na_call(proj, na_bias, b, t, rb))
            prev = []
            for d, view in zip(DA_DILATED, dil_views):
                prev += _da_call(view, 0, da_bias_for(d, min(128, t // d)), b, t, rb, d)
            ob.append(_da_call(proj, DA_SEGS[0] * SEG // DA_W, da_bias_for(1, min(128, t)), b, t, rb, 1,
                               prev=prev, prev_dils=DA_DILATED))
            cf, cb = _hg_call(proj, lb[:, l], hg_fwd_consts, hg_bwd_consts, b, t, rb)
            ocf.append(cf)
            ocb.append(cb)
        x1, h2 = _out_proj(jnp.concatenate(oa), jnp.concatenate(ob), jnp.concatenate(ocf), jnp.concatenate(ocb),
                           proj, hg_norm_g[l].reshape(1, -1).astype(F32), x, w_out[l].astype(BF16),
                           ln_ffn_g[l].reshape(1, -1), tm)
        x = _ffn(x1, h2, w_up[l].astype(BF16), conv_w[l], conv_b[l].reshape(1, -1), w_down[l].astype(BF16),
                 tm, starts, ends)

    outs = []
    for (b, t, rb), xg in zip(groups, (x_prompt, x_sample)):
        outs.append(x[rb:rb + b * t].reshape(xg.shape))
    return tuple(outs)
```

```python
import functools
import math

import numpy as np
import jax
import jax.numpy as jnp
from jax import lax
from jax.experimental import pallas as pl
from jax.experimental.pallas import tpu as pltpu

F32 = jnp.float32
BF16 = jnp.bfloat16

D_MODEL = 1024
GRID_W = 64
HEAD_DIM = 64
NA_HEADS = 4
NA_WIN_ROWS = 8
NA_WIN_COLS = 16
DA_HEADS = 4
DA_CONFIGS = ((128, 1), (512, 4), (2048, 16))
DA_HALO = 64
N_BUCKETS = 32
MAX_DISTANCE = 1024
HG_HEADS = 4
HG_DK = 128
HG_CHUNK = 64
N_IN = 4096
D_FF = 2816
NORM_EPS = 1e-6
NEG_INF = -1e30
SEG = 256
FF_SEG = 256
VMEM_LIMIT = 56 * 1024 * 1024


def _cparams(*sem):
    return pltpu.CompilerParams(dimension_semantics=sem, vmem_limit_bytes=VMEM_LIMIT)


def _into(kernel, n_in, outs):
    def body(*refs):
        return kernel(*refs[:n_in], *refs[n_in + len(outs):])
    specs = [pl.BlockSpec(memory_space=pl.ANY)] * len(outs)
    shapes = [jax.ShapeDtypeStruct(o.shape, o.dtype) for o in outs]
    return body, specs, shapes, {n_in + k: k for k in range(len(outs))}


def _dot(a, b):
    return jnp.dot(a, b, preferred_element_type=F32)


def _dot_nt(a, b):
    return lax.dot_general(a, b, (((1,), (1,)), ((), ())), preferred_element_type=F32)


def _dot_tn(a, b):
    return lax.dot_general(a, b, (((0,), (0,)), ((), ())), preferred_element_type=F32)


DA_SEGS = (3, 4, 5)
DA_W = len(DA_SEGS) * SEG
DA_DILATED = tuple(d for _, d in DA_CONFIGS if d > 1)


def _inproj_kernel(x_ref, g_ref, w_ref, qg_ref, gm_ref, o_ref, *rest, tm):
    dil_refs, ybuf = rest[:-1], rest[-1]
    x = x_ref[...]
    ms = jnp.mean(x * x, axis=-1, keepdims=True)
    h = (x * lax.rsqrt(ms + NORM_EPS) * g_ref[...]).astype(BF16)
    normed = {0: 0, 1: 1, 3: 2, 4: 3}
    nseg = N_IN // SEG
    y_next = _dot(h, w_ref[:, 0:SEG])
    for seg in range(nseg):
        y = y_next
        if seg + 1 < nseg:
            y_next = _dot(h, w_ref[:, (seg + 1) * SEG:(seg + 2) * SEG])
        if seg in normed:
            ss = _dot((y * y).astype(BF16), gm_ref[...])
            r = normed[seg]
            y = y * lax.rsqrt(ss + NORM_EPS) * qg_ref[r:r + 1, :]
        if seg in DA_SEGS:
            k = DA_SEGS.index(seg)
            for j in range(SEG // 128):
                ybuf[k * (SEG // 128) + j] = y[:, j * 128:(j + 1) * 128]
        o_ref[:, seg * SEG:(seg + 1) * SEG] = y.astype(BF16)
    for d, ref in zip(DA_DILATED, dil_refs):
        for rho in range(d):
            for j in range(DA_W // 128):
                ref[:, rho * DA_W + j * 128:rho * DA_W + (j + 1) * 128] = \
                    ybuf[j, pl.ds(rho, tm // d, stride=d), :].astype(BF16)


def _in_proj(x, ln_g, w_in, qg, gm, tm):
    n = x.shape[0]
    row = lambda i: (i, 0)
    const = lambda i: (0, 0)
    return pl.pallas_call(
        functools.partial(_inproj_kernel, tm=tm),
        grid=(n // tm,),
        in_specs=[
            pl.BlockSpec((tm, D_MODEL), row),
            pl.BlockSpec((1, D_MODEL), const),
            pl.BlockSpec((D_MODEL, N_IN), const),
            pl.BlockSpec((4, SEG), const),
            pl.BlockSpec((SEG, SEG), const),
        ],
        out_specs=[pl.BlockSpec((tm, N_IN), row)] + [pl.BlockSpec((tm // d, d * DA_W), row) for d in DA_DILATED],
        out_shape=[jax.ShapeDtypeStruct((n, N_IN), BF16)]
        + [jax.ShapeDtypeStruct((n // d, d * DA_W), BF16) for d in DA_DILATED],
        scratch_shapes=[pltpu.VMEM((DA_W // 128, tm, 128), F32)],
        compiler_params=_cparams("parallel"),
        name="in_proj",
    )(x, ln_g, w_in, qg, gm)


NA_RB = 8
NA_TOK = NA_RB * GRID_W
NA_GR = 4
NA_GQ = NA_GR * GRID_W
NA_KR = NA_GR + NA_WIN_ROWS
NA_GK = NA_KR * GRID_W
NA_NDR = 2 * NA_WIN_ROWS - 1


def _na_bias_table(rpb):
    cq = np.arange(GRID_W)
    ck = np.arange(GRID_W)
    c0 = np.clip(cq - NA_WIN_COLS // 2, 0, GRID_W - NA_WIN_COLS)
    col_ok = (ck[None, :] >= c0[:, None]) & (ck[None, :] < c0[:, None] + NA_WIN_COLS)
    dc = np.clip(ck[None, :] - cq[:, None], -(NA_WIN_COLS - 1), NA_WIN_COLS - 1) + NA_WIN_COLS - 1
    col_onehot = (dc[:, :, None] == np.arange(2 * NA_WIN_COLS - 1)).astype(np.float32)
    jq = np.arange(NA_GR)
    half = NA_WIN_ROWS // 2
    dq = np.stack([jq, half + jq, NA_WIN_ROWS + jq])
    dw0 = np.stack([0 * jq, jq, half + 0 * jq])
    w = np.arange(NA_KR)
    row_ok = (w[None, None, :] >= dw0[:, :, None]) & (w[None, None, :] < dw0[:, :, None] + NA_WIN_ROWS)
    dr = w[None, None, :] - dq[:, :, None] + NA_WIN_ROWS - 1
    row_onehot = ((dr[..., None] == np.arange(NA_NDR)) & row_ok[..., None]).astype(np.float32)
    t = jnp.einsum('hab,qkb->haqk', rpb.astype(F32), col_onehot, precision=lax.Precision.HIGHEST)
    bias = jnp.einsum('cjwa,haqk->chjqwk', row_onehot, t, precision=lax.Precision.HIGHEST)
    ok = row_ok[:, None, :, None, :, None] & col_ok[None, None, None, :, None, :]
    return jnp.where(ok, bias, NEG_INF).reshape(3, NA_HEADS, NA_GQ, NA_GK)


def _na_kernel(q_ref, kp_ref, kc_ref, kn_ref, vp_ref, vc_ref, vn_ref, bias0_ref, bias1_ref, o_ref, kbuf, vbuf, *,
               rows):
    i = pl.program_id(1)
    kbuf[0:NA_TOK] = kp_ref[...]
    kbuf[NA_TOK:2 * NA_TOK] = kc_ref[...]
    kbuf[2 * NA_TOK:3 * NA_TOK] = kn_ref[...]
    vbuf[0:NA_TOK] = vp_ref[...]
    vbuf[NA_TOK:2 * NA_TOK] = vc_ref[...]
    vbuf[2 * NA_TOK:3 * NA_TOK] = vn_ref[...]
    low = lax.broadcasted_iota(jnp.int32, (NA_GQ, 128), 1) < HEAD_DIM
    for g, bias_ref in enumerate((bias0_ref, bias1_ref)):
        r = i * NA_RB + g * NA_GR
        kstart = jnp.clip(r - NA_WIN_ROWS // 2, 0, rows - NA_KR)
        off = pl.multiple_of((kstart - i * NA_RB + NA_RB) * GRID_W, GRID_W)
        qrows = slice(g * NA_GQ, (g + 1) * NA_GQ)
        for a in range(2):
            cols = slice(a * 128, (a + 1) * 128)
            qa = q_ref[qrows, cols]
            ka = kbuf[pl.ds(off, NA_GK), cols]
            va = vbuf[pl.ds(off, NA_GK), cols]
            q2 = jnp.concatenate([jnp.where(low, qa, jnp.zeros_like(qa)), jnp.where(low, jnp.zeros_like(qa), qa)],
                                 axis=0)
            s = _dot_nt(q2, ka) + jnp.concatenate([bias_ref[0, 2 * a], bias_ref[0, 2 * a + 1]], axis=0)
            m = jnp.max(s, axis=-1, keepdims=True)
            p = jnp.exp(s - m)
            l = jnp.sum(p, axis=-1, keepdims=True)
            pv = _dot(p.astype(BF16), va) / l
            o_ref[qrows, cols] = jnp.where(low, pv[0:NA_GQ], pv[NA_GQ:]).astype(BF16)


def _na_call(proj, bias, b, t, row_base, out):
    rows = t // GRID_W
    nrb = rows // NA_RB
    base = row_base // NA_TOK
    assert rows >= NA_KR and rows % NA_RB == 0

    def spec(col, shift):
        def imap(bi, i):
            return (base + bi * nrb + jnp.clip(i + shift, 0, nrb - 1), col)
        return pl.BlockSpec((NA_TOK, SEG), imap)

    bias_block = (1, NA_HEADS, NA_GQ, NA_GK)
    bias0 = pl.BlockSpec(bias_block, lambda bi, i: (jnp.where(i == 0, 0, 1), 0, 0, 0))
    bias1 = pl.BlockSpec(bias_block, lambda bi, i: (jnp.where(i == nrb - 1, 2, 1), 0, 0, 0))
    in_specs = [spec(0, 0), spec(1, -1), spec(1, 0), spec(1, 1), spec(2, -1), spec(2, 0), spec(2, 1), bias0, bias1]
    body, alias_specs, out_shape, aliases = _into(functools.partial(_na_kernel, rows=rows), len(in_specs), [out])
    return pl.pallas_call(
        body,
        grid=(b, nrb),
        in_specs=in_specs + alias_specs,
        out_specs=[pl.BlockSpec((NA_TOK, SEG), lambda bi, i: (base + bi * nrb + i, 0))],
        out_shape=out_shape,
        input_output_aliases=aliases,
        scratch_shapes=[pltpu.VMEM((3 * NA_TOK, SEG), BF16), pltpu.VMEM((3 * NA_TOK, SEG), BF16)],
        compiler_params=_cparams("parallel", "parallel"),
        name="na",
    )(proj, proj, proj, proj, proj, proj, proj, bias, bias, out)[0]


def _t5_bucket(rel):
    nb = N_BUCKETS // 2
    max_exact = nb // 2
    ret = np.where(rel > 0, nb, 0)
    n = np.abs(rel)
    large = max_exact + (np.log(np.maximum(n, 1) / max_exact)
                         / np.log(MAX_DISTANCE / max_exact) * (nb - max_exact)).astype(np.int32)
    large = np.minimum(large, nb - 1)
    return (ret + np.where(n < max_exact, n, large)).astype(np.int32)


def _da_bias_table(t5_bias, dilation, qb):
    kb = qb + 2 * DA_HALO
    rel = np.arange(kb)[None, :] - np.arange(qb)[:, None] - DA_HALO
    ok = np.abs(rel) <= DA_HALO
    onehot = (_t5_bucket(rel * dilation)[:, :, None] == np.arange(N_BUCKETS)).astype(np.float32)
    bias = jnp.einsum('qkb,bh->hqk', onehot, t5_bias.astype(F32), precision=lax.Precision.HIGHEST)
    return jnp.where(ok[None], bias, NEG_INF)


def _da_kernel(*refs, qb, nb, d, length, prev_dils):
    final = bool(prev_dils)
    (main_ref, left_ref, right_ref, bias_ref), refs = refs[:4], refs[4:]
    if final:
        npv = 2 * len(prev_dils)
        prev_refs, o_ref, buf, und = refs[:npv], refs[npv], refs[npv + 1], refs[npv + 2:]
    else:
        o_ref, lse_ref, buf = refs
    c = pl.program_id(1)
    tr = qb * nb
    kb = qb + 2 * DA_HALO
    buf[0:DA_HALO] = left_ref[...]
    buf[DA_HALO:DA_HALO + tr] = main_ref[...]
    buf[DA_HALO + tr:tr + 2 * DA_HALO] = right_ref[...]
    if final:
        for k, pd in enumerate(prev_dils):
            for part in range(2):
                src, dst = prev_refs[2 * k + part], und[2 * k + part]
                for rho in range(pd):
                    for j in range(SEG // 128):
                        dst[j, pl.ds(rho, tr // pd, stride=pd), :] = \
                            src[:, rho * SEG + j * 128:rho * SEG + (j + 1) * 128]
    low = lax.broadcasted_iota(jnp.int32, (qb, 128), 1) < HEAD_DIM
    for blk in range(nb):
        rows = slice(blk * qb, (blk + 1) * qb)
        krows = slice(blk * qb, blk * qb + kb)
        pos = (c * nb + blk) * qb - DA_HALO + lax.broadcasted_iota(jnp.int32, (qb, kb), 1)
        pos_ok = (pos >= 0) & (pos < length)
        for rho in range(d):
            for a in range(2):
                base = rho * DA_W + a * 128
                qa = main_ref[rows, base:base + 128]
                ka = buf[krows, base + SEG:base + SEG + 128]
                va = buf[krows, base + 2 * SEG:base + 2 * SEG + 128]
                outs, lses = [], []
                for half in range(2):
                    qm = jnp.where(low if half == 0 else ~low, qa, jnp.zeros_like(qa))
                    s = jnp.where(pos_ok, _dot_nt(qm, ka) + bias_ref[2 * a + half], NEG_INF)
                    m = jnp.max(s, axis=-1, keepdims=True)
                    p = jnp.exp(s - m)
                    l = jnp.sum(p, axis=-1, keepdims=True)
                    outs.append(_dot(p.astype(BF16), va) / l)
                    lses.append(jnp.broadcast_to(m + jnp.log(l), (qb, 128)))
                o = jnp.where(low, outs[0], outs[1])
                lse = jnp.where(low, lses[0], lses[1])
                cols = slice(rho * SEG + a * 128, rho * SEG + (a + 1) * 128)
                if final:
                    mx = lse
                    for k in range(len(prev_dils)):
                        mx = jnp.maximum(mx, und[2 * k + 1][a, rows, :])
                    wsum = jnp.exp(lse - mx)
                    mix = wsum * o
                    for k in range(len(prev_dils)):
                        wk = jnp.exp(und[2 * k + 1][a, rows, :] - mx)
                        wsum = wsum + wk
                        mix = mix + wk * und[2 * k][a, rows, :]
                    o_ref[rows, cols] = (mix / wsum).astype(BF16)
                else:
                    o_ref[rows, cols] = o
                    lse_ref[rows, cols] = lse


DA_BLOCKS = {1: 8, 4: 4, 16: 1}


def _da_call(view, colblk, bias, b, t, row_base, dilation, prev=None, prev_dils=(), out=None):
    d = dilation
    length = t // d
    qb = min(128, length)
    nb = min(DA_BLOCKS[d], length // qb)
    tr = qb * nb
    nt = length // tr
    kb = qb + 2 * DA_HALO
    ubase = row_base // d
    hpt = tr // DA_HALO
    nhalo = view.shape[0] // DA_HALO
    final = prev is not None

    def halo(right):
        def imap(bi, c):
            blk = (ubase + bi * length) // DA_HALO + ((c + 1) * hpt if right else c * hpt - 1)
            return (jnp.clip(blk, 0, nhalo - 1), colblk)
        return pl.BlockSpec((DA_HALO, d * DA_W), imap)

    in_specs = [pl.BlockSpec((tr, d * DA_W), lambda bi, c: ((ubase + bi * length) // tr + c, colblk)),
                halo(False), halo(True),
                pl.BlockSpec((DA_HEADS, qb, kb), lambda bi, c: (0, 0, 0))]
    args = [view] * 3 + [bias]
    out_block = pl.BlockSpec((tr, d * SEG), lambda bi, c: (bi * nt + c, 0))
    scratch = [pltpu.VMEM((tr + 2 * DA_HALO, d * DA_W), BF16)]
    if final:
        assert d == 1
        for pd in prev_dils:
            in_specs += [pl.BlockSpec((tr // pd, pd * SEG), lambda bi, c: (bi * nt + c, 0))] * 2
            scratch += [pltpu.VMEM((SEG // 128, tr, 128), F32)] * 2
        args += list(prev)
        kern = functools.partial(_da_kernel, qb=qb, nb=nb, d=d, length=length, prev_dils=tuple(prev_dils))
        body, alias_specs, out_shape, aliases = _into(kern, len(in_specs), [out])
        return pl.pallas_call(
            body,
            grid=(b, nt),
            in_specs=in_specs + alias_specs,
            out_specs=[pl.BlockSpec((tr, SEG), lambda bi, c: ((ubase + bi * length) // tr + c, 0))],
            out_shape=out_shape,
            input_output_aliases=aliases,
            scratch_shapes=scratch,
            compiler_params=_cparams("parallel", "parallel"),
            name="da_d1",
        )(*args, out)[0]
    return pl.pallas_call(
        functools.partial(_da_kernel, qb=qb, nb=nb, d=d, length=length, prev_dils=()),
        grid=(b, nt),
        in_specs=in_specs,
        out_specs=[out_block, out_block],
        out_shape=[jax.ShapeDtypeStruct((b * t // d, d * SEG), F32)] * 2,
        scratch_shapes=scratch,
        compiler_params=_cparams("parallel", "parallel"),
        name="da_d%d" % d,
    )(*args)


HG_LEVELS = (32, 16, 8, 4, 2, 1)
HG_NSEC = 2 + len(HG_LEVELS)
HG_W = HG_HEADS * HG_DK
HG_PAIR = 2 * HG_DK
LOG2_E = math.log2(math.e)


def _hg_constants(rev):
    c = HG_CHUNK
    t = np.arange(c)[:, None]
    j = np.arange(c)[None, :]
    secs = []
    if not rev:
        secs.append(j <= t)
        secs.append(j > t)
    else:
        secs.append(j >= t)
        secs.append(j < t)
    masks = [np.eye(c, dtype=bool)]
    s = np.arange(c)[None, :]
    for m in HG_LEVELS:
        p0 = (t // (2 * m)) * (2 * m)
        upper = (t & m) != 0
        if not rev:
            mid = p0 + m - 1
            sec = np.where(upper, (j > mid) & (j <= t), (j > t) & (j <= mid))
            pair = ((t // (2 * m)) == (s // (2 * m))) & upper & ((s & m) == 0)
        else:
            mid = p0 + m
            sec = np.where(upper, (j >= mid) & (j < t), (j >= t) & (j < mid))
            pair = ((t // (2 * m)) == (s // (2 * m))) & (~upper) & ((s & m) != 0)
        secs.append(sec)
        masks.append(pair)
    w = np.tile(np.concatenate(secs, axis=0).astype(np.float32), (1, 2))
    masks = np.tile(np.stack(masks).astype(np.float32), (1, 1, 2))
    return jnp.asarray(w, BF16), jnp.asarray(masks)


def _bdiag(h0, h1):
    zero = jnp.zeros_like(h0)
    return jnp.concatenate([jnp.concatenate([h0, zero], axis=1), jnp.concatenate([zero, h1], axis=1)], axis=0)


def _hg_pair_operand(a, p):
    return _bdiag(a[:, (2 * p) * HG_DK:(2 * p + 1) * HG_DK], a[:, (2 * p + 1) * HG_DK:(2 * p + 2) * HG_DK])


def _hg_prepare(rev, z, x, v, lb, w_ref, mask_ref):
    c_sz = HG_CHUNK
    log_lb = jnp.log(lb)
    log_1m = jnp.log1p(-lb)
    log_sig = jnp.minimum(z, 0.0) - jnp.log(1.0 + jnp.exp(-jnp.abs(z)))
    cc = log_1m + log_sig
    logf = jnp.maximum(log_lb, cc) + jnp.log(1.0 + jnp.exp(-jnp.abs(log_lb - cc)))
    kk = (1.0 - lb) * jnp.exp(log_sig - z)
    q = x / (1.0 + jnp.exp(-x))
    log2f = logf * LOG2_E
    hi = log2f.astype(BF16)
    lo = (log2f - hi.astype(F32)).astype(BF16)
    dec = jnp.exp2(_dot(w_ref[...], jnp.concatenate([hi, lo], axis=0)))
    sub = lax.broadcasted_iota(jnp.int32, (1, 8, HG_W), 1)
    ys = []
    for lvl, m in enumerate(HG_LEVELS):
        if m >= 8:
            sel = jnp.concatenate(
                [(q if (((8 * r) & m) != 0) != rev else kk)[8 * r:8 * r + 8] for r in range(c_sz // 8)], axis=0)
        else:
            is_q = ((sub & m) == 0) if rev else ((sub & m) != 0)
            sel = jnp.where(is_q, q.reshape(c_sz // 8, 8, HG_W), kk.reshape(c_sz // 8, 8, HG_W)).reshape(c_sz, HG_W)
        ys.append((sel * dec[(2 + lvl) * c_sz:(3 + lvl) * c_sz]).astype(BF16))
    q16, k16 = q.astype(BF16), kk.astype(BF16)
    pairs = []
    for p in range(HG_HEADS // 2):
        cols = slice(p * HG_PAIR, (p + 1) * HG_PAIR)
        a2 = mask_ref[0] * _dot_nt(q16[:, cols], _hg_pair_operand(k16, p))
        for lvl in range(len(HG_LEVELS)):
            a2 = a2 + mask_ref[1 + lvl] * _dot_nt(ys[lvl][:, cols], _hg_pair_operand(ys[lvl], p))
        pairs.append(a2.astype(BF16))
    edge = 0 if rev else c_sz - 1
    return dict(pairs=pairs, v=v, qd=(q * dec[0:c_sz]).astype(BF16), kd=(kk * dec[c_sz:2 * c_sz]).astype(BF16),
                decay=dec[edge:edge + 1])


def _hg_finish(prep, st_ref):
    v, qd, kd = prep["v"], prep["qd"], prep["kd"]
    outs = []
    for p in range(HG_HEADS // 2):
        cols = slice(p * HG_PAIR, (p + 1) * HG_PAIR)
        st = _bdiag(st_ref[2 * p].astype(BF16), st_ref[2 * p + 1].astype(BF16))
        outs.append(_dot(prep["pairs"][p], _hg_pair_operand(v, p)) + _dot_nt(qd[:, cols], st))
        for h in (2 * p, 2 * p + 1):
            hc = slice(h * HG_DK, (h + 1) * HG_DK)
            st_ref[h] = st_ref[h] * prep["decay"][:, hc] + _dot_tn(v[:, hc], kd[:, hc])
    return jnp.concatenate(outs, axis=1)


def _hg_kernel(cqf_ref, cff_ref, cif_ref, cqb_ref, cfb_ref, cib_ref, lb_ref, wf_ref, wb_ref, mf_ref, mb_ref,
               of_ref, ob_ref, st_ref, *, nchunks):
    @pl.when(pl.program_id(1) == 0)
    def _():
        st_ref[...] = jnp.zeros_like(st_ref)

    def prepare(step):
        rf = slice(step * HG_CHUNK, (step + 1) * HG_CHUNK)
        rb = slice((nchunks - 1 - step) * HG_CHUNK, (nchunks - step) * HG_CHUNK)
        fwd = _hg_prepare(False, cff_ref[rf, :].astype(F32), cqf_ref[rf, :].astype(F32), cif_ref[rf, :],
                          lb_ref[0:1, :], wf_ref, mf_ref)
        bwd = _hg_prepare(True, cfb_ref[rb, :].astype(F32), cqb_ref[rb, :].astype(F32), cib_ref[rb, :],
                          lb_ref[1:2, :], wb_ref, mb_ref)
        return rf, fwd, rb, bwd

    def finish(rf, fwd, rb, bwd):
        of_ref[rf, :] = _hg_finish(fwd, st_ref.at[0]).astype(BF16)
        ob_ref[rb, :] = _hg_finish(bwd, st_ref.at[1]).astype(BF16)

    pending = [prepare(step) for step in range(nchunks)]
    for prep in pending:
        finish(*prep)


def _hg_call(proj, lb, consts_f, consts_b, b, t, row_base, outs):
    tb = min(256, t)
    nt = t // tb
    base = row_base // tb

    def pspec(col, rev):
        return pl.BlockSpec((tb, HG_W), lambda bi, i: (base + bi * nt + (nt - 1 - i if rev else i), col))

    def const(a):
        return pl.BlockSpec(a.shape, lambda bi, i: (0,) * a.ndim)

    (wf, mf), (wb, mb) = consts_f, consts_b
    in_specs = [pspec(3, False), pspec(4, False), pspec(6, False), pspec(3, True), pspec(5, True), pspec(6, True),
                const(lb), const(wf), const(wb), const(mf), const(mb)]
    body, alias_specs, out_shape, aliases = _into(functools.partial(_hg_kernel, nchunks=tb // HG_CHUNK),
                                                  len(in_specs), list(outs))
    return pl.pallas_call(
        body,
        grid=(b, nt),
        in_specs=in_specs + alias_specs,
        out_specs=[pspec(0, False), pspec(0, True)],
        out_shape=out_shape,
        input_output_aliases=aliases,
        scratch_shapes=[pltpu.VMEM((2, HG_HEADS, HG_DK, HG_DK), F32)],
        compiler_params=_cparams("parallel", "arbitrary"),
        name="hgrn",
    )(proj, proj, proj, proj, proj, proj, lb, wf, wb, mf, mb, *outs)


def _outproj_kernel(oa_ref, ob_ref, cf_ref, cb_ref, cg_ref, ng_ref, x_ref, w_ref, g_ref, x1_ref, h_ref):
    tot = cf_ref[...].astype(F32) + cb_ref[...].astype(F32)
    gate = cg_ref[...].astype(F32)
    parts = []
    for h in range(HG_HEADS):
        th = tot[:, h * HG_DK:(h + 1) * HG_DK]
        ms = jnp.mean(th * th, axis=-1, keepdims=True)
        parts.append(th * lax.rsqrt(ms + NORM_EPS) * ng_ref[...])
    oc = (jnp.concatenate(parts, axis=1) * (gate / (1.0 + jnp.exp(-gate)))).astype(BF16)
    y = _dot(oa_ref[...], w_ref[0:SEG, :]) + _dot(ob_ref[...], w_ref[SEG:2 * SEG, :]) + _dot(oc, w_ref[2 * SEG:, :])
    x1 = x_ref[...] + y
    x1_ref[...] = x1
    ms = jnp.mean(x1 * x1, axis=-1, keepdims=True)
    h_ref[...] = (x1 * lax.rsqrt(ms + NORM_EPS) * g_ref[...]).astype(BF16)


def _out_proj(oa, ob, ocf, ocb, proj, norm_g, x, w_out, ln_g, tm):
    n = x.shape[0]
    row = lambda i: (i, 0)
    const = lambda i: (0, 0)
    return pl.pallas_call(
        _outproj_kernel,
        grid=(n // tm,),
        in_specs=[pl.BlockSpec((tm, SEG), row), pl.BlockSpec((tm, SEG), row),
                  pl.BlockSpec((tm, HG_W), row), pl.BlockSpec((tm, HG_W), row),
                  pl.BlockSpec((tm, HG_W), lambda i: (i, N_IN // HG_W - 1)),
                  pl.BlockSpec((1, HG_DK), const),
                  pl.BlockSpec((tm, D_MODEL), row),
                  pl.BlockSpec((D_MODEL, D_MODEL), const),
                  pl.BlockSpec((1, D_MODEL), const)],
        out_specs=[pl.BlockSpec((tm, D_MODEL), row), pl.BlockSpec((tm, D_MODEL), row)],
        out_shape=[jax.ShapeDtypeStruct((n, D_MODEL), F32), jax.ShapeDtypeStruct((n, D_MODEL), BF16)],
        compiler_params=_cparams("parallel"),
        name="out_proj",
    )(oa, ob, ocf, ocb, proj, norm_g, x, w_out, ln_g)


FF_HALO = 16


def _ffn_kernel(x_ref, h_ref, hp_ref, hn_ref, wup_ref, cw_ref, cb_ref, wdn_ref, o_ref, hbuf, *, tm, row_base, starts,
                ends):
    row0 = row_base + pl.program_id(0) * tm
    at_start = functools.reduce(jnp.logical_or, [row0 == s for s in starts])
    at_end = functools.reduce(jnp.logical_or, [row0 + tm == e for e in ends])
    keep_prev = jnp.where(at_start, 0.0, 1.0)
    keep_next = jnp.where(at_end, 0.0, 1.0)
    hbuf[0:tm] = h_ref[...]
    hbuf[tm:tm + FF_HALO] = hp_ref[...]
    hbuf[tm + FF_HALO:tm + 2 * FF_HALO] = hn_ref[...]
    first = lax.broadcasted_iota(jnp.int32, (8, FF_SEG), 0) == 0
    last = lax.broadcasted_iota(jnp.int32, (8, FF_SEG), 0) == 7
    c_tanh = math.sqrt(2.0 / math.pi)
    o_ref[...] = x_ref[...]
    nseg = D_FF // FF_SEG

    def up_proj(j):
        return _dot(hbuf[...], wup_ref[:, 2 * j * FF_SEG:2 * (j + 1) * FF_SEG])

    y_next = up_proj(0)
    for j in range(nseg):
        cols = slice(j * FF_SEG, (j + 1) * FF_SEG)
        y = y_next
        if j + 1 < nseg:
            y_next = up_proj(j + 1)
        g, up = y[0:tm, 0:FF_SEG], y[0:tm, FF_SEG:]
        g_prev = y[tm + FF_HALO - 1:tm + FF_HALO, 0:FF_SEG] * keep_prev
        g_next = y[tm + FF_HALO:tm + FF_HALO + 1, 0:FF_SEG] * keep_next
        cw0, cw1, cw2 = cw_ref[0:1, cols], cw_ref[1:2, cols], cw_ref[2:3, cols]
        gate = pltpu.roll(g, 1, 0) * cw0 + g * cw1 + pltpu.roll(g, tm - 1, 0) * cw2 + cb_ref[:, cols]
        top = gate[0:8] + jnp.where(first, (g_prev - g[tm - 1:tm]) * cw0, 0.0)
        bot = gate[tm - 8:tm] + jnp.where(last, (g_next - g[0:1]) * cw2, 0.0)
        gate = jnp.concatenate([top, gate[8:tm - 8], bot], axis=0)
        inner = gate * (c_tanh + (c_tanh * 0.044715) * (gate * gate))
        act = (gate * up) * (1.0 + jnp.tanh(inner))
        o_ref[...] += _dot(act.astype(BF16), wdn_ref[cols, :])


def _ffn(x1, h2, w_up, conv_w, conv_b, w_down, tm, starts, ends, row_base=0, nrows=None):
    n = x1.shape[0]
    nrows = n if nrows is None else nrows
    hpb = tm // FF_HALO
    nh = n // FF_HALO
    t0 = row_base // tm
    row = lambda i: (t0 + i, 0)
    const = lambda i: (0, 0)
    return pl.pallas_call(
        functools.partial(_ffn_kernel, tm=tm, row_base=row_base, starts=starts, ends=ends),
        grid=(nrows // tm,),
        in_specs=[pl.BlockSpec((tm, D_MODEL), row), pl.BlockSpec((tm, D_MODEL), row),
                  pl.BlockSpec((FF_HALO, D_MODEL), lambda i: (jnp.maximum((t0 + i) * hpb - 1, 0), 0)),
                  pl.BlockSpec((FF_HALO, D_MODEL), lambda i: (jnp.minimum((t0 + i + 1) * hpb, nh - 1), 0)),
                  pl.BlockSpec((D_MODEL, 2 * D_FF), const),
                  pl.BlockSpec((3, D_FF), const), pl.BlockSpec((1, D_FF), const),
                  pl.BlockSpec((D_FF, D_MODEL), const)],
        out_specs=pl.BlockSpec((tm, D_MODEL), lambda i: (i, 0)),
        out_shape=jax.ShapeDtypeStruct((nrows, D_MODEL), F32),
        scratch_shapes=[pltpu.VMEM((tm + 2 * FF_HALO, D_MODEL), BF16)],
        compiler_params=_cparams("parallel"),
        name="ffn",
    )(x1, h2, h2, h2, w_up, conv_w, conv_b, w_down)


def _ffn_weights(w_up, w_down):
    nseg = D_FF // FF_SEG
    w_up = w_up.astype(BF16).reshape(D_MODEL, 2, nseg, FF_SEG).transpose(0, 2, 1, 3).reshape(D_MODEL, 2 * D_FF)
    return w_up, w_down.astype(BF16) * 0.5


def _lower_bounds(lb_logits):
    p = jax.nn.softmax(lb_logits.astype(F32), axis=1)
    c = jnp.cumsum(p, axis=1)
    return c - c[:, :1]


def _row_tile(groups):
    tm = 512
    while any(t % tm for _, t, _ in groups):
        tm //= 2
    return tm


def kernel(x_prompt, x_sample, ln_mix_g, w_in, na_q_g, na_k_g, na_rpb, da_q_g, da_k_g, t5_bias, hg_lb_logits,
           hg_norm_g, w_out, ln_ffn_g, w_up, conv_w, conv_b, w_down):
    depth = w_in.shape[0]
    groups = []
    base = 0
    for xg in (x_prompt, x_sample):
        b, t, _ = xg.shape
        groups.append((b, t, base))
        base += b * t
    starts = tuple(rb + i * t for b, t, rb in groups for i in range(b))
    ends = tuple(rb + (i + 1) * t for b, t, rb in groups for i in range(b))
    tm = _row_tile(groups)
    n = base
    x = jnp.concatenate([x_prompt.reshape(-1, D_MODEL), x_sample.reshape(-1, D_MODEL)], axis=0)

    lb = _lower_bounds(hg_lb_logits)
    scale = HEAD_DIM ** -0.5
    tile4 = lambda g: jnp.tile(g.astype(F32), NA_HEADS)
    gm = jnp.asarray(np.kron(np.eye(SEG // HEAD_DIM), np.ones((HEAD_DIM, HEAD_DIM))) / HEAD_DIM, BF16)
    hg_fwd_consts = _hg_constants(False)
    hg_bwd_consts = _hg_constants(True)
    da_bias = {}

    def da_bias_for(d, qb):
        if (d, qb) not in da_bias:
            da_bias[(d, qb)] = _da_bias_table(t5_bias, d, qb)
        return da_bias[(d, qb)]

    for l in range(depth):
        qg = jnp.stack([tile4(na_q_g[l]) * scale, tile4(na_k_g[l]), tile4(da_q_g[l]) * scale, tile4(da_k_g[l])])
        proj, *dil_views = _in_proj(x, ln_mix_g[l].reshape(1, -1), w_in[l].astype(BF16), qg, gm, tm)
        na_bias = _na_bias_table(na_rpb[l])
        oa = jnp.zeros((n, SEG), BF16)
        ob = jnp.zeros((n, SEG), BF16)
        ocf = jnp.zeros((n, HG_W), BF16)
        ocb = jnp.zeros((n, HG_W), BF16)
        for b, t, rb in groups:
            oa = _na_call(proj, na_bias, b, t, rb, oa)
            prev = []
            for d, view in zip(DA_DILATED, dil_views):
                prev += _da_call(view, 0, da_bias_for(d, min(128, t // d)), b, t, rb, d)
            ob = _da_call(proj, DA_SEGS[0] * SEG // DA_W, da_bias_for(1, min(128, t)), b, t, rb, 1,
                          prev=prev, prev_dils=DA_DILATED, out=ob)
            ocf, ocb = _hg_call(proj, lb[:, l], hg_fwd_consts, hg_bwd_consts, b, t, rb, (ocf, ocb))
        x1, h2 = _out_proj(oa, ob, ocf, ocb, proj, hg_norm_g[l].reshape(1, -1).astype(F32), x, w_out[l].astype(BF16),
                           ln_ffn_g[l].reshape(1, -1), tm)
        wu, wd = _ffn_weights(w_up[l], w_down[l])
        cw, cb = conv_w[l], conv_b[l].reshape(1, -1)
        if l + 1 < depth:
            x = _ffn(x1, h2, wu, cw, cb, wd, tm, starts, ends)
    return tuple(_ffn(x1, h2, wu, cw, cb, wd, tm, starts, ends, rb, b * t).reshape(xg.shape)
                 for (b, t, rb), xg in zip(groups, (x_prompt, x_sample)))
```

```python
import functools
import math

import numpy as np
import jax
import jax.numpy as jnp
from jax import lax
from jax.experimental import pallas as pl
from jax.experimental.pallas import tpu as pltpu

F32 = jnp.float32
BF16 = jnp.bfloat16

D_MODEL = 1024
GRID_W = 64
HEAD_DIM = 64
NA_HEADS = 4
NA_WIN_ROWS = 8
NA_WIN_COLS = 16
DA_HEADS = 4
DA_CONFIGS = ((128, 1), (512, 4), (2048, 16))
DA_HALO = 64
N_BUCKETS = 32
MAX_DISTANCE = 1024
HG_HEADS = 4
HG_DK = 128
HG_CHUNK = 64
N_IN = 4096
D_FF = 2816
NORM_EPS = 1e-6
NEG_INF = -1e30
SEG = 256
FF_SEG = 256
VMEM_LIMIT = 56 * 1024 * 1024


def _cparams(*sem):
    return pltpu.CompilerParams(dimension_semantics=sem, vmem_limit_bytes=VMEM_LIMIT)


def _into(kernel, n_in, outs):
    def body(*refs):
        return kernel(*refs[:n_in], *refs[n_in + len(outs):])
    specs = [pl.BlockSpec(memory_space=pl.ANY)] * len(outs)
    shapes = [jax.ShapeDtypeStruct(o.shape, o.dtype) for o in outs]
    return body, specs, shapes, {n_in + k: k for k in range(len(outs))}


def _dot(a, b):
    return jnp.dot(a, b, preferred_element_type=F32)


def _dot_nt(a, b):
    return lax.dot_general(a, b, (((1,), (1,)), ((), ())), preferred_element_type=F32)


def _dot_tn(a, b):
    return lax.dot_general(a, b, (((0,), (0,)), ((), ())), preferred_element_type=F32)


DA_SEGS = (3, 4, 5)
DA_W = len(DA_SEGS) * SEG
DA_DILATED = tuple(d for _, d in DA_CONFIGS if d > 1)


def _select_rows(x_refs, split):
    if len(x_refs) == 1:
        return x_refs[0][...]
    return jnp.where(pl.program_id(0) < split, x_refs[0][...], x_refs[1][...])


def _row_specs(xs, tm):
    if len(xs) == 1:
        return [pl.BlockSpec((tm, xs[0].shape[1]), lambda i: (i, 0))], 0
    split = xs[0].shape[0] // tm
    return [pl.BlockSpec((tm, xs[0].shape[1]), lambda i: (jnp.minimum(i, split - 1), 0)),
            pl.BlockSpec((tm, xs[1].shape[1]), lambda i: (jnp.maximum(i - split, 0), 0))], split


def _inproj_kernel(*refs, tm, nx, split):
    x_refs, (g_ref, w_ref, qg_ref, gm_ref, o_ref), rest = refs[:nx], refs[nx:nx + 5], refs[nx + 5:]
    dil_refs, ybuf = rest[:-1], rest[-1]
    x = _select_rows(x_refs, split)
    ms = jnp.mean(x * x, axis=-1, keepdims=True)
    h = (x * lax.rsqrt(ms + NORM_EPS) * g_ref[...]).astype(BF16)
    normed = {0: 0, 1: 1, 3: 2, 4: 3}
    nseg = N_IN // SEG
    y_next = _dot(h, w_ref[:, 0:SEG])
    for seg in range(nseg):
        y = y_next
        if seg + 1 < nseg:
            y_next = _dot(h, w_ref[:, (seg + 1) * SEG:(seg + 2) * SEG])
        if seg in normed:
            ss = _dot((y * y).astype(BF16), gm_ref[...])
            r = normed[seg]
            y = y * lax.rsqrt(ss + NORM_EPS) * qg_ref[r:r + 1, :]
        if seg in DA_SEGS:
            k = DA_SEGS.index(seg)
            for j in range(SEG // 128):
                ybuf[k * (SEG // 128) + j] = y[:, j * 128:(j + 1) * 128]
        o_ref[:, seg * SEG:(seg + 1) * SEG] = y.astype(BF16)
    for d, ref in zip(DA_DILATED, dil_refs):
        for rho in range(d):
            for j in range(DA_W // 128):
                ref[:, rho * DA_W + j * 128:rho * DA_W + (j + 1) * 128] = \
                    ybuf[j, pl.ds(rho, tm // d, stride=d), :].astype(BF16)


def _in_proj(xs, ln_g, w_in, qg, gm, tm):
    n = sum(x.shape[0] for x in xs)
    row = lambda i: (i, 0)
    const = lambda i: (0, 0)
    x_specs, split = _row_specs(xs, tm)
    return pl.pallas_call(
        functools.partial(_inproj_kernel, tm=tm, nx=len(xs), split=split),
        grid=(n // tm,),
        in_specs=x_specs + [
            pl.BlockSpec((1, D_MODEL), const),
            pl.BlockSpec((D_MODEL, N_IN), const),
            pl.BlockSpec((4, SEG), const),
            pl.BlockSpec((SEG, SEG), const),
        ],
        out_specs=[pl.BlockSpec((tm, N_IN), row)] + [pl.BlockSpec((tm // d, d * DA_W), row) for d in DA_DILATED],
        out_shape=[jax.ShapeDtypeStruct((n, N_IN), BF16)]
        + [jax.ShapeDtypeStruct((n // d, d * DA_W), BF16) for d in DA_DILATED],
        scratch_shapes=[pltpu.VMEM((DA_W // 128, tm, 128), F32)],
        compiler_params=_cparams("parallel"),
        name="in_proj",
    )(*xs, ln_g, w_in, qg, gm)


NA_RB = 8
NA_TOK = NA_RB * GRID_W
NA_GR = 4
NA_GQ = NA_GR * GRID_W
NA_KR = NA_GR + NA_WIN_ROWS
NA_GK = NA_KR * GRID_W
NA_NDR = 2 * NA_WIN_ROWS - 1


def _na_bias_table(rpb):
    cq = np.arange(GRID_W)
    ck = np.arange(GRID_W)
    c0 = np.clip(cq - NA_WIN_COLS // 2, 0, GRID_W - NA_WIN_COLS)
    col_ok = (ck[None, :] >= c0[:, None]) & (ck[None, :] < c0[:, None] + NA_WIN_COLS)
    dc = np.clip(ck[None, :] - cq[:, None], -(NA_WIN_COLS - 1), NA_WIN_COLS - 1) + NA_WIN_COLS - 1
    col_onehot = (dc[:, :, None] == np.arange(2 * NA_WIN_COLS - 1)).astype(np.float32)
    jq = np.arange(NA_GR)
    half = NA_WIN_ROWS // 2
    dq = np.stack([jq, half + jq, NA_WIN_ROWS + jq])
    dw0 = np.stack([0 * jq, jq, half + 0 * jq])
    w = np.arange(NA_KR)
    row_ok = (w[None, None, :] >= dw0[:, :, None]) & (w[None, None, :] < dw0[:, :, None] + NA_WIN_ROWS)
    dr = w[None, None, :] - dq[:, :, None] + NA_WIN_ROWS - 1
    row_onehot = ((dr[..., None] == np.arange(NA_NDR)) & row_ok[..., None]).astype(np.float32)
    t = jnp.einsum('hab,qkb->haqk', rpb.astype(F32), col_onehot, precision=lax.Precision.HIGHEST)
    bias = jnp.einsum('cjwa,haqk->chjqwk', row_onehot, t, precision=lax.Precision.HIGHEST)
    ok = row_ok[:, None, :, None, :, None] & col_ok[None, None, None, :, None, :]
    return jnp.where(ok, bias, NEG_INF).reshape(3, NA_HEADS, NA_GQ, NA_GK)


def _na_kernel(q_ref, kp_ref, kc_ref, kn_ref, vp_ref, vc_ref, vn_ref, bias0_ref, bias1_ref, o_ref, kbuf, vbuf, *,
               rows):
    i = pl.program_id(1)
    kbuf[0:NA_TOK] = kp_ref[...]
    kbuf[NA_TOK:2 * NA_TOK] = kc_ref[...]
    kbuf[2 * NA_TOK:3 * NA_TOK] = kn_ref[...]
    vbuf[0:NA_TOK] = vp_ref[...]
    vbuf[NA_TOK:2 * NA_TOK] = vc_ref[...]
    vbuf[2 * NA_TOK:3 * NA_TOK] = vn_ref[...]
    low = lax.broadcasted_iota(jnp.int32, (NA_GQ, 128), 1) < HEAD_DIM
    for g, bias_ref in enumerate((bias0_ref, bias1_ref)):
        r = i * NA_RB + g * NA_GR
        kstart = jnp.clip(r - NA_WIN_ROWS // 2, 0, rows - NA_KR)
        off = pl.multiple_of((kstart - i * NA_RB + NA_RB) * GRID_W, GRID_W)
        qrows = slice(g * NA_GQ, (g + 1) * NA_GQ)
        for a in range(2):
            cols = slice(a * 128, (a + 1) * 128)
            qa = q_ref[qrows, cols]
            ka = kbuf[pl.ds(off, NA_GK), cols]
            va = vbuf[pl.ds(off, NA_GK), cols]
            q2 = jnp.concatenate([jnp.where(low, qa, jnp.zeros_like(qa)), jnp.where(low, jnp.zeros_like(qa), qa)],
                                 axis=0)
            s = _dot_nt(q2, ka) + jnp.concatenate([bias_ref[0, 2 * a], bias_ref[0, 2 * a + 1]], axis=0)
            m = jnp.max(s, axis=-1, keepdims=True)
            p = jnp.exp(s - m)
            l = jnp.sum(p, axis=-1, keepdims=True)
            pv = _dot(p.astype(BF16), va) / l
            o_ref[qrows, cols] = jnp.where(low, pv[0:NA_GQ], pv[NA_GQ:]).astype(BF16)


def _na_call(proj, bias, b, t, row_base, out):
    rows = t // GRID_W
    nrb = rows // NA_RB
    base = row_base // NA_TOK
    assert rows >= NA_KR and rows % NA_RB == 0

    def spec(col, shift):
        def imap(bi, i):
            return (base + bi * nrb + jnp.clip(i + shift, 0, nrb - 1), col)
        return pl.BlockSpec((NA_TOK, SEG), imap)

    bias_block = (1, NA_HEADS, NA_GQ, NA_GK)
    bias0 = pl.BlockSpec(bias_block, lambda bi, i: (jnp.where(i == 0, 0, 1), 0, 0, 0))
    bias1 = pl.BlockSpec(bias_block, lambda bi, i: (jnp.where(i == nrb - 1, 2, 1), 0, 0, 0))
    in_specs = [spec(0, 0), spec(1, -1), spec(1, 0), spec(1, 1), spec(2, -1), spec(2, 0), spec(2, 1), bias0, bias1]
    body, alias_specs, out_shape, aliases = _into(functools.partial(_na_kernel, rows=rows), len(in_specs), [out])
    return pl.pallas_call(
        body,
        grid=(b, nrb),
        in_specs=in_specs + alias_specs,
        out_specs=[pl.BlockSpec((NA_TOK, SEG), lambda bi, i: (base + bi * nrb + i, 0))],
        out_shape=out_shape,
        input_output_aliases=aliases,
        scratch_shapes=[pltpu.VMEM((3 * NA_TOK, SEG), BF16), pltpu.VMEM((3 * NA_TOK, SEG), BF16)],
        compiler_params=_cparams("parallel", "parallel"),
        name="na",
    )(proj, proj, proj, proj, proj, proj, proj, bias, bias, out)[0]


def _t5_bucket(rel):
    nb = N_BUCKETS // 2
    max_exact = nb // 2
    ret = np.where(rel > 0, nb, 0)
    n = np.abs(rel)
    large = max_exact + (np.log(np.maximum(n, 1) / max_exact)
                         / np.log(MAX_DISTANCE / max_exact) * (nb - max_exact)).astype(np.int32)
    large = np.minimum(large, nb - 1)
    return (ret + np.where(n < max_exact, n, large)).astype(np.int32)


def _da_bias_table(t5_bias, dilation, qb):
    kb = qb + 2 * DA_HALO
    rel = np.arange(kb)[None, :] - np.arange(qb)[:, None] - DA_HALO
    ok = np.abs(rel) <= DA_HALO
    onehot = (_t5_bucket(rel * dilation)[:, :, None] == np.arange(N_BUCKETS)).astype(np.float32)
    bias = jnp.einsum('qkb,bh->hqk', onehot, t5_bias.astype(F32), precision=lax.Precision.HIGHEST)
    return jnp.where(ok[None], bias, NEG_INF)


def _da_kernel(*refs, qb, nb, d, length, prev_dils):
    final = bool(prev_dils)
    (main_ref, left_ref, right_ref, bias_ref), refs = refs[:4], refs[4:]
    if final:
        npv = 2 * len(prev_dils)
        prev_refs, o_ref, buf, und = refs[:npv], refs[npv], refs[npv + 1], refs[npv + 2:]
    else:
        o_ref, lse_ref, buf = refs
    c = pl.program_id(1)
    tr = qb * nb
    kb = qb + 2 * DA_HALO
    buf[0:DA_HALO] = left_ref[...]
    buf[DA_HALO:DA_HALO + tr] = main_ref[...]
    buf[DA_HALO + tr:tr + 2 * DA_HALO] = right_ref[...]
    if final:
        for k, pd in enumerate(prev_dils):
            for part in range(2):
                src, dst = prev_refs[2 * k + part], und[2 * k + part]
                for rho in range(pd):
                    for j in range(SEG // 128):
                        dst[j, pl.ds(rho, tr // pd, stride=pd), :] = \
                            src[:, rho * SEG + j * 128:rho * SEG + (j + 1) * 128]
    low = lax.broadcasted_iota(jnp.int32, (qb, 128), 1) < HEAD_DIM
    for blk in range(nb):
        rows = slice(blk * qb, (blk + 1) * qb)
        krows = slice(blk * qb, blk * qb + kb)
        pos = (c * nb + blk) * qb - DA_HALO + lax.broadcasted_iota(jnp.int32, (2 * qb, kb), 1)
        pos_ok2 = (pos >= 0) & (pos < length)
        for rho in range(d):
            for a in range(2):
                base = rho * DA_W + a * 128
                qa = main_ref[rows, base:base + 128]
                ka = buf[krows, base + SEG:base + SEG + 128]
                va = buf[krows, base + 2 * SEG:base + 2 * SEG + 128]
                q2 = jnp.concatenate([jnp.where(low, qa, jnp.zeros_like(qa)),
                                      jnp.where(low, jnp.zeros_like(qa), qa)], axis=0)
                bias = jnp.concatenate([bias_ref[2 * a], bias_ref[2 * a + 1]], axis=0)
                s = jnp.where(pos_ok2, _dot_nt(q2, ka) + bias, NEG_INF)
                m = jnp.max(s, axis=-1, keepdims=True)
                p = jnp.exp(s - m)
                l = jnp.sum(p, axis=-1, keepdims=True)
                pv = _dot(p.astype(BF16), va) / l
                lse2 = jnp.broadcast_to(m + jnp.log(l), (2 * qb, 128))
                o = jnp.where(low, pv[0:qb], pv[qb:])
                lse = jnp.where(low, lse2[0:qb], lse2[qb:])
                cols = slice(rho * SEG + a * 128, rho * SEG + (a + 1) * 128)
                if final:
                    mx = lse
                    for k in range(len(prev_dils)):
                        mx = jnp.maximum(mx, und[2 * k + 1][a, rows, :])
                    wsum = jnp.exp(lse - mx)
                    mix = wsum * o
                    for k in range(len(prev_dils)):
                        wk = jnp.exp(und[2 * k + 1][a, rows, :] - mx)
                        wsum = wsum + wk
                        mix = mix + wk * und[2 * k][a, rows, :]
                    o_ref[rows, cols] = (mix / wsum).astype(BF16)
                else:
                    o_ref[rows, cols] = o
                    lse_ref[rows, cols] = lse


DA_BLOCKS = {1: 8, 4: 4, 16: 1}


def _da_call(view, colblk, bias, b, t, row_base, dilation, prev=None, prev_dils=(), out=None):
    d = dilation
    length = t // d
    qb = min(128, length)
    nb = min(DA_BLOCKS[d], length // qb)
    tr = qb * nb
    nt = length // tr
    kb = qb + 2 * DA_HALO
    ubase = row_base // d
    hpt = tr // DA_HALO
    nhalo = view.shape[0] // DA_HALO
    final = prev is not None

    def halo(right):
        def imap(bi, c):
            blk = (ubase + bi * length) // DA_HALO + ((c + 1) * hpt if right else c * hpt - 1)
            return (jnp.clip(blk, 0, nhalo - 1), colblk)
        return pl.BlockSpec((DA_HALO, d * DA_W), imap)

    in_specs = [pl.BlockSpec((tr, d * DA_W), lambda bi, c: ((ubase + bi * length) // tr + c, colblk)),
                halo(False), halo(True),
                pl.BlockSpec((DA_HEADS, qb, kb), lambda bi, c: (0, 0, 0))]
    args = [view] * 3 + [bias]
    out_block = pl.BlockSpec((tr, d * SEG), lambda bi, c: (bi * nt + c, 0))
    scratch = [pltpu.VMEM((tr + 2 * DA_HALO, d * DA_W), BF16)]
    if final:
        assert d == 1
        for pd in prev_dils:
            in_specs += [pl.BlockSpec((tr // pd, pd * SEG), lambda bi, c: (bi * nt + c, 0))] * 2
            scratch += [pltpu.VMEM((SEG // 128, tr, 128), F32)] * 2
        args += list(prev)
        kern = functools.partial(_da_kernel, qb=qb, nb=nb, d=d, length=length, prev_dils=tuple(prev_dils))
        body, alias_specs, out_shape, aliases = _into(kern, len(in_specs), [out])
        return pl.pallas_call(
            body,
            grid=(b, nt),
            in_specs=in_specs + alias_specs,
            out_specs=[pl.BlockSpec((tr, SEG), lambda bi, c: ((ubase + bi * length) // tr + c, 0))],
            out_shape=out_shape,
            input_output_aliases=aliases,
            scratch_shapes=scratch,
            compiler_params=_cparams("parallel", "parallel"),
            name="da_d1",
        )(*args, out)[0]
    return pl.pallas_call(
        functools.partial(_da_kernel, qb=qb, nb=nb, d=d, length=length, prev_dils=()),
        grid=(b, nt),
        in_specs=in_specs,
        out_specs=[out_block, out_block],
        out_shape=[jax.ShapeDtypeStruct((b * t // d, d * SEG), F32)] * 2,
        scratch_shapes=scratch,
        compiler_params=_cparams("parallel", "parallel"),
        name="da_d%d" % d,
    )(*args)


HG_LEVELS = (32, 16, 8, 4, 2, 1)
HG_NSEC = 2 + len(HG_LEVELS)
HG_W = HG_HEADS * HG_DK
HG_PAIR = 2 * HG_DK
LOG2_E = math.log2(math.e)


def _hg_constants(rev):
    c = HG_CHUNK
    t = np.arange(c)[:, None]
    j = np.arange(c)[None, :]
    secs = []
    if not rev:
        secs.append(j <= t)
        secs.append(j > t)
    else:
        secs.append(j >= t)
        secs.append(j < t)
    masks = [np.eye(c, dtype=bool)]
    s = np.arange(c)[None, :]
    for m in HG_LEVELS:
        p0 = (t // (2 * m)) * (2 * m)
        upper = (t & m) != 0
        if not rev:
            mid = p0 + m - 1
            sec = np.where(upper, (j > mid) & (j <= t), (j > t) & (j <= mid))
            pair = ((t // (2 * m)) == (s // (2 * m))) & upper & ((s & m) == 0)
        else:
            mid = p0 + m
            sec = np.where(upper, (j >= mid) & (j < t), (j >= t) & (j < mid))
            pair = ((t // (2 * m)) == (s // (2 * m))) & (~upper) & ((s & m) != 0)
        secs.append(sec)
        masks.append(pair)
    w = np.tile(np.concatenate(secs, axis=0).astype(np.float32), (1, 2))
    masks = np.tile(np.stack(masks).astype(np.float32), (1, 1, 2))
    return jnp.asarray(w, BF16), jnp.asarray(masks)


def _bdiag(h0, h1):
    zero = jnp.zeros_like(h0)
    return jnp.concatenate([jnp.concatenate([h0, zero], axis=1), jnp.concatenate([zero, h1], axis=1)], axis=0)


def _hg_pair_operand(a, p):
    return _bdiag(a[:, (2 * p) * HG_DK:(2 * p + 1) * HG_DK], a[:, (2 * p + 1) * HG_DK:(2 * p + 2) * HG_DK])


def _hg_prepare(rev, z, x, v, lb, w_ref, mask_ref):
    c_sz = HG_CHUNK
    log_lb = jnp.log(lb)
    log_1m = jnp.log1p(-lb)
    log_sig = jnp.minimum(z, 0.0) - jnp.log(1.0 + jnp.exp(-jnp.abs(z)))
    cc = log_1m + log_sig
    logf = jnp.maximum(log_lb, cc) + jnp.log(1.0 + jnp.exp(-jnp.abs(log_lb - cc)))
    kk = (1.0 - lb) * jnp.exp(log_sig - z)
    q = x / (1.0 + jnp.exp(-x))
    log2f = logf * LOG2_E
    hi = log2f.astype(BF16)
    lo = (log2f - hi.astype(F32)).astype(BF16)
    dec = jnp.exp2(_dot(w_ref[...], jnp.concatenate([hi, lo], axis=0)))
    sub = lax.broadcasted_iota(jnp.int32, (1, 8, HG_W), 1)
    ys = []
    for lvl, m in enumerate(HG_LEVELS):
        if m >= 8:
            sel = jnp.concatenate(
                [(q if (((8 * r) & m) != 0) != rev else kk)[8 * r:8 * r + 8] for r in range(c_sz // 8)], axis=0)
        else:
            is_q = ((sub & m) == 0) if rev else ((sub & m) != 0)
            sel = jnp.where(is_q, q.reshape(c_sz // 8, 8, HG_W), kk.reshape(c_sz // 8, 8, HG_W)).reshape(c_sz, HG_W)
        ys.append((sel * dec[(2 + lvl) * c_sz:(3 + lvl) * c_sz]).astype(BF16))
    q16, k16 = q.astype(BF16), kk.astype(BF16)
    kd = (kk * dec[c_sz:2 * c_sz]).astype(BF16)
    intra = []
    for p in range(HG_HEADS // 2):
        cols = slice(p * HG_PAIR, (p + 1) * HG_PAIR)
        a2 = mask_ref[0] * _dot_nt(q16[:, cols], _hg_pair_operand(k16, p))
        for lvl in range(len(HG_LEVELS)):
            a2 = a2 + mask_ref[1 + lvl] * _dot_nt(ys[lvl][:, cols], _hg_pair_operand(ys[lvl], p))
        intra.append(_dot(a2.astype(BF16), _hg_pair_operand(v, p)))
    gain = [_dot_tn(v[:, h * HG_DK:(h + 1) * HG_DK], kd[:, h * HG_DK:(h + 1) * HG_DK]) for h in range(HG_HEADS)]
    edge = 0 if rev else c_sz - 1
    return dict(intra=intra, gain=gain, qd=(q * dec[0:c_sz]).astype(BF16), decay=dec[edge:edge + 1])


def _hg_finish(prep, st_ref):
    outs = []
    for p in range(HG_HEADS // 2):
        cols = slice(p * HG_PAIR, (p + 1) * HG_PAIR)
        st = _bdiag(st_ref[2 * p].astype(BF16), st_ref[2 * p + 1].astype(BF16))
        outs.append(prep["intra"][p] + _dot_nt(prep["qd"][:, cols], st))
        for h in (2 * p, 2 * p + 1):
            st_ref[h] = st_ref[h] * prep["decay"][:, h * HG_DK:(h + 1) * HG_DK] + prep["gain"][h]
    return jnp.concatenate(outs, axis=1)


def _hg_kernel(cqf_ref, cff_ref, cif_ref, cqb_ref, cfb_ref, cib_ref, lb_ref, wf_ref, wb_ref, mf_ref, mb_ref,
               of_ref, ob_ref, st_ref, *, nchunks):
    @pl.when(pl.program_id(1) == 0)
    def _():
        st_ref[...] = jnp.zeros_like(st_ref)

    def prepare(step):
        rf = slice(step * HG_CHUNK, (step + 1) * HG_CHUNK)
        rb = slice((nchunks - 1 - step) * HG_CHUNK, (nchunks - step) * HG_CHUNK)
        fwd = _hg_prepare(False, cff_ref[rf, :].astype(F32), cqf_ref[rf, :].astype(F32), cif_ref[rf, :],
                          lb_ref[0:1, :], wf_ref, mf_ref)
        bwd = _hg_prepare(True, cfb_ref[rb, :].astype(F32), cqb_ref[rb, :].astype(F32), cib_ref[rb, :],
                          lb_ref[1:2, :], wb_ref, mb_ref)
        return rf, fwd, rb, bwd

    def finish(rf, fwd, rb, bwd):
        of_ref[rf, :] = _hg_finish(fwd, st_ref.at[0]).astype(BF16)
        ob_ref[rb, :] = _hg_finish(bwd, st_ref.at[1]).astype(BF16)

    pending = [prepare(step) for step in range(nchunks)]
    for prep in pending:
        finish(*prep)


def _hg_call(proj, lb, consts_f, consts_b, b, t, row_base, outs):
    tb = min(256, t)
    nt = t // tb
    base = row_base // tb

    def pspec(col, rev):
        return pl.BlockSpec((tb, HG_W), lambda bi, i: (base + bi * nt + (nt - 1 - i if rev else i), col))

    def const(a):
        return pl.BlockSpec(a.shape, lambda bi, i: (0,) * a.ndim)

    (wf, mf), (wb, mb) = consts_f, consts_b
    in_specs = [pspec(3, False), pspec(4, False), pspec(6, False), pspec(3, True), pspec(5, True), pspec(6, True),
                const(lb), const(wf), const(wb), const(mf), const(mb)]
    body, alias_specs, out_shape, aliases = _into(functools.partial(_hg_kernel, nchunks=tb // HG_CHUNK),
                                                  len(in_specs), list(outs))
    return pl.pallas_call(
        body,
        grid=(b, nt),
        in_specs=in_specs + alias_specs,
        out_specs=[pspec(0, False), pspec(0, True)],
        out_shape=out_shape,
        input_output_aliases=aliases,
        scratch_shapes=[pltpu.VMEM((2, HG_HEADS, HG_DK, HG_DK), F32)],
        compiler_params=_cparams("parallel", "arbitrary"),
        name="hgrn",
    )(proj, proj, proj, proj, proj, proj, lb, wf, wb, mf, mb, *outs)


def _outproj_kernel(oa_ref, ob_ref, cf_ref, cb_ref, cg_ref, ng_ref, w_ref, g_ref, *rest, split):
    x_refs, (x1_ref, h_ref) = rest[:-2], rest[-2:]
    tot = cf_ref[...].astype(F32) + cb_ref[...].astype(F32)
    gate = cg_ref[...].astype(F32)
    parts = []
    for h in range(HG_HEADS):
        th = tot[:, h * HG_DK:(h + 1) * HG_DK]
        ms = jnp.mean(th * th, axis=-1, keepdims=True)
        parts.append(th * lax.rsqrt(ms + NORM_EPS) * ng_ref[...])
    oc = (jnp.concatenate(parts, axis=1) * (gate / (1.0 + jnp.exp(-gate)))).astype(BF16)
    y = _dot(oa_ref[...], w_ref[0:SEG, :]) + _dot(ob_ref[...], w_ref[SEG:2 * SEG, :]) + _dot(oc, w_ref[2 * SEG:, :])
    x1 = _select_rows(x_refs, split) + y
    x1_ref[...] = x1
    ms = jnp.mean(x1 * x1, axis=-1, keepdims=True)
    h_ref[...] = (x1 * lax.rsqrt(ms + NORM_EPS) * g_ref[...]).astype(BF16)


def _out_proj(oa, ob, ocf, ocb, proj, norm_g, xs, w_out, ln_g, tm):
    n = oa.shape[0]
    row = lambda i: (i, 0)
    const = lambda i: (0, 0)
    x_specs, split = _row_specs(xs, tm)
    return pl.pallas_call(
        functools.partial(_outproj_kernel, split=split),
        grid=(n // tm,),
        in_specs=[pl.BlockSpec((tm, SEG), row), pl.BlockSpec((tm, SEG), row),
                  pl.BlockSpec((tm, HG_W), row), pl.BlockSpec((tm, HG_W), row),
                  pl.BlockSpec((tm, HG_W), lambda i: (i, N_IN // HG_W - 1)),
                  pl.BlockSpec((1, HG_DK), const),
                  pl.BlockSpec((D_MODEL, D_MODEL), const),
                  pl.BlockSpec((1, D_MODEL), const)] + x_specs,
        out_specs=[pl.BlockSpec((tm, D_MODEL), row), pl.BlockSpec((tm, D_MODEL), row)],
        out_shape=[jax.ShapeDtypeStruct((n, D_MODEL), F32), jax.ShapeDtypeStruct((n, D_MODEL), BF16)],
        compiler_params=_cparams("parallel"),
        name="out_proj",
    )(oa, ob, ocf, ocb, proj, norm_g, w_out, ln_g, *xs)


FF_HALO = 16
FF_ROWS = 1024


def _ffn_kernel(x_ref, h_ref, hp_ref, hn_ref, wg_ref, wu_ref, cw_ref, cb_ref, wdn_ref, o_ref, hbuf, *, tm, row_base,
                starts, ends):
    row0 = row_base + pl.program_id(0) * tm
    at_start = functools.reduce(jnp.logical_or, [row0 == s for s in starts])
    at_end = functools.reduce(jnp.logical_or, [row0 + tm == e for e in ends])
    keep_prev = jnp.where(at_start, 0.0, 1.0)
    keep_next = jnp.where(at_end, 0.0, 1.0)
    hbuf[0:tm] = h_ref[...]
    hbuf[tm:tm + FF_HALO] = hp_ref[...]
    hbuf[tm + FF_HALO:tm + 2 * FF_HALO] = hn_ref[...]
    first = lax.broadcasted_iota(jnp.int32, (8, FF_SEG), 0) == 0
    last = lax.broadcasted_iota(jnp.int32, (8, FF_SEG), 0) == 7
    c_tanh = math.sqrt(2.0 / math.pi)
    o_ref[...] = x_ref[...]
    nseg = D_FF // FF_SEG

    def up_proj(j):
        w = jnp.concatenate([wg_ref[:, j * FF_SEG:(j + 1) * FF_SEG], wu_ref[:, j * FF_SEG:(j + 1) * FF_SEG]], axis=1)
        return _dot(hbuf[...], w)

    y_next = up_proj(0)
    for j in range(nseg):
        cols = slice(j * FF_SEG, (j + 1) * FF_SEG)
        y = y_next
        if j + 1 < nseg:
            y_next = up_proj(j + 1)
        g, up = y[0:tm, 0:FF_SEG], y[0:tm, FF_SEG:]
        g_prev = y[tm + FF_HALO - 1:tm + FF_HALO, 0:FF_SEG] * keep_prev
        g_next = y[tm + FF_HALO:tm + FF_HALO + 1, 0:FF_SEG] * keep_next
        cw0, cw1, cw2 = cw_ref[0:1, cols], cw_ref[1:2, cols], cw_ref[2:3, cols]
        gate = pltpu.roll(g, 1, 0) * cw0 + g * cw1 + pltpu.roll(g, tm - 1, 0) * cw2 + cb_ref[:, cols]
        top = gate[0:8] + jnp.where(first, (g_prev - g[tm - 1:tm]) * cw0, 0.0)
        bot = gate[tm - 8:tm] + jnp.where(last, (g_next - g[0:1]) * cw2, 0.0)
        gate = jnp.concatenate([top, gate[8:tm - 8], bot], axis=0)
        inner = gate * (c_tanh + (c_tanh * 0.044715) * (gate * gate))
        act = (gate * up) * (1.0 + jnp.tanh(inner))
        o_ref[...] += _dot(act.astype(BF16), wdn_ref[cols, :])


def _ffn(x1, h2, w_up, conv_w, conv_b, w_down, tm, starts, ends, row_base=0, nrows=None):
    n = x1.shape[0]
    nrows = n if nrows is None else nrows
    hpb = tm // FF_HALO
    nh = n // FF_HALO
    t0 = row_base // tm
    row = lambda i: (t0 + i, 0)
    const = lambda i: (0, 0)
    once = pl.Buffered(1)
    return pl.pallas_call(
        functools.partial(_ffn_kernel, tm=tm, row_base=row_base, starts=starts, ends=ends),
        grid=(nrows // tm,),
        in_specs=[pl.BlockSpec((tm, D_MODEL), row), pl.BlockSpec((tm, D_MODEL), row),
                  pl.BlockSpec((FF_HALO, D_MODEL), lambda i: (jnp.maximum((t0 + i) * hpb - 1, 0), 0)),
                  pl.BlockSpec((FF_HALO, D_MODEL), lambda i: (jnp.minimum((t0 + i + 1) * hpb, nh - 1), 0)),
                  pl.BlockSpec((D_MODEL, D_FF), lambda i: (0, 0), pipeline_mode=once),
                  pl.BlockSpec((D_MODEL, D_FF), lambda i: (0, 1), pipeline_mode=once),
                  pl.BlockSpec((3, D_FF), const), pl.BlockSpec((1, D_FF), const),
                  pl.BlockSpec((D_FF, D_MODEL), const, pipeline_mode=once)],
        out_specs=pl.BlockSpec((tm, D_MODEL), lambda i: (i, 0)),
        out_shape=jax.ShapeDtypeStruct((nrows, D_MODEL), F32),
        scratch_shapes=[pltpu.VMEM((tm + 2 * FF_HALO, D_MODEL), BF16)],
        compiler_params=_cparams("parallel"),
        name="ffn",
    )(x1, h2, h2, h2, w_up, w_up, conv_w, conv_b, w_down)


def _lower_bounds(lb_logits):
    p = jax.nn.softmax(lb_logits.astype(F32), axis=1)
    c = jnp.cumsum(p, axis=1)
    return c - c[:, :1]


def _row_tile(groups, tm=512):
    while any(t % tm for _, t, _ in groups):
        tm //= 2
    return tm


def kernel(x_prompt, x_sample, ln_mix_g, w_in, na_q_g, na_k_g, na_rpb, da_q_g, da_k_g, t5_bias, hg_lb_logits,
           hg_norm_g, w_out, ln_ffn_g, w_up, conv_w, conv_b, w_down):
    depth = w_in.shape[0]
    groups = []
    base = 0
    for xg in (x_prompt, x_sample):
        b, t, _ = xg.shape
        groups.append((b, t, base))
        base += b * t
    starts = tuple(rb + i * t for b, t, rb in groups for i in range(b))
    ends = tuple(rb + (i + 1) * t for b, t, rb in groups for i in range(b))
    tm = _row_tile(groups)
    tm_ffn = _row_tile(groups, FF_ROWS)
    n = base
    xs = [x_prompt.reshape(-1, D_MODEL), x_sample.reshape(-1, D_MODEL)]

    lb = _lower_bounds(hg_lb_logits)
    scale = HEAD_DIM ** -0.5
    tile4 = lambda g: jnp.tile(g.astype(F32), NA_HEADS)
    gm = jnp.asarray(np.kron(np.eye(SEG // HEAD_DIM), np.ones((HEAD_DIM, HEAD_DIM))) / HEAD_DIM, BF16)
    hg_fwd_consts = _hg_constants(False)
    hg_bwd_consts = _hg_constants(True)
    da_bias = {}

    def da_bias_for(d, qb):
        if (d, qb) not in da_bias:
            da_bias[(d, qb)] = _da_bias_table(t5_bias, d, qb)
        return da_bias[(d, qb)]

    for l in range(depth):
        qg = jnp.stack([tile4(na_q_g[l]) * scale, tile4(na_k_g[l]), tile4(da_q_g[l]) * scale, tile4(da_k_g[l])])
        proj, *dil_views = _in_proj(xs, ln_mix_g[l].reshape(1, -1), w_in[l].astype(BF16), qg, gm, tm)
        na_bias = _na_bias_table(na_rpb[l])
        oa = jnp.zeros((n, SEG), BF16)
        ob = jnp.zeros((n, SEG), BF16)
        ocf = jnp.zeros((n, HG_W), BF16)
        ocb = jnp.zeros((n, HG_W), BF16)
        for b, t, rb in groups:
            oa = _na_call(proj, na_bias, b, t, rb, oa)
            prev = []
            for d, view in zip(DA_DILATED, dil_views):
                prev += _da_call(view, 0, da_bias_for(d, min(128, t // d)), b, t, rb, d)
            ob = _da_call(proj, DA_SEGS[0] * SEG // DA_W, da_bias_for(1, min(128, t)), b, t, rb, 1,
                          prev=prev, prev_dils=DA_DILATED, out=ob)
            ocf, ocb = _hg_call(proj, lb[:, l], hg_fwd_consts, hg_bwd_consts, b, t, rb, (ocf, ocb))
        x1, h2 = _out_proj(oa, ob, ocf, ocb, proj, hg_norm_g[l].reshape(1, -1).astype(F32), xs, w_out[l].astype(BF16),
                           ln_ffn_g[l].reshape(1, -1), tm)
        wu, wd = w_up[l].astype(BF16), w_down[l].astype(BF16) * 0.5
        cw, cb = conv_w[l], conv_b[l].reshape(1, -1)
        if l + 1 < depth:
            xs = [_ffn(x1, h2, wu, cw, cb, wd, tm_ffn, starts, ends)]
    return tuple(_ffn(x1, h2, wu, cw, cb, wd, tm_ffn, starts, ends, rb, b * t).reshape(xg.shape)
                 for (b, t, rb), xg in zip(groups, (x_prompt, x_sample)))
```

```python
import functools
import math

import numpy as np
import jax
import jax.numpy as jnp
from jax import lax
from jax.experimental import pallas as pl
from jax.experimental.pallas import tpu as pltpu

F32 = jnp.float32
BF16 = jnp.bfloat16

D_MODEL = 1024
GRID_W = 64
HEAD_DIM = 64
NA_HEADS = 4
NA_WIN_ROWS = 8
NA_WIN_COLS = 16
DA_HEADS = 4
DA_CONFIGS = ((128, 1), (512, 4), (2048, 16))
DA_HALO = 64
N_BUCKETS = 32
MAX_DISTANCE = 1024
HG_HEADS = 4
HG_DK = 128
HG_CHUNK = 64
N_IN = 4096
D_FF = 2816
NORM_EPS = 1e-6
NEG_INF = -1e30
LOG2_E = math.log2(math.e)
SEG = 256
FF_SEG = 256
VMEM_LIMIT = 56 * 1024 * 1024


def _cparams(*sem):
    return pltpu.CompilerParams(dimension_semantics=sem, vmem_limit_bytes=VMEM_LIMIT)


def _into(kernel, n_in, outs):
    def body(*refs):
        return kernel(*refs[:n_in], *refs[n_in + len(outs):])
    specs = [pl.BlockSpec(memory_space=pl.ANY)] * len(outs)
    shapes = [jax.ShapeDtypeStruct(o.shape, o.dtype) for o in outs]
    return body, specs, shapes, {n_in + k: k for k in range(len(outs))}


def _dot(a, b):
    return jnp.dot(a, b, preferred_element_type=F32)


def _dot_nt(a, b):
    return lax.dot_general(a, b, (((1,), (1,)), ((), ())), preferred_element_type=F32)


def _dot_tn(a, b):
    return lax.dot_general(a, b, (((0,), (0,)), ((), ())), preferred_element_type=F32)


DA_SEGS = (3, 4, 5)
DA_W = len(DA_SEGS) * SEG
DA_DILATED = tuple(d for _, d in DA_CONFIGS if d > 1)


def _select_rows(x_refs, split):
    if len(x_refs) == 1:
        return x_refs[0][...]
    return jnp.where(pl.program_id(0) < split, x_refs[0][...], x_refs[1][...])


def _row_specs(xs, tm):
    if len(xs) == 1:
        return [pl.BlockSpec((tm, xs[0].shape[1]), lambda i: (i, 0))], 0
    split = xs[0].shape[0] // tm
    return [pl.BlockSpec((tm, xs[0].shape[1]), lambda i: (jnp.minimum(i, split - 1), 0)),
            pl.BlockSpec((tm, xs[1].shape[1]), lambda i: (jnp.maximum(i - split, 0), 0))], split


def _inproj_kernel(*refs, tm, nx, split):
    x_refs, (g_ref, w_ref, qg_ref, gm_ref, o_ref), rest = refs[:nx], refs[nx:nx + 5], refs[nx + 5:]
    dil_refs, ybuf = rest[:-1], rest[-1]
    x = _select_rows(x_refs, split)
    ms = jnp.mean(x * x, axis=-1, keepdims=True)
    h = (x * lax.rsqrt(ms + NORM_EPS) * g_ref[...]).astype(BF16)
    normed = {0: 0, 1: 1, 3: 2, 4: 3}
    nseg = N_IN // SEG
    y_next = _dot(h, w_ref[:, 0:SEG])
    for seg in range(nseg):
        y = y_next
        if seg + 1 < nseg:
            y_next = _dot(h, w_ref[:, (seg + 1) * SEG:(seg + 2) * SEG])
        if seg in normed:
            ss = _dot((y * y).astype(BF16), gm_ref[...])
            r = normed[seg]
            y = y * lax.rsqrt(ss + NORM_EPS) * qg_ref[r:r + 1, :]
        if seg in DA_SEGS:
            k = DA_SEGS.index(seg)
            for j in range(SEG // 128):
                ybuf[k * (SEG // 128) + j] = y[:, j * 128:(j + 1) * 128]
        o_ref[:, seg * SEG:(seg + 1) * SEG] = y.astype(BF16)
    for d, ref in zip(DA_DILATED, dil_refs):
        for rho in range(d):
            for j in range(DA_W // 128):
                ref[:, rho * DA_W + j * 128:rho * DA_W + (j + 1) * 128] = \
                    ybuf[j, pl.ds(rho, tm // d, stride=d), :].astype(BF16)


def _in_proj(xs, ln_g, w_in, qg, gm, tm):
    n = sum(x.shape[0] for x in xs)
    row = lambda i: (i, 0)
    const = lambda i: (0, 0)
    x_specs, split = _row_specs(xs, tm)
    return pl.pallas_call(
        functools.partial(_inproj_kernel, tm=tm, nx=len(xs), split=split),
        grid=(n // tm,),
        in_specs=x_specs + [
            pl.BlockSpec((1, D_MODEL), const),
            pl.BlockSpec((D_MODEL, N_IN), const),
            pl.BlockSpec((4, SEG), const),
            pl.BlockSpec((SEG, SEG), const),
        ],
        out_specs=[pl.BlockSpec((tm, N_IN), row)] + [pl.BlockSpec((tm // d, d * DA_W), row) for d in DA_DILATED],
        out_shape=[jax.ShapeDtypeStruct((n, N_IN), BF16)]
        + [jax.ShapeDtypeStruct((n // d, d * DA_W), BF16) for d in DA_DILATED],
        scratch_shapes=[pltpu.VMEM((DA_W // 128, tm, 128), F32)],
        compiler_params=_cparams("parallel"),
        name="in_proj",
    )(*xs, ln_g, w_in, qg, gm)


NA_RB = 8
NA_TOK = NA_RB * GRID_W
NA_GR = 4
NA_GQ = NA_GR * GRID_W
NA_KR = NA_GR + NA_WIN_ROWS
NA_GK = NA_KR * GRID_W
NA_NDR = 2 * NA_WIN_ROWS - 1


def _na_bias_table(rpb):
    cq = np.arange(GRID_W)
    ck = np.arange(GRID_W)
    c0 = np.clip(cq - NA_WIN_COLS // 2, 0, GRID_W - NA_WIN_COLS)
    col_ok = (ck[None, :] >= c0[:, None]) & (ck[None, :] < c0[:, None] + NA_WIN_COLS)
    dc = np.clip(ck[None, :] - cq[:, None], -(NA_WIN_COLS - 1), NA_WIN_COLS - 1) + NA_WIN_COLS - 1
    col_onehot = (dc[:, :, None] == np.arange(2 * NA_WIN_COLS - 1)).astype(np.float32)
    jq = np.arange(NA_GR)
    half = NA_WIN_ROWS // 2
    dq = np.stack([jq, half + jq, NA_WIN_ROWS + jq])
    dw0 = np.stack([0 * jq, jq, half + 0 * jq])
    w = np.arange(NA_KR)
    row_ok = (w[None, None, :] >= dw0[:, :, None]) & (w[None, None, :] < dw0[:, :, None] + NA_WIN_ROWS)
    dr = w[None, None, :] - dq[:, :, None] + NA_WIN_ROWS - 1
    row_onehot = ((dr[..., None] == np.arange(NA_NDR)) & row_ok[..., None]).astype(np.float32)
    t = jnp.einsum('hab,qkb->haqk', rpb.astype(F32), col_onehot, precision=lax.Precision.HIGHEST)
    bias = jnp.einsum('cjwa,haqk->chjqwk', row_onehot, t, precision=lax.Precision.HIGHEST)
    ok = row_ok[:, None, :, None, :, None] & col_ok[None, None, None, :, None, :]
    return jnp.where(ok, bias * LOG2_E, NEG_INF).reshape(3, NA_HEADS, NA_GQ, NA_GK)


def _na_kernel(q_ref, kp_ref, kc_ref, kn_ref, vp_ref, vc_ref, vn_ref, bias0_ref, bias1_ref, o_ref, kbuf, vbuf, *,
               rows):
    i = pl.program_id(1)
    kbuf[0:NA_TOK] = kp_ref[...]
    kbuf[NA_TOK:2 * NA_TOK] = kc_ref[...]
    kbuf[2 * NA_TOK:3 * NA_TOK] = kn_ref[...]
    vbuf[0:NA_TOK] = vp_ref[...]
    vbuf[NA_TOK:2 * NA_TOK] = vc_ref[...]
    vbuf[2 * NA_TOK:3 * NA_TOK] = vn_ref[...]
    low = lax.broadcasted_iota(jnp.int32, (NA_GQ, 128), 1) < HEAD_DIM
    bias_refs = (bias0_ref, bias1_ref)

    def scores(g, a):
        r = i * NA_RB + g * NA_GR
        kstart = jnp.clip(r - NA_WIN_ROWS // 2, 0, rows - NA_KR)
        off = pl.multiple_of((kstart - i * NA_RB + NA_RB) * GRID_W, GRID_W)
        cols = slice(a * 128, (a + 1) * 128)
        qa = q_ref[g * NA_GQ:(g + 1) * NA_GQ, cols]
        q2 = jnp.concatenate([jnp.where(low, qa, jnp.zeros_like(qa)), jnp.where(low, jnp.zeros_like(qa), qa)], axis=0)
        bias = jnp.concatenate([bias_refs[g][0, 2 * a], bias_refs[g][0, 2 * a + 1]], axis=0)
        return _dot_nt(q2, kbuf[pl.ds(off, NA_GK), cols]) + bias, vbuf[pl.ds(off, NA_GK), cols]

    items = [(g, a) for g in range(NA_RB // NA_GR) for a in range(2)]
    nxt = scores(*items[0])
    for n, (g, a) in enumerate(items):
        s, va = nxt
        if n + 1 < len(items):
            nxt = scores(*items[n + 1])
        m = jnp.max(s, axis=-1, keepdims=True)
        p = jnp.exp2(s - m)
        l = jnp.sum(p, axis=-1, keepdims=True)
        pv = _dot(p.astype(BF16), va) / l
        o_ref[g * NA_GQ:(g + 1) * NA_GQ, a * 128:(a + 1) * 128] = \
            jnp.where(low, pv[0:NA_GQ], pv[NA_GQ:]).astype(BF16)


def _na_call(proj, bias, b, t, row_base, out):
    rows = t // GRID_W
    nrb = rows // NA_RB
    base = row_base // NA_TOK
    assert rows >= NA_KR and rows % NA_RB == 0

    def spec(col, shift):
        def imap(bi, i):
            return (base + bi * nrb + jnp.clip(i + shift, 0, nrb - 1), col)
        return pl.BlockSpec((NA_TOK, SEG), imap)

    bias_block = (1, NA_HEADS, NA_GQ, NA_GK)
    bias0 = pl.BlockSpec(bias_block, lambda bi, i: (jnp.where(i == 0, 0, 1), 0, 0, 0))
    bias1 = pl.BlockSpec(bias_block, lambda bi, i: (jnp.where(i == nrb - 1, 2, 1), 0, 0, 0))
    in_specs = [spec(0, 0), spec(1, -1), spec(1, 0), spec(1, 1), spec(2, -1), spec(2, 0), spec(2, 1), bias0, bias1]
    body, alias_specs, out_shape, aliases = _into(functools.partial(_na_kernel, rows=rows), len(in_specs), [out])
    return pl.pallas_call(
        body,
        grid=(b, nrb),
        in_specs=in_specs + alias_specs,
        out_specs=[pl.BlockSpec((NA_TOK, SEG), lambda bi, i: (base + bi * nrb + i, 0))],
        out_shape=out_shape,
        input_output_aliases=aliases,
        scratch_shapes=[pltpu.VMEM((3 * NA_TOK, SEG), BF16), pltpu.VMEM((3 * NA_TOK, SEG), BF16)],
        compiler_params=_cparams("parallel", "parallel"),
        name="na",
    )(proj, proj, proj, proj, proj, proj, proj, bias, bias, out)[0]


def _t5_bucket(rel):
    nb = N_BUCKETS // 2
    max_exact = nb // 2
    ret = np.where(rel > 0, nb, 0)
    n = np.abs(rel)
    large = max_exact + (np.log(np.maximum(n, 1) / max_exact)
                         / np.log(MAX_DISTANCE / max_exact) * (nb - max_exact)).astype(np.int32)
    large = np.minimum(large, nb - 1)
    return (ret + np.where(n < max_exact, n, large)).astype(np.int32)


def _da_bias_table(t5_bias, dilation, qb):
    kb = qb + 2 * DA_HALO
    rel = np.arange(kb)[None, :] - np.arange(qb)[:, None] - DA_HALO
    ok = np.abs(rel) <= DA_HALO
    onehot = (_t5_bucket(rel * dilation)[:, :, None] == np.arange(N_BUCKETS)).astype(np.float32)
    bias = jnp.einsum('qkb,bh->hqk', onehot, t5_bias.astype(F32), precision=lax.Precision.HIGHEST)
    j = np.arange(kb)
    edge = np.stack([np.ones(kb, bool), j >= DA_HALO, j < qb + DA_HALO, (j >= DA_HALO) & (j < qb + DA_HALO)])
    return jnp.where(ok[None, None] & edge[:, None, None, :], (bias * LOG2_E)[None], NEG_INF)


def _da_kernel(*refs, qb, nb, d, length, prev_dils):
    final = bool(prev_dils)
    (main_ref, left_ref, right_ref, bias_ref), refs = refs[:4], refs[4:]
    if final:
        npv = 2 * len(prev_dils)
        prev_refs, o_ref, buf, und = refs[:npv], refs[npv], refs[npv + 1], refs[npv + 2:]
    else:
        o_ref, lse_ref, buf = refs
    c = pl.program_id(1)
    tr = qb * nb
    kb = qb + 2 * DA_HALO
    buf[0:DA_HALO] = left_ref[...]
    buf[DA_HALO:DA_HALO + tr] = main_ref[...]
    buf[DA_HALO + tr:tr + 2 * DA_HALO] = right_ref[...]
    if final:
        for k, pd in enumerate(prev_dils):
            for part in range(2):
                src, dst = prev_refs[2 * k + part], und[2 * k + part]
                for rho in range(pd):
                    for j in range(SEG // 128):
                        dst[j, pl.ds(rho, tr // pd, stride=pd), :] = \
                            src[:, rho * SEG + j * 128:rho * SEG + (j + 1) * 128]
    low = lax.broadcasted_iota(jnp.int32, (qb, 128), 1) < HEAD_DIM
    nblk = length // qb

    def scores(blk, rho, a):
        krows = slice(blk * qb, blk * qb + kb)
        gblk = c * nb + blk
        var = jnp.where(gblk == 0, 1, 0) + jnp.where(gblk == nblk - 1, 2, 0)
        base = rho * DA_W + a * 128
        qa = main_ref[blk * qb:(blk + 1) * qb, base:base + 128]
        q2 = jnp.concatenate([jnp.where(low, qa, jnp.zeros_like(qa)), jnp.where(low, jnp.zeros_like(qa), qa)], axis=0)
        bias = jnp.concatenate([bias_ref[var, 2 * a], bias_ref[var, 2 * a + 1]], axis=0)
        s = _dot_nt(q2, buf[krows, base + SEG:base + SEG + 128]) + bias
        return s, buf[krows, base + 2 * SEG:base + 2 * SEG + 128]

    items = [(blk, rho, a) for blk in range(nb) for rho in range(d) for a in range(2)]
    nxt = scores(*items[0])
    for n, (blk, rho, a) in enumerate(items):
        s, va = nxt
        if n + 1 < len(items):
            nxt = scores(*items[n + 1])
        rows = slice(blk * qb, (blk + 1) * qb)
        m = jnp.max(s, axis=-1, keepdims=True)
        p = jnp.exp2(s - m)
        l = jnp.sum(p, axis=-1, keepdims=True)
        pv = _dot(p.astype(BF16), va) / l
        lse2 = jnp.broadcast_to(m + jnp.log2(l), (2 * qb, 128))
        o = jnp.where(low, pv[0:qb], pv[qb:])
        lse = jnp.where(low, lse2[0:qb], lse2[qb:])
        cols = slice(rho * SEG + a * 128, rho * SEG + (a + 1) * 128)
        if final:
            mx = lse
            for k in range(len(prev_dils)):
                mx = jnp.maximum(mx, und[2 * k + 1][a, rows, :])
            wsum = jnp.exp2(lse - mx)
            mix = wsum * o
            for k in range(len(prev_dils)):
                wk = jnp.exp2(und[2 * k + 1][a, rows, :] - mx)
                wsum = wsum + wk
                mix = mix + wk * und[2 * k][a, rows, :]
            o_ref[rows, cols] = (mix / wsum).astype(BF16)
        else:
            o_ref[rows, cols] = o
            lse_ref[rows, cols] = lse


DA_QB = {1: 128, 4: 128, 16: 128}
DA_BLOCKS = {1: 8, 4: 4, 16: 1}


def _da_call(view, colblk, bias, b, t, row_base, dilation, prev=None, prev_dils=(), out=None):
    d = dilation
    length = t // d
    qb = min(DA_QB[d], length)
    nb = min(DA_BLOCKS[d], length // qb)
    tr = qb * nb
    nt = length // tr
    kb = qb + 2 * DA_HALO
    ubase = row_base // d
    hpt = tr // DA_HALO
    nhalo = view.shape[0] // DA_HALO
    final = prev is not None

    def halo(right):
        def imap(bi, c):
            blk = (ubase + bi * length) // DA_HALO + ((c + 1) * hpt if right else c * hpt - 1)
            return (jnp.clip(blk, 0, nhalo - 1), colblk)
        return pl.BlockSpec((DA_HALO, d * DA_W), imap)

    in_specs = [pl.BlockSpec((tr, d * DA_W), lambda bi, c: ((ubase + bi * length) // tr + c, colblk)),
                halo(False), halo(True),
                pl.BlockSpec((4, DA_HEADS, qb, kb), lambda bi, c: (0, 0, 0, 0))]
    args = [view] * 3 + [bias]
    out_block = pl.BlockSpec((tr, d * SEG), lambda bi, c: (bi * nt + c, 0))
    scratch = [pltpu.VMEM((tr + 2 * DA_HALO, d * DA_W), BF16)]
    if final:
        assert d == 1
        for pd in prev_dils:
            in_specs += [pl.BlockSpec((tr // pd, pd * SEG), lambda bi, c: (bi * nt + c, 0))] * 2
            scratch += [pltpu.VMEM((SEG // 128, tr, 128), F32)] * 2
        args += list(prev)
        kern = functools.partial(_da_kernel, qb=qb, nb=nb, d=d, length=length, prev_dils=tuple(prev_dils))
        body, alias_specs, out_shape, aliases = _into(kern, len(in_specs), [out])
        return pl.pallas_call(
            body,
            grid=(b, nt),
            in_specs=in_specs + alias_specs,
            out_specs=[pl.BlockSpec((tr, SEG), lambda bi, c: ((ubase + bi * length) // tr + c, 0))],
            out_shape=out_shape,
            input_output_aliases=aliases,
            scratch_shapes=scratch,
            compiler_params=_cparams("parallel", "parallel"),
            name="da_d1",
        )(*args, out)[0]
    return pl.pallas_call(
        functools.partial(_da_kernel, qb=qb, nb=nb, d=d, length=length, prev_dils=()),
        grid=(b, nt),
        in_specs=in_specs,
        out_specs=[out_block, out_block],
        out_shape=[jax.ShapeDtypeStruct((b * t // d, d * SEG), F32)] * 2,
        scratch_shapes=scratch,
        compiler_params=_cparams("parallel", "parallel"),
        name="da_d%d" % d,
    )(*args)


HG_LEVELS = (32, 16, 8, 4, 2, 1)
HG_NSEC = 2 + len(HG_LEVELS)
HG_W = HG_HEADS * HG_DK
HG_PAIR = 2 * HG_DK


def _hg_constants(rev):
    c = HG_CHUNK
    t = np.arange(c)[:, None]
    j = np.arange(c)[None, :]
    secs = []
    if not rev:
        secs.append(j <= t)
        secs.append(j > t)
    else:
        secs.append(j >= t)
        secs.append(j < t)
    masks = [np.eye(c, dtype=bool)]
    s = np.arange(c)[None, :]
    for m in HG_LEVELS:
        p0 = (t // (2 * m)) * (2 * m)
        upper = (t & m) != 0
        if not rev:
            mid = p0 + m - 1
            sec = np.where(upper, (j > mid) & (j <= t), (j > t) & (j <= mid))
            pair = ((t // (2 * m)) == (s // (2 * m))) & upper & ((s & m) == 0)
        else:
            mid = p0 + m
            sec = np.where(upper, (j >= mid) & (j < t), (j >= t) & (j < mid))
            pair = ((t // (2 * m)) == (s // (2 * m))) & (~upper) & ((s & m) != 0)
        secs.append(sec)
        masks.append(pair)
    w = np.tile(np.concatenate(secs, axis=0).astype(np.float32), (1, 2))
    masks = np.tile(np.stack(masks).astype(np.float32), (1, 1, 2))
    return jnp.asarray(w, BF16), jnp.asarray(masks)


def _bdiag(h0, h1):
    zero = jnp.zeros_like(h0)
    return jnp.concatenate([jnp.concatenate([h0, zero], axis=1), jnp.concatenate([zero, h1], axis=1)], axis=0)


def _hg_pair_operand(a, p):
    return _bdiag(a[:, (2 * p) * HG_DK:(2 * p + 1) * HG_DK], a[:, (2 * p + 1) * HG_DK:(2 * p + 2) * HG_DK])


def _hg_prepare(rev, z, x, v, lb, w_ref, mask_ref):
    c_sz = HG_CHUNK
    log_lb = jnp.log(lb)
    log_1m = jnp.log1p(-lb)
    log_sig = jnp.minimum(z, 0.0) - jnp.log(1.0 + jnp.exp(-jnp.abs(z)))
    cc = log_1m + log_sig
    logf = jnp.maximum(log_lb, cc) + jnp.log(1.0 + jnp.exp(-jnp.abs(log_lb - cc)))
    kk = (1.0 - lb) * jnp.exp(log_sig - z)
    q = x / (1.0 + jnp.exp(-x))
    log2f = logf * LOG2_E
    hi = log2f.astype(BF16)
    lo = (log2f - hi.astype(F32)).astype(BF16)
    dec = jnp.exp2(_dot(w_ref[...], jnp.concatenate([hi, lo], axis=0)))
    sub = lax.broadcasted_iota(jnp.int32, (1, 8, HG_W), 1)
    ys = []
    for lvl, m in enumerate(HG_LEVELS):
        if m >= 8:
            sel = jnp.concatenate(
                [(q if (((8 * r) & m) != 0) != rev else kk)[8 * r:8 * r + 8] for r in range(c_sz // 8)], axis=0)
        else:
            is_q = ((sub & m) == 0) if rev else ((sub & m) != 0)
            sel = jnp.where(is_q, q.reshape(c_sz // 8, 8, HG_W), kk.reshape(c_sz // 8, 8, HG_W)).reshape(c_sz, HG_W)
        ys.append((sel * dec[(2 + lvl) * c_sz:(3 + lvl) * c_sz]).astype(BF16))
    q16, k16 = q.astype(BF16), kk.astype(BF16)
    kd = (kk * dec[c_sz:2 * c_sz]).astype(BF16)
    intra = []
    for p in range(HG_HEADS // 2):
        cols = slice(p * HG_PAIR, (p + 1) * HG_PAIR)
        a2 = mask_ref[0] * _dot_nt(q16[:, cols], _hg_pair_operand(k16, p))
        for lvl in range(len(HG_LEVELS)):
            a2 = a2 + mask_ref[1 + lvl] * _dot_nt(ys[lvl][:, cols], _hg_pair_operand(ys[lvl], p))
        intra.append(_dot(a2.astype(BF16), _hg_pair_operand(v, p)))
    gain = [_dot_tn(v[:, h * HG_DK:(h + 1) * HG_DK], kd[:, h * HG_DK:(h + 1) * HG_DK]) for h in range(HG_HEADS)]
    edge = 0 if rev else c_sz - 1
    return dict(intra=intra, gain=gain, qd=(q * dec[0:c_sz]).astype(BF16), decay=dec[edge:edge + 1])


def _hg_finish(prep, st_ref):
    outs = []
    for p in range(HG_HEADS // 2):
        cols = slice(p * HG_PAIR, (p + 1) * HG_PAIR)
        st = _bdiag(st_ref[2 * p].astype(BF16), st_ref[2 * p + 1].astype(BF16))
        outs.append(prep["intra"][p] + _dot_nt(prep["qd"][:, cols], st))
        for h in (2 * p, 2 * p + 1):
            st_ref[h] = st_ref[h] * prep["decay"][:, h * HG_DK:(h + 1) * HG_DK] + prep["gain"][h]
    return jnp.concatenate(outs, axis=1)


def _hg_kernel(cqf_ref, cff_ref, cif_ref, cqb_ref, cfb_ref, cib_ref, lb_ref, wf_ref, wb_ref, mf_ref, mb_ref,
               of_ref, ob_ref, st_ref, *, nchunks):
    @pl.when(pl.program_id(1) == 0)
    def _():
        st_ref[...] = jnp.zeros_like(st_ref)

    def prepare(step):
        rf = slice(step * HG_CHUNK, (step + 1) * HG_CHUNK)
        rb = slice((nchunks - 1 - step) * HG_CHUNK, (nchunks - step) * HG_CHUNK)
        fwd = _hg_prepare(False, cff_ref[rf, :].astype(F32), cqf_ref[rf, :].astype(F32), cif_ref[rf, :],
                          lb_ref[0:1, :], wf_ref, mf_ref)
        bwd = _hg_prepare(True, cfb_ref[rb, :].astype(F32), cqb_ref[rb, :].astype(F32), cib_ref[rb, :],
                          lb_ref[1:2, :], wb_ref, mb_ref)
        return rf, fwd, rb, bwd

    def finish(rf, fwd, rb, bwd):
        of_ref[rf, :] = _hg_finish(fwd, st_ref.at[0]).astype(BF16)
        ob_ref[rb, :] = _hg_finish(bwd, st_ref.at[1]).astype(BF16)

    pending = [prepare(step) for step in range(nchunks)]
    for prep in pending:
        finish(*prep)


def _hg_call(proj, lb, consts_f, consts_b, b, t, row_base, outs):
    tb = min(256, t)
    nt = t // tb
    base = row_base // tb

    def pspec(col, rev):
        return pl.BlockSpec((tb, HG_W), lambda bi, i: (base + bi * nt + (nt - 1 - i if rev else i), col))

    def const(a):
        return pl.BlockSpec(a.shape, lambda bi, i: (0,) * a.ndim)

    (wf, mf), (wb, mb) = consts_f, consts_b
    in_specs = [pspec(3, False), pspec(4, False), pspec(6, False), pspec(3, True), pspec(5, True), pspec(6, True),
                const(lb), const(wf), const(wb), const(mf), const(mb)]
    body, alias_specs, out_shape, aliases = _into(functools.partial(_hg_kernel, nchunks=tb // HG_CHUNK),
                                                  len(in_specs), list(outs))
    return pl.pallas_call(
        body,
        grid=(b, nt),
        in_specs=in_specs + alias_specs,
        out_specs=[pspec(0, False), pspec(0, True)],
        out_shape=out_shape,
        input_output_aliases=aliases,
        scratch_shapes=[pltpu.VMEM((2, HG_HEADS, HG_DK, HG_DK), F32)],
        compiler_params=_cparams("parallel", "arbitrary"),
        name="hgrn",
    )(proj, proj, proj, proj, proj, proj, lb, wf, wb, mf, mb, *outs)


def _outproj_kernel(oa_ref, ob_ref, cf_ref, cb_ref, cg_ref, ng_ref, w_ref, g_ref, *rest, split):
    x_refs, (x1_ref, h_ref) = rest[:-2], rest[-2:]
    tot = cf_ref[...].astype(F32) + cb_ref[...].astype(F32)
    gate = cg_ref[...].astype(F32)
    parts = []
    for h in range(HG_HEADS):
        th = tot[:, h * HG_DK:(h + 1) * HG_DK]
        ms = jnp.mean(th * th, axis=-1, keepdims=True)
        parts.append(th * lax.rsqrt(ms + NORM_EPS) * ng_ref[...])
    oc = (jnp.concatenate(parts, axis=1) * (gate / (1.0 + jnp.exp(-gate)))).astype(BF16)
    y = _dot(oa_ref[...], w_ref[0:SEG, :]) + _dot(ob_ref[...], w_ref[SEG:2 * SEG, :]) + _dot(oc, w_ref[2 * SEG:, :])
    x1 = _select_rows(x_refs, split) + y
    x1_ref[...] = x1
    ms = jnp.mean(x1 * x1, axis=-1, keepdims=True)
    h_ref[...] = (x1 * lax.rsqrt(ms + NORM_EPS) * g_ref[...]).astype(BF16)


def _out_proj(oa, ob, ocf, ocb, proj, norm_g, xs, w_out, ln_g, tm):
    n = oa.shape[0]
    row = lambda i: (i, 0)
    const = lambda i: (0, 0)
    x_specs, split = _row_specs(xs, tm)
    return pl.pallas_call(
        functools.partial(_outproj_kernel, split=split),
        grid=(n // tm,),
        in_specs=[pl.BlockSpec((tm, SEG), row), pl.BlockSpec((tm, SEG), row),
                  pl.BlockSpec((tm, HG_W), row), pl.BlockSpec((tm, HG_W), row),
                  pl.BlockSpec((tm, HG_W), lambda i: (i, N_IN // HG_W - 1)),
                  pl.BlockSpec((1, HG_DK), const),
                  pl.BlockSpec((D_MODEL, D_MODEL), const),
                  pl.BlockSpec((1, D_MODEL), const)] + x_specs,
        out_specs=[pl.BlockSpec((tm, D_MODEL), row), pl.BlockSpec((tm, D_MODEL), row)],
        out_shape=[jax.ShapeDtypeStruct((n, D_MODEL), F32), jax.ShapeDtypeStruct((n, D_MODEL), BF16)],
        compiler_params=_cparams("parallel"),
        name="out_proj",
    )(oa, ob, ocf, ocb, proj, norm_g, w_out, ln_g, *xs)


FF_HALO = 16
FF_ROWS = 1024


def _ffn_kernel(x_ref, h_ref, hp_ref, hn_ref, wg_ref, wu_ref, cw_ref, cb_ref, wdn_ref, o_ref, hbuf, *, tm, row_base,
                starts, ends):
    row0 = row_base + pl.program_id(0) * tm
    at_start = functools.reduce(jnp.logical_or, [row0 == s for s in starts])
    at_end = functools.reduce(jnp.logical_or, [row0 + tm == e for e in ends])
    keep_prev = jnp.where(at_start, 0.0, 1.0)
    keep_next = jnp.where(at_end, 0.0, 1.0)
    hbuf[0:tm] = h_ref[...]
    hbuf[tm:tm + FF_HALO] = hp_ref[...]
    hbuf[tm + FF_HALO:tm + 2 * FF_HALO] = hn_ref[...]
    first = lax.broadcasted_iota(jnp.int32, (8, FF_SEG), 0) == 0
    last = lax.broadcasted_iota(jnp.int32, (8, FF_SEG), 0) == 7
    c_tanh = math.sqrt(2.0 / math.pi)
    o_ref[...] = x_ref[...]
    nseg = D_FF // FF_SEG

    def up_proj(j):
        w = jnp.concatenate([wg_ref[:, j * FF_SEG:(j + 1) * FF_SEG], wu_ref[:, j * FF_SEG:(j + 1) * FF_SEG]], axis=1)
        return _dot(hbuf[...], w)

    y_next = up_proj(0)
    for j in range(nseg):
        cols = slice(j * FF_SEG, (j + 1) * FF_SEG)
        y = y_next
        if j + 1 < nseg:
            y_next = up_proj(j + 1)
        g, up = y[0:tm, 0:FF_SEG], y[0:tm, FF_SEG:]
        g_prev = y[tm + FF_HALO - 1:tm + FF_HALO, 0:FF_SEG] * keep_prev
        g_next = y[tm + FF_HALO:tm + FF_HALO + 1, 0:FF_SEG] * keep_next
        cw0, cw1, cw2 = cw_ref[0:1, cols], cw_ref[1:2, cols], cw_ref[2:3, cols]
        gate = pltpu.roll(g, 1, 0) * cw0 + g * cw1 + pltpu.roll(g, tm - 1, 0) * cw2 + cb_ref[:, cols]
        top = gate[0:8] + jnp.where(first, (g_prev - g[tm - 1:tm]) * cw0, 0.0)
        bot = gate[tm - 8:tm] + jnp.where(last, (g_next - g[0:1]) * cw2, 0.0)
        gate = jnp.concatenate([top, gate[8:tm - 8], bot], axis=0)
        inner = gate * (c_tanh + (c_tanh * 0.044715) * (gate * gate))
        act = (gate * up) * (1.0 + jnp.tanh(inner))
        o_ref[...] += _dot(act.astype(BF16), wdn_ref[cols, :])


def _ffn(x1, h2, w_up, conv_w, conv_b, w_down, tm, starts, ends, row_base=0, nrows=None):
    n = x1.shape[0]
    nrows = n if nrows is None else nrows
    hpb = tm // FF_HALO
    nh = n // FF_HALO
    t0 = row_base // tm
    row = lambda i: (t0 + i, 0)
    const = lambda i: (0, 0)
    once = pl.Buffered(1)
    return pl.pallas_call(
        functools.partial(_ffn_kernel, tm=tm, row_base=row_base, starts=starts, ends=ends),
        grid=(nrows // tm,),
        in_specs=[pl.BlockSpec((tm, D_MODEL), row), pl.BlockSpec((tm, D_MODEL), row),
                  pl.BlockSpec((FF_HALO, D_MODEL), lambda i: (jnp.maximum((t0 + i) * hpb - 1, 0), 0)),
                  pl.BlockSpec((FF_HALO, D_MODEL), lambda i: (jnp.minimum((t0 + i + 1) * hpb, nh - 1), 0)),
                  pl.BlockSpec((D_MODEL, D_FF), lambda i: (0, 0), pipeline_mode=once),
                  pl.BlockSpec((D_MODEL, D_FF), lambda i: (0, 1), pipeline_mode=once),
                  pl.BlockSpec((3, D_FF), const), pl.BlockSpec((1, D_FF), const),
                  pl.BlockSpec((D_FF, D_MODEL), const, pipeline_mode=once)],
        out_specs=pl.BlockSpec((tm, D_MODEL), lambda i: (i, 0)),
        out_shape=jax.ShapeDtypeStruct((nrows, D_MODEL), F32),
        scratch_shapes=[pltpu.VMEM((tm + 2 * FF_HALO, D_MODEL), BF16)],
        compiler_params=_cparams("parallel"),
        name="ffn",
    )(x1, h2, h2, h2, w_up, w_up, conv_w, conv_b, w_down)


def _lower_bounds(lb_logits):
    p = jax.nn.softmax(lb_logits.astype(F32), axis=1)
    c = jnp.cumsum(p, axis=1)
    return c - c[:, :1]


def _row_tile(groups, tm=512):
    while any(t % tm for _, t, _ in groups):
        tm //= 2
    return tm


def kernel(x_prompt, x_sample, ln_mix_g, w_in, na_q_g, na_k_g, na_rpb, da_q_g, da_k_g, t5_bias, hg_lb_logits,
           hg_norm_g, w_out, ln_ffn_g, w_up, conv_w, conv_b, w_down):
    depth = w_in.shape[0]
    groups = []
    base = 0
    for xg in (x_prompt, x_sample):
        b, t, _ = xg.shape
        groups.append((b, t, base))
        base += b * t
    starts = tuple(rb + i * t for b, t, rb in groups for i in range(b))
    ends = tuple(rb + (i + 1) * t for b, t, rb in groups for i in range(b))
    tm = _row_tile(groups)
    tm_ffn = _row_tile(groups, FF_ROWS)
    n = base
    xs = [x_prompt.reshape(-1, D_MODEL), x_sample.reshape(-1, D_MODEL)]

    lb = _lower_bounds(hg_lb_logits)
    scale = HEAD_DIM ** -0.5
    tile4 = lambda g: jnp.tile(g.astype(F32), NA_HEADS)
    gm = jnp.asarray(np.kron(np.eye(SEG // HEAD_DIM), np.ones((HEAD_DIM, HEAD_DIM))) / HEAD_DIM, BF16)
    hg_fwd_consts = _hg_constants(False)
    hg_bwd_consts = _hg_constants(True)
    da_bias = {}

    def da_bias_for(d, qb):
        if (d, qb) not in da_bias:
            da_bias[(d, qb)] = _da_bias_table(t5_bias, d, qb)
        return da_bias[(d, qb)]

    for l in range(depth):
        qg = jnp.stack([tile4(na_q_g[l]) * (scale * LOG2_E), tile4(na_k_g[l]),
                        tile4(da_q_g[l]) * (scale * LOG2_E), tile4(da_k_g[l])])
        proj, *dil_views = _in_proj(xs, ln_mix_g[l].reshape(1, -1), w_in[l].astype(BF16), qg, gm, tm)
        na_bias = _na_bias_table(na_rpb[l])
        oa = jnp.zeros((n, SEG), BF16)
        ob = jnp.zeros((n, SEG), BF16)
        ocf = jnp.zeros((n, HG_W), BF16)
        ocb = jnp.zeros((n, HG_W), BF16)
        for b, t, rb in groups:
            oa = _na_call(proj, na_bias, b, t, rb, oa)
            prev = []
            for d, view in zip(DA_DILATED, dil_views):
                prev += _da_call(view, 0, da_bias_for(d, min(DA_QB[d], t // d)), b, t, rb, d)
            ob = _da_call(proj, DA_SEGS[0] * SEG // DA_W, da_bias_for(1, min(DA_QB[1], t)), b, t, rb, 1,
                          prev=prev, prev_dils=DA_DILATED, out=ob)
            ocf, ocb = _hg_call(proj, lb[:, l], hg_fwd_consts, hg_bwd_consts, b, t, rb, (ocf, ocb))
        x1, h2 = _out_proj(oa, ob, ocf, ocb, proj, hg_norm_g[l].reshape(1, -1).astype(F32), xs, w_out[l].astype(BF16),
                           ln_ffn_g[l].reshape(1, -1), tm)
        wu, wd = w_up[l].astype(BF16), w_down[l].astype(BF16) * 0.5
        cw, cb = conv_w[l], conv_b[l].reshape(1, -1)
        if l + 1 < depth:
            xs = [_ffn(x1, h2, wu, cw, cb, wd, tm_ffn, starts, ends)]
    return tuple(_ffn(x1, h2, wu, cw, cb, wd, tm_ffn, starts, ends, rb, b * t).reshape(xg.shape)
                 for (b, t, rb), xg in zip(groups, (x_prompt, x_sample)))
```

```python
import functools
import math

import numpy as np
import jax
import jax.numpy as jnp
from jax import lax
from jax.experimental import pallas as pl
from jax.experimental.pallas import tpu as pltpu

F32 = jnp.float32
BF16 = jnp.bfloat16

D_MODEL = 1024
GRID_W = 64
HEAD_DIM = 64
NA_HEADS = 4
NA_WIN_ROWS = 8
NA_WIN_COLS = 16
DA_HEADS = 4
DA_CONFIGS = ((128, 1), (512, 4), (2048, 16))
DA_HALO = 64
N_BUCKETS = 32
MAX_DISTANCE = 1024
HG_HEADS = 4
HG_DK = 128
HG_CHUNK = 64
N_IN = 4096
D_FF = 2816
NORM_EPS = 1e-6
NEG_INF = -1e30
LOG2_E = math.log2(math.e)
SEG = 256
FF_SEG = 256
VMEM_LIMIT = 56 * 1024 * 1024


def _cparams(*sem):
    return pltpu.CompilerParams(dimension_semantics=sem, vmem_limit_bytes=VMEM_LIMIT)


def _into(kernel, n_in, outs):
    def body(*refs):
        return kernel(*refs[:n_in], *refs[n_in + len(outs):])
    specs = [pl.BlockSpec(memory_space=pl.ANY)] * len(outs)
    shapes = [jax.ShapeDtypeStruct(o.shape, o.dtype) for o in outs]
    return body, specs, shapes, {n_in + k: k for k in range(len(outs))}


def _dot(a, b):
    return jnp.dot(a, b, preferred_element_type=F32)


def _dot_nt(a, b):
    return lax.dot_general(a, b, (((1,), (1,)), ((), ())), preferred_element_type=F32)


def _dot_tn(a, b):
    return lax.dot_general(a, b, (((0,), (0,)), ((), ())), preferred_element_type=F32)


DA_SEGS = (3, 4, 5)
DA_W = len(DA_SEGS) * SEG
DA_DILATED = tuple(d for _, d in DA_CONFIGS if d > 1)


def _select_rows(x_refs, split):
    if len(x_refs) == 1:
        return x_refs[0][...]
    return jnp.where(pl.program_id(0) < split, x_refs[0][...], x_refs[1][...])


def _row_specs(xs, tm):
    if len(xs) == 1:
        return [pl.BlockSpec((tm, xs[0].shape[1]), lambda i: (i, 0))], 0
    split = xs[0].shape[0] // tm
    return [pl.BlockSpec((tm, xs[0].shape[1]), lambda i: (jnp.minimum(i, split - 1), 0)),
            pl.BlockSpec((tm, xs[1].shape[1]), lambda i: (jnp.maximum(i - split, 0), 0))], split


def _inproj_kernel(*refs, tm, nx, split):
    x_refs, (g_ref, w_ref, qg_ref, gm_ref, o_ref), rest = refs[:nx], refs[nx:nx + 5], refs[nx + 5:]
    dil_refs, ybuf, hbuf = rest[:-2], rest[-2], rest[-1]
    x = _select_rows(x_refs, split)
    ms = jnp.mean(x * x, axis=-1, keepdims=True)
    hbuf[...] = (x * lax.rsqrt(ms + NORM_EPS) * g_ref[...]).astype(BF16)
    normed = {0: 0, 1: 1, 3: 2, 4: 3}
    nseg = N_IN // SEG
    y_next = _dot(hbuf[...], w_ref[:, 0:SEG])
    for seg in range(nseg):
        y = y_next
        if seg + 1 < nseg:
            y_next = _dot(hbuf[...], w_ref[:, (seg + 1) * SEG:(seg + 2) * SEG])
        if seg in normed:
            ss = _dot((y * y).astype(BF16), gm_ref[...])
            r = normed[seg]
            y = y * lax.rsqrt(ss + NORM_EPS) * qg_ref[r:r + 1, :]
        if seg in DA_SEGS:
            k = DA_SEGS.index(seg)
            for j in range(SEG // 128):
                ybuf[k * (SEG // 128) + j] = y[:, j * 128:(j + 1) * 128]
        o_ref[:, seg * SEG:(seg + 1) * SEG] = y.astype(BF16)
    for d, ref in zip(DA_DILATED, dil_refs):
        for rho in range(d):
            for j in range(DA_W // 128):
                ref[:, rho * DA_W + j * 128:rho * DA_W + (j + 1) * 128] = \
                    ybuf[j, pl.ds(rho, tm // d, stride=d), :].astype(BF16)


def _in_proj(xs, ln_g, w_in, qg, gm, tm):
    n = sum(x.shape[0] for x in xs)
    row = lambda i: (i, 0)
    const = lambda i: (0, 0)
    x_specs, split = _row_specs(xs, tm)
    return pl.pallas_call(
        functools.partial(_inproj_kernel, tm=tm, nx=len(xs), split=split),
        grid=(n // tm,),
        in_specs=x_specs + [
            pl.BlockSpec((1, D_MODEL), const),
            pl.BlockSpec((D_MODEL, N_IN), const, pipeline_mode=pl.Buffered(1)),
            pl.BlockSpec((4, SEG), const),
            pl.BlockSpec((SEG, SEG), const),
        ],
        out_specs=[pl.BlockSpec((tm, N_IN), row)] + [pl.BlockSpec((tm // d, d * DA_W), row) for d in DA_DILATED],
        out_shape=[jax.ShapeDtypeStruct((n, N_IN), BF16)]
        + [jax.ShapeDtypeStruct((n // d, d * DA_W), BF16) for d in DA_DILATED],
        scratch_shapes=[pltpu.VMEM((DA_W // 128, tm, 128), F32),
                        pltpu.VMEM((tm, D_MODEL), BF16)],
        compiler_params=_cparams("parallel"),
        name="in_proj",
    )(*xs, ln_g, w_in, qg, gm)


NA_RB = 8
NA_TOK = NA_RB * GRID_W
NA_GR = 4
NA_GQ = NA_GR * GRID_W
NA_KR = NA_GR + NA_WIN_ROWS
NA_GK = NA_KR * GRID_W
NA_NDR = 2 * NA_WIN_ROWS - 1


def _na_bias_table(rpb):
    cq = np.arange(GRID_W)
    ck = np.arange(GRID_W)
    c0 = np.clip(cq - NA_WIN_COLS // 2, 0, GRID_W - NA_WIN_COLS)
    col_ok = (ck[None, :] >= c0[:, None]) & (ck[None, :] < c0[:, None] + NA_WIN_COLS)
    dc = np.clip(ck[None, :] - cq[:, None], -(NA_WIN_COLS - 1), NA_WIN_COLS - 1) + NA_WIN_COLS - 1
    col_onehot = (dc[:, :, None] == np.arange(2 * NA_WIN_COLS - 1)).astype(np.float32)
    jq = np.arange(NA_GR)
    half = NA_WIN_ROWS // 2
    dq = np.stack([jq, half + jq, NA_WIN_ROWS + jq])
    dw0 = np.stack([0 * jq, jq, half + 0 * jq])
    w = np.arange(NA_KR)
    row_ok = (w[None, None, :] >= dw0[:, :, None]) & (w[None, None, :] < dw0[:, :, None] + NA_WIN_ROWS)
    dr = w[None, None, :] - dq[:, :, None] + NA_WIN_ROWS - 1
    row_onehot = ((dr[..., None] == np.arange(NA_NDR)) & row_ok[..., None]).astype(np.float32)
    t = jnp.einsum('hab,qkb->haqk', rpb.astype(F32), col_onehot, precision=lax.Precision.HIGHEST)
    bias = jnp.einsum('cjwa,haqk->chjqwk', row_onehot, t, precision=lax.Precision.HIGHEST)
    ok = row_ok[:, None, :, None, :, None] & col_ok[None, None, None, :, None, :]
    return jnp.where(ok, bias * LOG2_E, NEG_INF).reshape(3, NA_HEADS, NA_GQ, NA_GK)


def _na_kernel(q_ref, kp_ref, kc_ref, kn_ref, vp_ref, vc_ref, vn_ref, bias0_ref, bias1_ref, o_ref, kbuf, vbuf, *,
               rows):
    i = pl.program_id(1)
    kbuf[0:NA_TOK] = kp_ref[...]
    kbuf[NA_TOK:2 * NA_TOK] = kc_ref[...]
    kbuf[2 * NA_TOK:3 * NA_TOK] = kn_ref[...]
    vbuf[0:NA_TOK] = vp_ref[...]
    vbuf[NA_TOK:2 * NA_TOK] = vc_ref[...]
    vbuf[2 * NA_TOK:3 * NA_TOK] = vn_ref[...]
    low = lax.broadcasted_iota(jnp.int32, (NA_GQ, 128), 1) < HEAD_DIM
    bias_refs = (bias0_ref, bias1_ref)

    def scores(g, a):
        r = i * NA_RB + g * NA_GR
        kstart = jnp.clip(r - NA_WIN_ROWS // 2, 0, rows - NA_KR)
        off = pl.multiple_of((kstart - i * NA_RB + NA_RB) * GRID_W, GRID_W)
        cols = slice(a * 128, (a + 1) * 128)
        qa = q_ref[g * NA_GQ:(g + 1) * NA_GQ, cols]
        q2 = jnp.concatenate([jnp.where(low, qa, jnp.zeros_like(qa)), jnp.where(low, jnp.zeros_like(qa), qa)], axis=0)
        bias = jnp.concatenate([bias_refs[g][0, 2 * a], bias_refs[g][0, 2 * a + 1]], axis=0)
        return _dot_nt(q2, kbuf[pl.ds(off, NA_GK), cols]) + bias, vbuf[pl.ds(off, NA_GK), cols]

    items = [(g, a) for g in range(NA_RB // NA_GR) for a in range(2)]
    nxt = scores(*items[0])
    for n, (g, a) in enumerate(items):
        s, va = nxt
        if n + 1 < len(items):
            nxt = scores(*items[n + 1])
        m = jnp.max(s, axis=-1, keepdims=True)
        p = jnp.exp2(s - m)
        l = jnp.sum(p, axis=-1, keepdims=True)
        pv = _dot(p.astype(BF16), va) / l
        o_ref[g * NA_GQ:(g + 1) * NA_GQ, a * 128:(a + 1) * 128] = \
            jnp.where(low, pv[0:NA_GQ], pv[NA_GQ:]).astype(BF16)


def _na_call(proj, bias, b, t, row_base, out):
    rows = t // GRID_W
    nrb = rows // NA_RB
    base = row_base // NA_TOK
    assert rows >= NA_KR and rows % NA_RB == 0

    def spec(col, shift):
        def imap(bi, i):
            return (base + bi * nrb + jnp.clip(i + shift, 0, nrb - 1), col)
        return pl.BlockSpec((NA_TOK, SEG), imap)

    bias_block = (1, NA_HEADS, NA_GQ, NA_GK)
    bias0 = pl.BlockSpec(bias_block, lambda bi, i: (jnp.where(i == 0, 0, 1), 0, 0, 0))
    bias1 = pl.BlockSpec(bias_block, lambda bi, i: (jnp.where(i == nrb - 1, 2, 1), 0, 0, 0))
    in_specs = [spec(0, 0), spec(1, -1), spec(1, 0), spec(1, 1), spec(2, -1), spec(2, 0), spec(2, 1), bias0, bias1]
    body, alias_specs, out_shape, aliases = _into(functools.partial(_na_kernel, rows=rows), len(in_specs), [out])
    return pl.pallas_call(
        body,
        grid=(b, nrb),
        in_specs=in_specs + alias_specs,
        out_specs=[pl.BlockSpec((NA_TOK, SEG), lambda bi, i: (base + bi * nrb + i, 0))],
        out_shape=out_shape,
        input_output_aliases=aliases,
        scratch_shapes=[pltpu.VMEM((3 * NA_TOK, SEG), BF16), pltpu.VMEM((3 * NA_TOK, SEG), BF16)],
        compiler_params=_cparams("parallel", "parallel"),
        name="na",
    )(proj, proj, proj, proj, proj, proj, proj, bias, bias, out)[0]


def _t5_bucket(rel):
    nb = N_BUCKETS // 2
    max_exact = nb // 2
    ret = np.where(rel > 0, nb, 0)
    n = np.abs(rel)
    large = max_exact + (np.log(np.maximum(n, 1) / max_exact)
                         / np.log(MAX_DISTANCE / max_exact) * (nb - max_exact)).astype(np.int32)
    large = np.minimum(large, nb - 1)
    return (ret + np.where(n < max_exact, n, large)).astype(np.int32)


def _da_bias_table(t5_bias, dilation, qb):
    kb = qb + 2 * DA_HALO
    rel = np.arange(kb)[None, :] - np.arange(qb)[:, None] - DA_HALO
    ok = np.abs(rel) <= DA_HALO
    onehot = (_t5_bucket(rel * dilation)[:, :, None] == np.arange(N_BUCKETS)).astype(np.float32)
    bias = jnp.einsum('qkb,bh->hqk', onehot, t5_bias.astype(F32), precision=lax.Precision.HIGHEST)
    j = np.arange(kb)
    edge = np.stack([np.ones(kb, bool), j >= DA_HALO, j < qb + DA_HALO, (j >= DA_HALO) & (j < qb + DA_HALO)])
    return jnp.where(ok[None, None] & edge[:, None, None, :], (bias * LOG2_E)[None], NEG_INF)


def _da_kernel(*refs, qb, nb, d, length, prev_dils):
    final = bool(prev_dils)
    (main_ref, left_ref, right_ref, bias_ref), refs = refs[:4], refs[4:]
    if final:
        npv = 2 * len(prev_dils)
        prev_refs, o_ref, buf, und = refs[:npv], refs[npv], refs[npv + 1], refs[npv + 2:]
    else:
        o_ref, lse_ref, buf = refs
    c = pl.program_id(1)
    tr = qb * nb
    kb = qb + 2 * DA_HALO
    buf[0:DA_HALO] = left_ref[...]
    buf[DA_HALO:DA_HALO + tr] = main_ref[...]
    buf[DA_HALO + tr:tr + 2 * DA_HALO] = right_ref[...]
    if final:
        for k, pd in enumerate(prev_dils):
            for part in range(2):
                src, dst = prev_refs[2 * k + part], und[2 * k + part]
                for rho in range(pd):
                    for j in range(SEG // 128):
                        dst[j, pl.ds(rho, tr // pd, stride=pd), :] = \
                            src[:, rho * SEG + j * 128:rho * SEG + (j + 1) * 128]
    low = lax.broadcasted_iota(jnp.int32, (qb, 128), 1) < HEAD_DIM
    nblk = length // qb

    def scores(blk, rho, a):
        krows = slice(blk * qb, blk * qb + kb)
        gblk = c * nb + blk
        var = jnp.where(gblk == 0, 1, 0) + jnp.where(gblk == nblk - 1, 2, 0)
        base = rho * DA_W + a * 128
        qa = main_ref[blk * qb:(blk + 1) * qb, base:base + 128]
        q2 = jnp.concatenate([jnp.where(low, qa, jnp.zeros_like(qa)), jnp.where(low, jnp.zeros_like(qa), qa)], axis=0)
        bias = jnp.concatenate([bias_ref[var, 2 * a], bias_ref[var, 2 * a + 1]], axis=0)
        s = _dot_nt(q2, buf[krows, base + SEG:base + SEG + 128]) + bias
        return s, buf[krows, base + 2 * SEG:base + 2 * SEG + 128]

    items = [(blk, rho, a) for blk in range(nb) for rho in range(d) for a in range(2)]
    nxt = scores(*items[0])
    for n, (blk, rho, a) in enumerate(items):
        s, va = nxt
        if n + 1 < len(items):
            nxt = scores(*items[n + 1])
        rows = slice(blk * qb, (blk + 1) * qb)
        m = jnp.max(s, axis=-1, keepdims=True)
        p = jnp.exp2(s - m)
        l = jnp.sum(p, axis=-1, keepdims=True)
        pv = _dot(p.astype(BF16), va) / l
        lse2 = jnp.broadcast_to(m + jnp.log2(l), (2 * qb, 128))
        o = jnp.where(low, pv[0:qb], pv[qb:])
        lse = jnp.where(low, lse2[0:qb], lse2[qb:])
        cols = slice(rho * SEG + a * 128, rho * SEG + (a + 1) * 128)
        if final:
            mx = lse
            for k in range(len(prev_dils)):
                mx = jnp.maximum(mx, und[2 * k + 1][a, rows, :])
            wsum = jnp.exp2(lse - mx)
            mix = wsum * o
            for k in range(len(prev_dils)):
                wk = jnp.exp2(und[2 * k + 1][a, rows, :] - mx)
                wsum = wsum + wk
                mix = mix + wk * und[2 * k][a, rows, :]
            o_ref[rows, cols] = (mix / wsum).astype(BF16)
        else:
            o_ref[rows, cols] = o
            lse_ref[rows, cols] = lse


DA_QB = {1: 128, 4: 128, 16: 128}
DA_BLOCKS = {1: 8, 4: 4, 16: 1}


def _da_call(view, colblk, bias, b, t, row_base, dilation, prev=None, prev_dils=(), out=None):
    d = dilation
    length = t // d
    qb = min(DA_QB[d], length)
    nb = min(DA_BLOCKS[d], length // qb)
    tr = qb * nb
    nt = length // tr
    kb = qb + 2 * DA_HALO
    ubase = row_base // d
    hpt = tr // DA_HALO
    nhalo = view.shape[0] // DA_HALO
    final = prev is not None

    def halo(right):
        def imap(bi, c):
            blk = (ubase + bi * length) // DA_HALO + ((c + 1) * hpt if right else c * hpt - 1)
            return (jnp.clip(blk, 0, nhalo - 1), colblk)
        return pl.BlockSpec((DA_HALO, d * DA_W), imap)

    in_specs = [pl.BlockSpec((tr, d * DA_W), lambda bi, c: ((ubase + bi * length) // tr + c, colblk)),
                halo(False), halo(True),
                pl.BlockSpec((4, DA_HEADS, qb, kb), lambda bi, c: (0, 0, 0, 0))]
    args = [view] * 3 + [bias]
    out_block = pl.BlockSpec((tr, d * SEG), lambda bi, c: (bi * nt + c, 0))
    scratch = [pltpu.VMEM((tr + 2 * DA_HALO, d * DA_W), BF16)]
    if final:
        assert d == 1
        for pd in prev_dils:
            in_specs += [pl.BlockSpec((tr // pd, pd * SEG), lambda bi, c: (bi * nt + c, 0))] * 2
            scratch += [pltpu.VMEM((SEG // 128, tr, 128), F32)] * 2
        args += list(prev)
        kern = functools.partial(_da_kernel, qb=qb, nb=nb, d=d, length=length, prev_dils=tuple(prev_dils))
        body, alias_specs, out_shape, aliases = _into(kern, len(in_specs), [out])
        return pl.pallas_call(
            body,
            grid=(b, nt),
            in_specs=in_specs + alias_specs,
            out_specs=[pl.BlockSpec((tr, SEG), lambda bi, c: ((ubase + bi * length) // tr + c, 0))],
            out_shape=out_shape,
            input_output_aliases=aliases,
            scratch_shapes=scratch,
            compiler_params=_cparams("parallel", "parallel"),
            name="da_d1",
        )(*args, out)[0]
    return pl.pallas_call(
        functools.partial(_da_kernel, qb=qb, nb=nb, d=d, length=length, prev_dils=()),
        grid=(b, nt),
        in_specs=in_specs,
        out_specs=[out_block, out_block],
        out_shape=[jax.ShapeDtypeStruct((b * t // d, d * SEG), F32)] * 2,
        scratch_shapes=scratch,
        compiler_params=_cparams("parallel", "parallel"),
        name="da_d%d" % d,
    )(*args)


HG_LEVELS = (32, 16, 8, 4, 2, 1)
HG_NSEC = 2 + len(HG_LEVELS)
HG_W = HG_HEADS * HG_DK
HG_PAIR = 2 * HG_DK


def _hg_constants(rev):
    c = HG_CHUNK
    t = np.arange(c)[:, None]
    j = np.arange(c)[None, :]
    secs = []
    if not rev:
        secs.append(j <= t)
        secs.append(j > t)
    else:
        secs.append(j >= t)
        secs.append(j < t)
    masks = [np.eye(c, dtype=bool)]
    s = np.arange(c)[None, :]
    for m in HG_LEVELS:
        p0 = (t // (2 * m)) * (2 * m)
        upper = (t & m) != 0
        if not rev:
            mid = p0 + m - 1
            sec = np.where(upper, (j > mid) & (j <= t), (j > t) & (j <= mid))
            pair = ((t // (2 * m)) == (s // (2 * m))) & upper & ((s & m) == 0)
        else:
            mid = p0 + m
            sec = np.where(upper, (j >= mid) & (j < t), (j >= t) & (j < mid))
            pair = ((t // (2 * m)) == (s // (2 * m))) & (~upper) & ((s & m) != 0)
        secs.append(sec)
        masks.append(pair)
    w = np.tile(np.concatenate(secs, axis=0).astype(np.float32), (1, 2))
    masks = np.tile(np.stack(masks).astype(np.float32), (1, 1, 2))
    return jnp.asarray(w, BF16), jnp.asarray(masks)


def _bdiag(h0, h1):
    zero = jnp.zeros_like(h0)
    return jnp.concatenate([jnp.concatenate([h0, zero], axis=1), jnp.concatenate([zero, h1], axis=1)], axis=0)


def _hg_pair_operand(a, p):
    return _bdiag(a[:, (2 * p) * HG_DK:(2 * p + 1) * HG_DK], a[:, (2 * p + 1) * HG_DK:(2 * p + 2) * HG_DK])


def _hg_prepare(rev, z, x, v, lb, w_ref, mask_ref):
    c_sz = HG_CHUNK
    log_lb = jnp.log(lb)
    log_1m = jnp.log1p(-lb)
    log_sig = jnp.minimum(z, 0.0) - jnp.log(1.0 + jnp.exp(-jnp.abs(z)))
    cc = log_1m + log_sig
    logf = jnp.maximum(log_lb, cc) + jnp.log(1.0 + jnp.exp(-jnp.abs(log_lb - cc)))
    kk = (1.0 - lb) * jnp.exp(log_sig - z)
    q = x / (1.0 + jnp.exp(-x))
    log2f = logf * LOG2_E
    hi = log2f.astype(BF16)
    lo = (log2f - hi.astype(F32)).astype(BF16)
    dec = jnp.exp2(_dot(w_ref[...], jnp.concatenate([hi, lo], axis=0)))
    sub = lax.broadcasted_iota(jnp.int32, (1, 8, HG_W), 1)
    ys = []
    for lvl, m in enumerate(HG_LEVELS):
        if m >= 8:
            sel = jnp.concatenate(
                [(q if (((8 * r) & m) != 0) != rev else kk)[8 * r:8 * r + 8] for r in range(c_sz // 8)], axis=0)
        else:
            is_q = ((sub & m) == 0) if rev else ((sub & m) != 0)
            sel = jnp.where(is_q, q.reshape(c_sz // 8, 8, HG_W), kk.reshape(c_sz // 8, 8, HG_W)).reshape(c_sz, HG_W)
        ys.append((sel * dec[(2 + lvl) * c_sz:(3 + lvl) * c_sz]).astype(BF16))
    q16, k16 = q.astype(BF16), kk.astype(BF16)
    kd = (kk * dec[c_sz:2 * c_sz]).astype(BF16)
    intra = []
    for p in range(HG_HEADS // 2):
        cols = slice(p * HG_PAIR, (p + 1) * HG_PAIR)
        a2 = mask_ref[0] * _dot_nt(q16[:, cols], _hg_pair_operand(k16, p))
        for lvl in range(len(HG_LEVELS)):
            a2 = a2 + mask_ref[1 + lvl] * _dot_nt(ys[lvl][:, cols], _hg_pair_operand(ys[lvl], p))
        intra.append(_dot(a2.astype(BF16), _hg_pair_operand(v, p)))
    gain = [_dot_tn(v[:, h * HG_DK:(h + 1) * HG_DK], kd[:, h * HG_DK:(h + 1) * HG_DK]) for h in range(HG_HEADS)]
    edge = 0 if rev else c_sz - 1
    return dict(intra=intra, gain=gain, qd=(q * dec[0:c_sz]).astype(BF16), decay=dec[edge:edge + 1])


def _hg_finish(prep, st_ref):
    outs = []
    for p in range(HG_HEADS // 2):
        cols = slice(p * HG_PAIR, (p + 1) * HG_PAIR)
        st = _bdiag(st_ref[2 * p].astype(BF16), st_ref[2 * p + 1].astype(BF16))
        outs.append(prep["intra"][p] + _dot_nt(prep["qd"][:, cols], st))
        for h in (2 * p, 2 * p + 1):
            st_ref[h] = st_ref[h] * prep["decay"][:, h * HG_DK:(h + 1) * HG_DK] + prep["gain"][h]
    return jnp.concatenate(outs, axis=1)


def _hg_kernel(cqf_ref, cff_ref, cif_ref, cqb_ref, cfb_ref, cib_ref, lb_ref, wf_ref, wb_ref, mf_ref, mb_ref,
               of_ref, ob_ref, st_ref, *, nchunks):
    @pl.when(pl.program_id(1) == 0)
    def _():
        st_ref[...] = jnp.zeros_like(st_ref)

    def prepare(step):
        rf = slice(step * HG_CHUNK, (step + 1) * HG_CHUNK)
        rb = slice((nchunks - 1 - step) * HG_CHUNK, (nchunks - step) * HG_CHUNK)
        fwd = _hg_prepare(False, cff_ref[rf, :].astype(F32), cqf_ref[rf, :].astype(F32), cif_ref[rf, :],
                          lb_ref[0:1, :], wf_ref, mf_ref)
        bwd = _hg_prepare(True, cfb_ref[rb, :].astype(F32), cqb_ref[rb, :].astype(F32), cib_ref[rb, :],
                          lb_ref[1:2, :], wb_ref, mb_ref)
        return rf, fwd, rb, bwd

    def finish(rf, fwd, rb, bwd):
        of_ref[rf, :] = _hg_finish(fwd, st_ref.at[0]).astype(BF16)
        ob_ref[rb, :] = _hg_finish(bwd, st_ref.at[1]).astype(BF16)

    pending = [prepare(step) for step in range(nchunks)]
    for prep in pending:
        finish(*prep)


def _hg_call(proj, lb, consts_f, consts_b, b, t, row_base, outs):
    tb = min(256, t)
    nt = t // tb
    base = row_base // tb

    def pspec(col, rev):
        return pl.BlockSpec((tb, HG_W), lambda bi, i: (base + bi * nt + (nt - 1 - i if rev else i), col))

    def const(a):
        return pl.BlockSpec(a.shape, lambda bi, i: (0,) * a.ndim)

    (wf, mf), (wb, mb) = consts_f, consts_b
    in_specs = [pspec(3, False), pspec(4, False), pspec(6, False), pspec(3, True), pspec(5, True), pspec(6, True),
                const(lb), const(wf), const(wb), const(mf), const(mb)]
    body, alias_specs, out_shape, aliases = _into(functools.partial(_hg_kernel, nchunks=tb // HG_CHUNK),
                                                  len(in_specs), list(outs))
    return pl.pallas_call(
        body,
        grid=(b, nt),
        in_specs=in_specs + alias_specs,
        out_specs=[pspec(0, False), pspec(0, True)],
        out_shape=out_shape,
        input_output_aliases=aliases,
        scratch_shapes=[pltpu.VMEM((2, HG_HEADS, HG_DK, HG_DK), F32)],
        compiler_params=_cparams("parallel", "arbitrary"),
        name="hgrn",
    )(proj, proj, proj, proj, proj, proj, lb, wf, wb, mf, mb, *outs)


def _outproj_kernel(oa_ref, ob_ref, cf_ref, cb_ref, cg_ref, ng_ref, w_ref, g_ref, *rest, split):
    x_refs, (x1_ref, h_ref) = rest[:-2], rest[-2:]
    tot = cf_ref[...].astype(F32) + cb_ref[...].astype(F32)
    gate = cg_ref[...].astype(F32)
    parts = []
    for h in range(HG_HEADS):
        th = tot[:, h * HG_DK:(h + 1) * HG_DK]
        ms = jnp.mean(th * th, axis=-1, keepdims=True)
        parts.append(th * lax.rsqrt(ms + NORM_EPS) * ng_ref[...])
    oc = (jnp.concatenate(parts, axis=1) * (gate / (1.0 + jnp.exp(-gate)))).astype(BF16)
    y = _dot(oa_ref[...], w_ref[0:SEG, :]) + _dot(ob_ref[...], w_ref[SEG:2 * SEG, :]) + _dot(oc, w_ref[2 * SEG:, :])
    x1 = _select_rows(x_refs, split) + y
    x1_ref[...] = x1
    ms = jnp.mean(x1 * x1, axis=-1, keepdims=True)
    h_ref[...] = (x1 * lax.rsqrt(ms + NORM_EPS) * g_ref[...]).astype(BF16)


def _out_proj(oa, ob, ocf, ocb, proj, norm_g, xs, w_out, ln_g, tm):
    n = oa.shape[0]
    row = lambda i: (i, 0)
    const = lambda i: (0, 0)
    x_specs, split = _row_specs(xs, tm)
    return pl.pallas_call(
        functools.partial(_outproj_kernel, split=split),
        grid=(n // tm,),
        in_specs=[pl.BlockSpec((tm, SEG), row), pl.BlockSpec((tm, SEG), row),
                  pl.BlockSpec((tm, HG_W), row), pl.BlockSpec((tm, HG_W), row),
                  pl.BlockSpec((tm, HG_W), lambda i: (i, N_IN // HG_W - 1)),
                  pl.BlockSpec((1, HG_DK), const),
                  pl.BlockSpec((D_MODEL, D_MODEL), const),
                  pl.BlockSpec((1, D_MODEL), const)] + x_specs,
        out_specs=[pl.BlockSpec((tm, D_MODEL), row), pl.BlockSpec((tm, D_MODEL), row)],
        out_shape=[jax.ShapeDtypeStruct((n, D_MODEL), F32), jax.ShapeDtypeStruct((n, D_MODEL), BF16)],
        compiler_params=_cparams("parallel"),
        name="out_proj",
    )(oa, ob, ocf, ocb, proj, norm_g, w_out, ln_g, *xs)


FF_HALO = 16
FF_ROWS = 1024


def _ffn_kernel(x_ref, h_ref, hp_ref, hn_ref, wg_ref, wu_ref, cw_ref, cb_ref, wdn_ref, o_ref, hbuf, abuf, *, tm, row_base,
                starts, ends):
    row0 = row_base + pl.program_id(0) * tm
    at_start = functools.reduce(jnp.logical_or, [row0 == s for s in starts])
    at_end = functools.reduce(jnp.logical_or, [row0 + tm == e for e in ends])
    keep_prev = jnp.where(at_start, 0.0, 1.0)
    keep_next = jnp.where(at_end, 0.0, 1.0)
    hbuf[0:tm] = h_ref[...]
    hbuf[tm:tm + FF_HALO] = hp_ref[...]
    hbuf[tm + FF_HALO:tm + 2 * FF_HALO] = hn_ref[...]
    first = lax.broadcasted_iota(jnp.int32, (8, FF_SEG), 0) == 0
    last = lax.broadcasted_iota(jnp.int32, (8, FF_SEG), 0) == 7
    c_tanh = math.sqrt(2.0 / math.pi)
    nseg = D_FF // FF_SEG

    def up_proj(j):
        w = jnp.concatenate([wg_ref[:, j * FF_SEG:(j + 1) * FF_SEG], wu_ref[:, j * FF_SEG:(j + 1) * FF_SEG]], axis=1)
        return _dot(hbuf[...], w)

    y_next = up_proj(0)
    for j in range(nseg):
        cols = slice(j * FF_SEG, (j + 1) * FF_SEG)
        y = y_next
        if j + 1 < nseg:
            y_next = up_proj(j + 1)
        g, up = y[0:tm, 0:FF_SEG], y[0:tm, FF_SEG:]
        g_prev = y[tm + FF_HALO - 1:tm + FF_HALO, 0:FF_SEG] * keep_prev
        g_next = y[tm + FF_HALO:tm + FF_HALO + 1, 0:FF_SEG] * keep_next
        cw0, cw1, cw2 = cw_ref[0:1, cols], cw_ref[1:2, cols], cw_ref[2:3, cols]
        gate = pltpu.roll(g, 1, 0) * cw0 + g * cw1 + pltpu.roll(g, tm - 1, 0) * cw2 + cb_ref[:, cols]
        top = gate[0:8] + jnp.where(first, (g_prev - g[tm - 1:tm]) * cw0, 0.0)
        bot = gate[tm - 8:tm] + jnp.where(last, (g_next - g[0:1]) * cw2, 0.0)
        gate = jnp.concatenate([top, gate[8:tm - 8], bot], axis=0)
        inner = gate * (c_tanh + (c_tanh * 0.044715) * (gate * gate))
        act = (gate * up) * (1.0 + jnp.tanh(inner))
        abuf[:, cols] = act.astype(BF16)
    o_ref[...] = x_ref[...] + _dot(abuf[...], wdn_ref[...])


def _ffn(x1, h2, w_up, conv_w, conv_b, w_down, tm, starts, ends, row_base=0, nrows=None):
    n = x1.shape[0]
    nrows = n if nrows is None else nrows
    hpb = tm // FF_HALO
    nh = n // FF_HALO
    t0 = row_base // tm
    row = lambda i: (t0 + i, 0)
    const = lambda i: (0, 0)
    once = pl.Buffered(1)
    return pl.pallas_call(
        functools.partial(_ffn_kernel, tm=tm, row_base=row_base, starts=starts, ends=ends),
        grid=(nrows // tm,),
        in_specs=[pl.BlockSpec((tm, D_MODEL), row), pl.BlockSpec((tm, D_MODEL), row),
                  pl.BlockSpec((FF_HALO, D_MODEL), lambda i: (jnp.maximum((t0 + i) * hpb - 1, 0), 0)),
                  pl.BlockSpec((FF_HALO, D_MODEL), lambda i: (jnp.minimum((t0 + i + 1) * hpb, nh - 1), 0)),
                  pl.BlockSpec((D_MODEL, D_FF), lambda i: (0, 0), pipeline_mode=once),
                  pl.BlockSpec((D_MODEL, D_FF), lambda i: (0, 1), pipeline_mode=once),
                  pl.BlockSpec((3, D_FF), const), pl.BlockSpec((1, D_FF), const),
                  pl.BlockSpec((D_FF, D_MODEL), const, pipeline_mode=once)],
        out_specs=pl.BlockSpec((tm, D_MODEL), lambda i: (i, 0)),
        out_shape=jax.ShapeDtypeStruct((nrows, D_MODEL), F32),
        scratch_shapes=[pltpu.VMEM((tm + 2 * FF_HALO, D_MODEL), BF16), pltpu.VMEM((tm, D_FF), BF16)],
        compiler_params=_cparams("parallel"),
        name="ffn",
    )(x1, h2, h2, h2, w_up, w_up, conv_w, conv_b, w_down)


def _lower_bounds(lb_logits):
    p = jax.nn.softmax(lb_logits.astype(F32), axis=1)
    c = jnp.cumsum(p, axis=1)
    return c - c[:, :1]


def _row_tile(groups, tm=512):
    while any(t % tm for _, t, _ in groups):
        tm //= 2
    return tm


def kernel(x_prompt, x_sample, ln_mix_g, w_in, na_q_g, na_k_g, na_rpb, da_q_g, da_k_g, t5_bias, hg_lb_logits,
           hg_norm_g, w_out, ln_ffn_g, w_up, conv_w, conv_b, w_down):
    depth = w_in.shape[0]
    groups = []
    base = 0
    for xg in (x_prompt, x_sample):
        b, t, _ = xg.shape
        groups.append((b, t, base))
        base += b * t
    starts = tuple(rb + i * t for b, t, rb in groups for i in range(b))
    ends = tuple(rb + (i + 1) * t for b, t, rb in groups for i in range(b))
    tm = _row_tile(groups)
    tm_ffn = _row_tile(groups, FF_ROWS)
    n = base
    xs = [x_prompt.reshape(-1, D_MODEL), x_sample.reshape(-1, D_MODEL)]

    lb = _lower_bounds(hg_lb_logits)
    scale = HEAD_DIM ** -0.5
    tile4 = lambda g: jnp.tile(g.astype(F32), NA_HEADS)
    gm = jnp.asarray(np.kron(np.eye(SEG // HEAD_DIM), np.ones((HEAD_DIM, HEAD_DIM))) / HEAD_DIM, BF16)
    hg_fwd_consts = _hg_constants(False)
    hg_bwd_consts = _hg_constants(True)
    da_bias = {}

    def da_bias_for(d, qb):
        if (d, qb) not in da_bias:
            da_bias[(d, qb)] = _da_bias_table(t5_bias, d, qb)
        return da_bias[(d, qb)]

    for l in range(depth):
        qg = jnp.stack([tile4(na_q_g[l]) * (scale * LOG2_E), tile4(na_k_g[l]),
                        tile4(da_q_g[l]) * (scale * LOG2_E), tile4(da_k_g[l])])
        proj, *dil_views = _in_proj(xs, ln_mix_g[l].reshape(1, -1), w_in[l].astype(BF16), qg, gm, tm_ffn)
        na_bias = _na_bias_table(na_rpb[l])
        oa = jnp.zeros((n, SEG), BF16)
        ob = jnp.zeros((n, SEG), BF16)
        ocf = jnp.zeros((n, HG_W), BF16)
        ocb = jnp.zeros((n, HG_W), BF16)
        for b, t, rb in groups:
            oa = _na_call(proj, na_bias, b, t, rb, oa)
            prev = []
            for d, view in zip(DA_DILATED, dil_views):
                prev += _da_call(view, 0, da_bias_for(d, min(DA_QB[d], t // d)), b, t, rb, d)
            ob = _da_call(proj, DA_SEGS[0] * SEG // DA_W, da_bias_for(1, min(DA_QB[1], t)), b, t, rb, 1,
                          prev=prev, prev_dils=DA_DILATED, out=ob)
            ocf, ocb = _hg_call(proj, lb[:, l], hg_fwd_consts, hg_bwd_consts, b, t, rb, (ocf, ocb))
        x1, h2 = _out_proj(oa, ob, ocf, ocb, proj, hg_norm_g[l].reshape(1, -1).astype(F32), xs, w_out[l].astype(BF16),
                           ln_ffn_g[l].reshape(1, -1), tm)
        wu, wd = w_up[l].astype(BF16), w_down[l].astype(BF16) * 0.5
        cw, cb = conv_w[l], conv_b[l].reshape(1, -1)
        if l + 1 < depth:
            xs = [_ffn(x1, h2, wu, cw, cb, wd, tm_ffn, starts, ends)]
    return tuple(_ffn(x1, h2, wu, cw, cb, wd, tm_ffn, starts, ends, rb, b * t).reshape(xg.shape)
                 for (b, t, rb), xg in zip(groups, (x_prompt, x_sample)))
```

```python
import functools
import math

import numpy as np
import jax
import jax.numpy as jnp
from jax import lax
from jax.experimental import pallas as pl
from jax.experimental.pallas import tpu as pltpu

F32 = jnp.float32
BF16 = jnp.bfloat16

D_MODEL = 1024
GRID_W = 64
HEAD_DIM = 64
NA_HEADS = 4
NA_WIN_ROWS = 8
NA_WIN_COLS = 16
DA_HEADS = 4
DA_CONFIGS = ((128, 1), (512, 4), (2048, 16))
DA_HALO = 64
N_BUCKETS = 32
MAX_DISTANCE = 1024
HG_HEADS = 4
HG_DK = 128
HG_CHUNK = 64
N_IN = 4096
D_FF = 2816
NORM_EPS = 1e-6
NEG_INF = -1e30
LOG2_E = math.log2(math.e)
SEG = 256
FF_SEG = 256
VMEM_LIMIT = 56 * 1024 * 1024


def _cparams(*sem):
    return pltpu.CompilerParams(dimension_semantics=sem, vmem_limit_bytes=VMEM_LIMIT)


def _into(kernel, n_in, outs):
    def body(*refs):
        return kernel(*refs[:n_in], *refs[n_in + len(outs):])
    specs = [pl.BlockSpec(memory_space=pl.ANY)] * len(outs)
    shapes = [jax.ShapeDtypeStruct(o.shape, o.dtype) for o in outs]
    return body, specs, shapes, {n_in + k: k for k in range(len(outs))}


def _dot(a, b):
    return jnp.dot(a, b, preferred_element_type=F32)


def _dot_nt(a, b):
    return lax.dot_general(a, b, (((1,), (1,)), ((), ())), preferred_element_type=F32)


def _dot_tn(a, b):
    return lax.dot_general(a, b, (((0,), (0,)), ((), ())), preferred_element_type=F32)


DA_SEGS = (3, 4, 5)
DA_W = len(DA_SEGS) * SEG
DA_DILATED = tuple(d for _, d in DA_CONFIGS if d > 1)


def _select_rows(x_refs, split):
    if len(x_refs) == 1:
        return x_refs[0][...]
    return jnp.where(pl.program_id(0) < split, x_refs[0][...], x_refs[1][...])


def _row_specs(xs, tm):
    if len(xs) == 1:
        return [pl.BlockSpec((tm, xs[0].shape[1]), lambda i: (i, 0))], 0
    split = xs[0].shape[0] // tm
    return [pl.BlockSpec((tm, xs[0].shape[1]), lambda i: (jnp.minimum(i, split - 1), 0)),
            pl.BlockSpec((tm, xs[1].shape[1]), lambda i: (jnp.maximum(i - split, 0), 0))], split


def _inproj_kernel(*refs, tm, nx, split):
    x_refs, (g_ref, w_ref, qg_ref, gm_ref, o_ref), rest = refs[:nx], refs[nx:nx + 5], refs[nx + 5:]
    dil_refs, ybuf, hbuf = rest[:-2], rest[-2], rest[-1]
    x = _select_rows(x_refs, split)
    ms = jnp.mean(x * x, axis=-1, keepdims=True)
    hbuf[...] = (x * lax.rsqrt(ms + NORM_EPS) * g_ref[...]).astype(BF16)
    normed = {0: 0, 1: 1, 3: 2, 4: 3}
    nseg = N_IN // SEG
    y_next = _dot(hbuf[...], w_ref[:, 0:SEG])
    for seg in range(nseg):
        y = y_next
        if seg + 1 < nseg:
            y_next = _dot(hbuf[...], w_ref[:, (seg + 1) * SEG:(seg + 2) * SEG])
        if seg in normed:
            ss = _dot((y * y).astype(BF16), gm_ref[...])
            r = normed[seg]
            y = y * lax.rsqrt(ss + NORM_EPS) * qg_ref[r:r + 1, :]
        if seg in DA_SEGS:
            k = DA_SEGS.index(seg)
            for j in range(SEG // 128):
                ybuf[k * (SEG // 128) + j] = y[:, j * 128:(j + 1) * 128]
        o_ref[:, seg * SEG:(seg + 1) * SEG] = y.astype(BF16)
        if seg == DA_SEGS[-1]:
            for d, ref in zip(DA_DILATED, dil_refs):
                for rho in range(d):
                    for j in range(DA_W // 128):
                        ref[:, rho * DA_W + j * 128:rho * DA_W + (j + 1) * 128] = \
                            ybuf[j, pl.ds(rho, tm // d, stride=d), :].astype(BF16)


def _in_proj(xs, ln_g, w_in, qg, gm, tm):
    n = sum(x.shape[0] for x in xs)
    row = lambda i: (i, 0)
    const = lambda i: (0, 0)
    x_specs, split = _row_specs(xs, tm)
    return pl.pallas_call(
        functools.partial(_inproj_kernel, tm=tm, nx=len(xs), split=split),
        grid=(n // tm,),
        in_specs=x_specs + [
            pl.BlockSpec((1, D_MODEL), const),
            pl.BlockSpec((D_MODEL, N_IN), const, pipeline_mode=pl.Buffered(1)),
            pl.BlockSpec((4, SEG), const),
            pl.BlockSpec((SEG, SEG), const),
        ],
        out_specs=[pl.BlockSpec((tm, N_IN), row)] + [pl.BlockSpec((tm // d, d * DA_W), row) for d in DA_DILATED],
        out_shape=[jax.ShapeDtypeStruct((n, N_IN), BF16)]
        + [jax.ShapeDtypeStruct((n // d, d * DA_W), BF16) for d in DA_DILATED],
        scratch_shapes=[pltpu.VMEM((DA_W // 128, tm, 128), F32),
                        pltpu.VMEM((tm, D_MODEL), BF16)],
        compiler_params=_cparams("parallel"),
        name="in_proj",
    )(*xs, ln_g, w_in, qg, gm)


NA_RB = 8
NA_TOK = NA_RB * GRID_W
NA_GR = 4
NA_GQ = NA_GR * GRID_W
NA_KR = NA_GR + NA_WIN_ROWS
NA_GK = NA_KR * GRID_W
NA_NDR = 2 * NA_WIN_ROWS - 1


def _na_bias_table(rpb):
    cq = np.arange(GRID_W)
    ck = np.arange(GRID_W)
    c0 = np.clip(cq - NA_WIN_COLS // 2, 0, GRID_W - NA_WIN_COLS)
    col_ok = (ck[None, :] >= c0[:, None]) & (ck[None, :] < c0[:, None] + NA_WIN_COLS)
    dc = np.clip(ck[None, :] - cq[:, None], -(NA_WIN_COLS - 1), NA_WIN_COLS - 1) + NA_WIN_COLS - 1
    col_onehot = (dc[:, :, None] == np.arange(2 * NA_WIN_COLS - 1)).astype(np.float32)
    jq = np.arange(NA_GR)
    half = NA_WIN_ROWS // 2
    dq = np.stack([jq, half + jq, NA_WIN_ROWS + jq])
    dw0 = np.stack([0 * jq, jq, half + 0 * jq])
    w = np.arange(NA_KR)
    row_ok = (w[None, None, :] >= dw0[:, :, None]) & (w[None, None, :] < dw0[:, :, None] + NA_WIN_ROWS)
    dr = w[None, None, :] - dq[:, :, None] + NA_WIN_ROWS - 1
    row_onehot = ((dr[..., None] == np.arange(NA_NDR)) & row_ok[..., None]).astype(np.float32)
    t = jnp.einsum('hab,qkb->haqk', rpb.astype(F32), col_onehot, precision=lax.Precision.HIGHEST)
    bias = jnp.einsum('cjwa,haqk->chjqwk', row_onehot, t, precision=lax.Precision.HIGHEST)
    ok = row_ok[:, None, :, None, :, None] & col_ok[None, None, None, :, None, :]
    return jnp.where(ok, bias * LOG2_E, NEG_INF).reshape(3, NA_HEADS, NA_GQ, NA_GK)


def _na_kernel(q_ref, kp_ref, kc_ref, kn_ref, vp_ref, vc_ref, vn_ref, bias0_ref, bias1_ref, o_ref, kbuf, vbuf, *,
               rows):
    i = pl.program_id(1)
    kbuf[0:NA_TOK] = kp_ref[...]
    kbuf[NA_TOK:2 * NA_TOK] = kc_ref[...]
    kbuf[2 * NA_TOK:3 * NA_TOK] = kn_ref[...]
    vbuf[0:NA_TOK] = vp_ref[...]
    vbuf[NA_TOK:2 * NA_TOK] = vc_ref[...]
    vbuf[2 * NA_TOK:3 * NA_TOK] = vn_ref[...]
    low = lax.broadcasted_iota(jnp.int32, (NA_GQ, 128), 1) < HEAD_DIM
    bias_refs = (bias0_ref, bias1_ref)

    def scores(g, a):
        r = i * NA_RB + g * NA_GR
        kstart = jnp.clip(r - NA_WIN_ROWS // 2, 0, rows - NA_KR)
        off = pl.multiple_of((kstart - i * NA_RB + NA_RB) * GRID_W, GRID_W)
        cols = slice(a * 128, (a + 1) * 128)
        qa = q_ref[g * NA_GQ:(g + 1) * NA_GQ, cols]
        q2 = jnp.concatenate([jnp.where(low, qa, jnp.zeros_like(qa)), jnp.where(low, jnp.zeros_like(qa), qa)], axis=0)
        bias = jnp.concatenate([bias_refs[g][0, 2 * a], bias_refs[g][0, 2 * a + 1]], axis=0)
        return _dot_nt(q2, kbuf[pl.ds(off, NA_GK), cols]) + bias, vbuf[pl.ds(off, NA_GK), cols]

    items = [(g, a) for g in range(NA_RB // NA_GR) for a in range(2)]
    nxt = scores(*items[0])
    for n, (g, a) in enumerate(items):
        s, va = nxt
        if n + 1 < len(items):
            nxt = scores(*items[n + 1])
        m = jnp.max(s, axis=-1, keepdims=True)
        p = jnp.exp2(s - m)
        l = jnp.sum(p, axis=-1, keepdims=True)
        pv = _dot(p.astype(BF16), va) / l
        o_ref[g * NA_GQ:(g + 1) * NA_GQ, a * 128:(a + 1) * 128] = \
            jnp.where(low, pv[0:NA_GQ], pv[NA_GQ:]).astype(BF16)


def _na_call(proj, bias, b, t, row_base, out):
    rows = t // GRID_W
    nrb = rows // NA_RB
    base = row_base // NA_TOK
    assert rows >= NA_KR and rows % NA_RB == 0

    def spec(col, shift):
        def imap(bi, i):
            return (base + bi * nrb + jnp.clip(i + shift, 0, nrb - 1), col)
        return pl.BlockSpec((NA_TOK, SEG), imap)

    bias_block = (1, NA_HEADS, NA_GQ, NA_GK)
    bias0 = pl.BlockSpec(bias_block, lambda bi, i: (jnp.where(i == 0, 0, 1), 0, 0, 0))
    bias1 = pl.BlockSpec(bias_block, lambda bi, i: (jnp.where(i == nrb - 1, 2, 1), 0, 0, 0))
    in_specs = [spec(0, 0), spec(1, -1), spec(1, 0), spec(1, 1), spec(2, -1), spec(2, 0), spec(2, 1), bias0, bias1]
    body, alias_specs, out_shape, aliases = _into(functools.partial(_na_kernel, rows=rows), len(in_specs), [out])
    return pl.pallas_call(
        body,
        grid=(b, nrb),
        in_specs=in_specs + alias_specs,
        out_specs=[pl.BlockSpec((NA_TOK, SEG), lambda bi, i: (base + bi * nrb + i, 0))],
        out_shape=out_shape,
        input_output_aliases=aliases,
        scratch_shapes=[pltpu.VMEM((3 * NA_TOK, SEG), BF16), pltpu.VMEM((3 * NA_TOK, SEG), BF16)],
        compiler_params=_cparams("parallel", "parallel"),
        name="na",
    )(proj, proj, proj, proj, proj, proj, proj, bias, bias, out)[0]


def _t5_bucket(rel):
    nb = N_BUCKETS // 2
    max_exact = nb // 2
    ret = np.where(rel > 0, nb, 0)
    n = np.abs(rel)
    large = max_exact + (np.log(np.maximum(n, 1) / max_exact)
                         / np.log(MAX_DISTANCE / max_exact) * (nb - max_exact)).astype(np.int32)
    large = np.minimum(large, nb - 1)
    return (ret + np.where(n < max_exact, n, large)).astype(np.int32)


def _da_bias_table(t5_bias, dilation, qb):
    kb = qb + 2 * DA_HALO
    rel = np.arange(kb)[None, :] - np.arange(qb)[:, None] - DA_HALO
    ok = np.abs(rel) <= DA_HALO
    onehot = (_t5_bucket(rel * dilation)[:, :, None] == np.arange(N_BUCKETS)).astype(np.float32)
    bias = jnp.einsum('qkb,bh->hqk', onehot, t5_bias.astype(F32), precision=lax.Precision.HIGHEST)
    j = np.arange(kb)
    edge = np.stack([np.ones(kb, bool), j >= DA_HALO, j < qb + DA_HALO, (j >= DA_HALO) & (j < qb + DA_HALO)])
    return jnp.where(ok[None, None] & edge[:, None, None, :], (bias * LOG2_E)[None], NEG_INF)


def _da_kernel(*refs, qb, nb, d, length, prev_dils):
    final = bool(prev_dils)
    (main_ref, left_ref, right_ref, bias_ref), refs = refs[:4], refs[4:]
    if final:
        npv = 2 * len(prev_dils)
        prev_refs, o_ref, buf, und = refs[:npv], refs[npv], refs[npv + 1], refs[npv + 2:]
    else:
        o_ref, lse_ref, buf = refs
    c = pl.program_id(1)
    tr = qb * nb
    kb = qb + 2 * DA_HALO
    buf[0:DA_HALO] = left_ref[...]
    buf[DA_HALO:DA_HALO + tr] = main_ref[...]
    buf[DA_HALO + tr:tr + 2 * DA_HALO] = right_ref[...]
    if final:
        for k, pd in enumerate(prev_dils):
            for part in range(2):
                src, dst = prev_refs[2 * k + part], und[2 * k + part]
                for rho in range(pd):
                    for j in range(SEG // 128):
                        dst[j, pl.ds(rho, tr // pd, stride=pd), :] = \
                            src[:, rho * SEG + j * 128:rho * SEG + (j + 1) * 128]
    low = lax.broadcasted_iota(jnp.int32, (qb, 128), 1) < HEAD_DIM
    nblk = length // qb

    def scores(blk, rho, a):
        krows = slice(blk * qb, blk * qb + kb)
        gblk = c * nb + blk
        var = jnp.where(gblk == 0, 1, 0) + jnp.where(gblk == nblk - 1, 2, 0)
        base = rho * DA_W + a * 128
        qa = main_ref[blk * qb:(blk + 1) * qb, base:base + 128]
        q2 = jnp.concatenate([jnp.where(low, qa, jnp.zeros_like(qa)), jnp.where(low, jnp.zeros_like(qa), qa)], axis=0)
        bias = jnp.concatenate([bias_ref[var, 2 * a], bias_ref[var, 2 * a + 1]], axis=0)
        s = _dot_nt(q2, buf[krows, base + SEG:base + SEG + 128]) + bias
        return s, buf[krows, base + 2 * SEG:base + 2 * SEG + 128]

    items = [(blk, rho, a) for blk in range(nb) for rho in range(d) for a in range(2)]
    nxt = scores(*items[0])
    for n, (blk, rho, a) in enumerate(items):
        s, va = nxt
        if n + 1 < len(items):
            nxt = scores(*items[n + 1])
        rows = slice(blk * qb, (blk + 1) * qb)
        m = jnp.max(s, axis=-1, keepdims=True)
        p = jnp.exp2(s - m)
        l = jnp.sum(p, axis=-1, keepdims=True)
        pv = _dot(p.astype(BF16), va) / l
        lse2 = jnp.broadcast_to(m + jnp.log2(l), (2 * qb, 128))
        o = jnp.where(low, pv[0:qb], pv[qb:])
        lse = jnp.where(low, lse2[0:qb], lse2[qb:])
        cols = slice(rho * SEG + a * 128, rho * SEG + (a + 1) * 128)
        if final:
            mx = lse
            for k in range(len(prev_dils)):
                mx = jnp.maximum(mx, und[2 * k + 1][a, rows, :])
            wsum = jnp.exp2(lse - mx)
            mix = wsum * o
            for k in range(len(prev_dils)):
                wk = jnp.exp2(und[2 * k + 1][a, rows, :] - mx)
                wsum = wsum + wk
                mix = mix + wk * und[2 * k][a, rows, :]
            o_ref[rows, cols] = (mix / wsum).astype(BF16)
        else:
            o_ref[rows, cols] = o
            lse_ref[rows, cols] = lse


DA_QB = {1: 128, 4: 128, 16: 128}
DA_BLOCKS = {1: 8, 4: 4, 16: 1}


def _da_call(view, colblk, bias, b, t, row_base, dilation, prev=None, prev_dils=(), out=None):
    d = dilation
    length = t // d
    qb = min(DA_QB[d], length)
    nb = min(DA_BLOCKS[d], length // qb)
    tr = qb * nb
    nt = length // tr
    kb = qb + 2 * DA_HALO
    ubase = row_base // d
    hpt = tr // DA_HALO
    nhalo = view.shape[0] // DA_HALO
    final = prev is not None

    def halo(right):
        def imap(bi, c):
            blk = (ubase + bi * length) // DA_HALO + ((c + 1) * hpt if right else c * hpt - 1)
            return (jnp.clip(blk, 0, nhalo - 1), colblk)
        return pl.BlockSpec((DA_HALO, d * DA_W), imap)

    in_specs = [pl.BlockSpec((tr, d * DA_W), lambda bi, c: ((ubase + bi * length) // tr + c, colblk)),
                halo(False), halo(True),
                pl.BlockSpec((4, DA_HEADS, qb, kb), lambda bi, c: (0, 0, 0, 0))]
    args = [view] * 3 + [bias]
    out_block = pl.BlockSpec((tr, d * SEG), lambda bi, c: (bi * nt + c, 0))
    scratch = [pltpu.VMEM((tr + 2 * DA_HALO, d * DA_W), BF16)]
    if final:
        assert d == 1
        for pd in prev_dils:
            in_specs += [pl.BlockSpec((tr // pd, pd * SEG), lambda bi, c: (bi * nt + c, 0))] * 2
            scratch += [pltpu.VMEM((SEG // 128, tr, 128), F32)] * 2
        args += list(prev)
        kern = functools.partial(_da_kernel, qb=qb, nb=nb, d=d, length=length, prev_dils=tuple(prev_dils))
        body, alias_specs, out_shape, aliases = _into(kern, len(in_specs), [out])
        return pl.pallas_call(
            body,
            grid=(b, nt),
            in_specs=in_specs + alias_specs,
            out_specs=[pl.BlockSpec((tr, SEG), lambda bi, c: ((ubase + bi * length) // tr + c, 0))],
            out_shape=out_shape,
            input_output_aliases=aliases,
            scratch_shapes=scratch,
            compiler_params=_cparams("parallel", "parallel"),
            name="da_d1",
        )(*args, out)[0]
    return pl.pallas_call(
        functools.partial(_da_kernel, qb=qb, nb=nb, d=d, length=length, prev_dils=()),
        grid=(b, nt),
        in_specs=in_specs,
        out_specs=[out_block, out_block],
        out_shape=[jax.ShapeDtypeStruct((b * t // d, d * SEG), F32)] * 2,
        scratch_shapes=scratch,
        compiler_params=_cparams("parallel", "parallel"),
        name="da_d%d" % d,
    )(*args)


HG_LEVELS = (32, 16, 8, 4, 2, 1)
HG_NSEC = 2 + len(HG_LEVELS)
HG_W = HG_HEADS * HG_DK
HG_PAIR = 2 * HG_DK


def _hg_constants(rev):
    c = HG_CHUNK
    t = np.arange(c)[:, None]
    j = np.arange(c)[None, :]
    secs = []
    if not rev:
        secs.append(j <= t)
        secs.append(j > t)
    else:
        secs.append(j >= t)
        secs.append(j < t)
    masks = [np.eye(c, dtype=bool)]
    s = np.arange(c)[None, :]
    for m in HG_LEVELS:
        p0 = (t // (2 * m)) * (2 * m)
        upper = (t & m) != 0
        if not rev:
            mid = p0 + m - 1
            sec = np.where(upper, (j > mid) & (j <= t), (j > t) & (j <= mid))
            pair = ((t // (2 * m)) == (s // (2 * m))) & upper & ((s & m) == 0)
        else:
            mid = p0 + m
            sec = np.where(upper, (j >= mid) & (j < t), (j >= t) & (j < mid))
            pair = ((t // (2 * m)) == (s // (2 * m))) & (~upper) & ((s & m) != 0)
        secs.append(sec)
        masks.append(pair)
    w = np.tile(np.concatenate(secs, axis=0).astype(np.float32), (1, 2))
    masks = np.tile(np.stack(masks).astype(np.float32), (1, 1, 2))
    return jnp.asarray(w, BF16), jnp.asarray(masks)


def _bdiag(h0, h1):
    zero = jnp.zeros_like(h0)
    return jnp.concatenate([jnp.concatenate([h0, zero], axis=1), jnp.concatenate([zero, h1], axis=1)], axis=0)


def _hg_pair_operand(a, p):
    return _bdiag(a[:, (2 * p) * HG_DK:(2 * p + 1) * HG_DK], a[:, (2 * p + 1) * HG_DK:(2 * p + 2) * HG_DK])


def _hg_prepare(rev, z, x, v, lb, w_ref, mask_ref):
    c_sz = HG_CHUNK
    z2 = z * LOG2_E
    log2_lb = jnp.log2(lb)
    log2_sig = jnp.minimum(z2, 0.0) - jnp.log2(1.0 + jnp.exp2(-jnp.abs(z2)))
    cc = jnp.log2(1.0 - lb) + log2_sig
    log2f = jnp.maximum(log2_lb, cc) + jnp.log2(1.0 + jnp.exp2(-jnp.abs(log2_lb - cc)))
    kk = (1.0 - lb) * jnp.exp2(log2_sig - z2)
    q = x / (1.0 + jnp.exp2(x * -LOG2_E))
    hi = log2f.astype(BF16)
    lo = (log2f - hi.astype(F32)).astype(BF16)
    dec = jnp.exp2(_dot(w_ref[...], jnp.concatenate([hi, lo], axis=0)))
    sub = lax.broadcasted_iota(jnp.int32, (1, 8, HG_W), 1)
    ys = []
    for lvl, m in enumerate(HG_LEVELS):
        if m >= 8:
            sel = jnp.concatenate(
                [(q if (((8 * r) & m) != 0) != rev else kk)[8 * r:8 * r + 8] for r in range(c_sz // 8)], axis=0)
        else:
            is_q = ((sub & m) == 0) if rev else ((sub & m) != 0)
            sel = jnp.where(is_q, q.reshape(c_sz // 8, 8, HG_W), kk.reshape(c_sz // 8, 8, HG_W)).reshape(c_sz, HG_W)
        ys.append((sel * dec[(2 + lvl) * c_sz:(3 + lvl) * c_sz]).astype(BF16))
    q16, k16 = q.astype(BF16), kk.astype(BF16)
    kd = (kk * dec[c_sz:2 * c_sz]).astype(BF16)
    intra = []
    for p in range(HG_HEADS // 2):
        cols = slice(p * HG_PAIR, (p + 1) * HG_PAIR)
        a2 = mask_ref[0] * _dot_nt(q16[:, cols], _hg_pair_operand(k16, p))
        for lvl in range(len(HG_LEVELS)):
            a2 = a2 + mask_ref[1 + lvl] * _dot_nt(ys[lvl][:, cols], _hg_pair_operand(ys[lvl], p))
        intra.append(_dot(a2.astype(BF16), _hg_pair_operand(v, p)))
    gain = [_dot_tn(v[:, h * HG_DK:(h + 1) * HG_DK], kd[:, h * HG_DK:(h + 1) * HG_DK]) for h in range(HG_HEADS)]
    edge = 0 if rev else c_sz - 1
    return dict(intra=intra, gain=gain, qd=(q * dec[0:c_sz]).astype(BF16), decay=dec[edge:edge + 1])


def _hg_finish(prep, st_ref):
    outs = []
    for p in range(HG_HEADS // 2):
        cols = slice(p * HG_PAIR, (p + 1) * HG_PAIR)
        st = _bdiag(st_ref[2 * p].astype(BF16), st_ref[2 * p + 1].astype(BF16))
        outs.append(prep["intra"][p] + _dot_nt(prep["qd"][:, cols], st))
        for h in (2 * p, 2 * p + 1):
            st_ref[h] = st_ref[h] * prep["decay"][:, h * HG_DK:(h + 1) * HG_DK] + prep["gain"][h]
    return jnp.concatenate(outs, axis=1)


def _hg_kernel(cqf_ref, cff_ref, cif_ref, cqb_ref, cfb_ref, cib_ref, lb_ref, wf_ref, wb_ref, mf_ref, mb_ref,
               of_ref, ob_ref, st_ref, *, nchunks):
    @pl.when(pl.program_id(1) == 0)
    def _():
        st_ref[...] = jnp.zeros_like(st_ref)

    def prepare(step):
        rf = slice(step * HG_CHUNK, (step + 1) * HG_CHUNK)
        rb = slice((nchunks - 1 - step) * HG_CHUNK, (nchunks - step) * HG_CHUNK)
        fwd = _hg_prepare(False, cff_ref[rf, :].astype(F32), cqf_ref[rf, :].astype(F32), cif_ref[rf, :],
                          lb_ref[0:1, :], wf_ref, mf_ref)
        bwd = _hg_prepare(True, cfb_ref[rb, :].astype(F32), cqb_ref[rb, :].astype(F32), cib_ref[rb, :],
                          lb_ref[1:2, :], wb_ref, mb_ref)
        return rf, fwd, rb, bwd

    def finish(rf, fwd, rb, bwd):
        of_ref[rf, :] = _hg_finish(fwd, st_ref.at[0]).astype(BF16)
        ob_ref[rb, :] = _hg_finish(bwd, st_ref.at[1]).astype(BF16)

    pending = prepare(0)
    for step in range(1, nchunks):
        nxt = prepare(step)
        finish(*pending)
        pending = nxt
    finish(*pending)


def _hg_call(proj, lb, consts_f, consts_b, b, t, row_base, outs):
    tb = min(512, t)
    nt = t // tb
    base = row_base // tb

    def pspec(col, rev):
        return pl.BlockSpec((tb, HG_W), lambda bi, i: (base + bi * nt + (nt - 1 - i if rev else i), col))

    def const(a):
        return pl.BlockSpec(a.shape, lambda bi, i: (0,) * a.ndim)

    (wf, mf), (wb, mb) = consts_f, consts_b
    in_specs = [pspec(3, False), pspec(4, False), pspec(6, False), pspec(3, True), pspec(5, True), pspec(6, True),
                const(lb), const(wf), const(wb), const(mf), const(mb)]
    body, alias_specs, out_shape, aliases = _into(functools.partial(_hg_kernel, nchunks=tb // HG_CHUNK),
                                                  len(in_specs), list(outs))
    return pl.pallas_call(
        body,
        grid=(b, nt),
        in_specs=in_specs + alias_specs,
        out_specs=[pspec(0, False), pspec(0, True)],
        out_shape=out_shape,
        input_output_aliases=aliases,
        scratch_shapes=[pltpu.VMEM((2, HG_HEADS, HG_DK, HG_DK), F32)],
        compiler_params=_cparams("parallel", "arbitrary"),
        name="hgrn",
    )(proj, proj, proj, proj, proj, proj, lb, wf, wb, mf, mb, *outs)


def _outproj_kernel(oa_ref, ob_ref, cf_ref, cb_ref, cg_ref, ng_ref, w_ref, g_ref, *rest, split):
    x_refs, (x1_ref, h_ref, mix) = rest[:-3], rest[-3:]
    tot = cf_ref[...].astype(F32) + cb_ref[...].astype(F32)
    gate = cg_ref[...].astype(F32)
    parts = []
    for h in range(HG_HEADS):
        th = tot[:, h * HG_DK:(h + 1) * HG_DK]
        ms = jnp.mean(th * th, axis=-1, keepdims=True)
        parts.append(th * lax.rsqrt(ms + NORM_EPS) * ng_ref[...])
    oc = (jnp.concatenate(parts, axis=1) * (gate / (1.0 + jnp.exp(-gate)))).astype(BF16)
    mix[:, 0:SEG] = oa_ref[...]
    mix[:, SEG:2 * SEG] = ob_ref[...]
    mix[:, 2 * SEG:] = oc
    x1 = _select_rows(x_refs, split) + _dot(mix[...], w_ref[...])
    x1_ref[...] = x1
    ms = jnp.mean(x1 * x1, axis=-1, keepdims=True)
    h_ref[...] = (x1 * lax.rsqrt(ms + NORM_EPS) * g_ref[...]).astype(BF16)


def _out_proj(oa, ob, ocf, ocb, proj, norm_g, xs, w_out, ln_g, tm):
    n = oa.shape[0]
    row = lambda i: (i, 0)
    const = lambda i: (0, 0)
    x_specs, split = _row_specs(xs, tm)
    return pl.pallas_call(
        functools.partial(_outproj_kernel, split=split),
        grid=(n // tm,),
        in_specs=[pl.BlockSpec((tm, SEG), row), pl.BlockSpec((tm, SEG), row),
                  pl.BlockSpec((tm, HG_W), row), pl.BlockSpec((tm, HG_W), row),
                  pl.BlockSpec((tm, HG_W), lambda i: (i, N_IN // HG_W - 1)),
                  pl.BlockSpec((1, HG_DK), const),
                  pl.BlockSpec((D_MODEL, D_MODEL), const),
                  pl.BlockSpec((1, D_MODEL), const)] + x_specs,
        out_specs=[pl.BlockSpec((tm, D_MODEL), row), pl.BlockSpec((tm, D_MODEL), row)],
        out_shape=[jax.ShapeDtypeStruct((n, D_MODEL), F32), jax.ShapeDtypeStruct((n, D_MODEL), BF16)],
        scratch_shapes=[pltpu.VMEM((tm, D_MODEL), BF16)],
        compiler_params=_cparams("parallel"),
        name="out_proj",
    )(oa, ob, ocf, ocb, proj, norm_g, w_out, ln_g, *xs)


FF_HALO = 16
FF_ROWS = 1024


def _ffn_kernel(x_ref, h_ref, hp_ref, hn_ref, wg_ref, wu_ref, cw_ref, cb_ref, wdn_ref, o_ref, hbuf, abuf, *, tm, row_base,
                starts, ends):
    row0 = row_base + pl.program_id(0) * tm
    at_start = functools.reduce(jnp.logical_or, [row0 == s for s in starts])
    at_end = functools.reduce(jnp.logical_or, [row0 + tm == e for e in ends])
    keep_prev = jnp.where(at_start, 0.0, 1.0)
    keep_next = jnp.where(at_end, 0.0, 1.0)
    hbuf[0:tm] = h_ref[...]
    hbuf[tm:tm + FF_HALO] = hp_ref[...]
    hbuf[tm + FF_HALO:tm + 2 * FF_HALO] = hn_ref[...]
    first = lax.broadcasted_iota(jnp.int32, (8, FF_SEG), 0) == 0
    last = lax.broadcasted_iota(jnp.int32, (8, FF_SEG), 0) == 7
    c_tanh = math.sqrt(2.0 / math.pi)
    nseg = D_FF // FF_SEG

    def up_proj(j):
        w = jnp.concatenate([wg_ref[:, j * FF_SEG:(j + 1) * FF_SEG], wu_ref[:, j * FF_SEG:(j + 1) * FF_SEG]], axis=1)
        return _dot(hbuf[...], w)

    y_next = up_proj(0)
    for j in range(nseg):
        cols = slice(j * FF_SEG, (j + 1) * FF_SEG)
        y = y_next
        if j + 1 < nseg:
            y_next = up_proj(j + 1)
        g, up = y[0:tm, 0:FF_SEG], y[0:tm, FF_SEG:]
        g_prev = y[tm + FF_HALO - 1:tm + FF_HALO, 0:FF_SEG] * keep_prev
        g_next = y[tm + FF_HALO:tm + FF_HALO + 1, 0:FF_SEG] * keep_next
        cw0, cw1, cw2 = cw_ref[0:1, cols], cw_ref[1:2, cols], cw_ref[2:3, cols]
        gate = pltpu.roll(g, 1, 0) * cw0 + g * cw1 + pltpu.roll(g, tm - 1, 0) * cw2 + cb_ref[:, cols]
        top = gate[0:8] + jnp.where(first, (g_prev - g[tm - 1:tm]) * cw0, 0.0)
        bot = gate[tm - 8:tm] + jnp.where(last, (g_next - g[0:1]) * cw2, 0.0)
        gate = jnp.concatenate([top, gate[8:tm - 8], bot], axis=0)
        inner = gate * (c_tanh + (c_tanh * 0.044715) * (gate * gate))
        act = (gate * up) * (1.0 + jnp.tanh(inner))
        abuf[:, cols] = act.astype(BF16)
    o_ref[...] = x_ref[...] + _dot(abuf[...], wdn_ref[...])


def _ffn(x1, h2, w_up, conv_w, conv_b, w_down, tm, starts, ends, row_base=0, nrows=None):
    n = x1.shape[0]
    nrows = n if nrows is None else nrows
    hpb = tm // FF_HALO
    nh = n // FF_HALO
    t0 = row_base // tm
    row = lambda i: (t0 + i, 0)
    const = lambda i: (0, 0)
    once = pl.Buffered(1)
    return pl.pallas_call(
        functools.partial(_ffn_kernel, tm=tm, row_base=row_base, starts=starts, ends=ends),
        grid=(nrows // tm,),
        in_specs=[pl.BlockSpec((tm, D_MODEL), row), pl.BlockSpec((tm, D_MODEL), row),
                  pl.BlockSpec((FF_HALO, D_MODEL), lambda i: (jnp.maximum((t0 + i) * hpb - 1, 0), 0)),
                  pl.BlockSpec((FF_HALO, D_MODEL), lambda i: (jnp.minimum((t0 + i + 1) * hpb, nh - 1), 0)),
                  pl.BlockSpec((D_MODEL, D_FF), lambda i: (0, 0), pipeline_mode=once),
                  pl.BlockSpec((D_MODEL, D_FF), lambda i: (0, 1), pipeline_mode=once),
                  pl.BlockSpec((3, D_FF), const), pl.BlockSpec((1, D_FF), const),
                  pl.BlockSpec((D_FF, D_MODEL), const, pipeline_mode=once)],
        out_specs=pl.BlockSpec((tm, D_MODEL), lambda i: (i, 0)),
        out_shape=jax.ShapeDtypeStruct((nrows, D_MODEL), F32),
        scratch_shapes=[pltpu.VMEM((tm + 2 * FF_HALO, D_MODEL), BF16), pltpu.VMEM((tm, D_FF), BF16)],
        compiler_params=_cparams("parallel"),
        name="ffn",
    )(x1, h2, h2, h2, w_up, w_up, conv_w, conv_b, w_down)


def _lower_bounds(lb_logits):
    p = jax.nn.softmax(lb_logits.astype(F32), axis=1)
    c = jnp.cumsum(p, axis=1)
    return c - c[:, :1]


def _row_tile(groups, tm=512):
    while any(t % tm for _, t, _ in groups):
        tm //= 2
    return tm


def kernel(x_prompt, x_sample, ln_mix_g, w_in, na_q_g, na_k_g, na_rpb, da_q_g, da_k_g, t5_bias, hg_lb_logits,
           hg_norm_g, w_out, ln_ffn_g, w_up, conv_w, conv_b, w_down):
    depth = w_in.shape[0]
    groups = []
    base = 0
    for xg in (x_prompt, x_sample):
        b, t, _ = xg.shape
        groups.append((b, t, base))
        base += b * t
    starts = tuple(rb + i * t for b, t, rb in groups for i in range(b))
    ends = tuple(rb + (i + 1) * t for b, t, rb in groups for i in range(b))
    tm = _row_tile(groups)
    tm_ffn = _row_tile(groups, FF_ROWS)
    n = base
    xs = [x_prompt.reshape(-1, D_MODEL), x_sample.reshape(-1, D_MODEL)]

    lb = _lower_bounds(hg_lb_logits)
    scale = HEAD_DIM ** -0.5
    tile4 = lambda g: jnp.tile(g.astype(F32), NA_HEADS)
    gm = jnp.asarray(np.kron(np.eye(SEG // HEAD_DIM), np.ones((HEAD_DIM, HEAD_DIM))) / HEAD_DIM, BF16)
    hg_fwd_consts = _hg_constants(False)
    hg_bwd_consts = _hg_constants(True)
    da_bias = {}

    def da_bias_for(d, qb):
        if (d, qb) not in da_bias:
            da_bias[(d, qb)] = _da_bias_table(t5_bias, d, qb)
        return da_bias[(d, qb)]

    for l in range(depth):
        qg = jnp.stack([tile4(na_q_g[l]) * (scale * LOG2_E), tile4(na_k_g[l]),
                        tile4(da_q_g[l]) * (scale * LOG2_E), tile4(da_k_g[l])])
        proj, *dil_views = _in_proj(xs, ln_mix_g[l].reshape(1, -1), w_in[l].astype(BF16), qg, gm, tm_ffn)
        na_bias = _na_bias_table(na_rpb[l])
        if l == 0:
            oa = jnp.zeros((n, SEG), BF16)
            ob = jnp.zeros((n, SEG), BF16)
            ocf = jnp.zeros((n, HG_W), BF16)
            ocb = jnp.zeros((n, HG_W), BF16)
        for b, t, rb in groups:
            oa = _na_call(proj, na_bias, b, t, rb, oa)
            prev = []
            for d, view in zip(DA_DILATED, dil_views):
                prev += _da_call(view, 0, da_bias_for(d, min(DA_QB[d], t // d)), b, t, rb, d)
            ob = _da_call(proj, DA_SEGS[0] * SEG // DA_W, da_bias_for(1, min(DA_QB[1], t)), b, t, rb, 1,
                          prev=prev, prev_dils=DA_DILATED, out=ob)
            ocf, ocb = _hg_call(proj, lb[:, l], hg_fwd_consts, hg_bwd_consts, b, t, rb, (ocf, ocb))
        x1, h2 = _out_proj(oa, ob, ocf, ocb, proj, hg_norm_g[l].reshape(1, -1).astype(F32), xs, w_out[l].astype(BF16),
                           ln_ffn_g[l].reshape(1, -1), tm)
        wu, wd = w_up[l].astype(BF16), w_down[l].astype(BF16) * 0.5
        cw, cb = conv_w[l], conv_b[l].reshape(1, -1)
        if l + 1 < depth:
            xs = [_ffn(x1, h2, wu, cw, cb, wd, tm_ffn, starts, ends)]
    return tuple(_ffn(x1, h2, wu, cw, cb, wd, tm_ffn, starts, ends, rb, b * t).reshape(xg.shape)
                 for (b, t, rb), xg in zip(groups, (x_prompt, x_sample)))
```

```python
import functools
import math

import numpy as np
import jax
import jax.numpy as jnp
from jax import lax
from jax.experimental import pallas as pl
from jax.experimental.pallas import tpu as pltpu

F32 = jnp.float32
BF16 = jnp.bfloat16

D_MODEL = 1024
GRID_W = 64
HEAD_DIM = 64
NA_HEADS = 4
NA_WIN_ROWS = 8
NA_WIN_COLS = 16
DA_HEADS = 4
DA_CONFIGS = ((128, 1), (512, 4), (2048, 16))
DA_HALO = 64
N_BUCKETS = 32
MAX_DISTANCE = 1024
HG_HEADS = 4
HG_DK = 128
HG_CHUNK = 64
N_IN = 4096
D_FF = 2816
NORM_EPS = 1e-6
NEG_INF = -1e30
LOG2_E = math.log2(math.e)
SEG = 256
FF_SEG = 256
VMEM_LIMIT = 56 * 1024 * 1024


def _cparams(*sem):
    return pltpu.CompilerParams(dimension_semantics=sem, vmem_limit_bytes=VMEM_LIMIT)


def _into(kernel, n_in, outs):
    def body(*refs):
        return kernel(*refs[:n_in], *refs[n_in + len(outs):])
    specs = [pl.BlockSpec(memory_space=pl.ANY)] * len(outs)
    shapes = [jax.ShapeDtypeStruct(o.shape, o.dtype) for o in outs]
    return body, specs, shapes, {n_in + k: k for k in range(len(outs))}


def _dot(a, b):
    return jnp.dot(a, b, preferred_element_type=F32)


def _dot_nt(a, b):
    return lax.dot_general(a, b, (((1,), (1,)), ((), ())), preferred_element_type=F32)


def _dot_tn(a, b):
    return lax.dot_general(a, b, (((0,), (0,)), ((), ())), preferred_element_type=F32)


DA_SEGS = (3, 4, 5)
DA_W = len(DA_SEGS) * SEG
DA_DILATED = tuple(d for _, d in DA_CONFIGS if d > 1)


def _select_rows(x_refs, split):
    if len(x_refs) == 1:
        return x_refs[0][...]
    return jnp.where(pl.program_id(0) < split, x_refs[0][...], x_refs[1][...])


def _row_specs(xs, tm):
    if len(xs) == 1:
        return [pl.BlockSpec((tm, xs[0].shape[1]), lambda i: (i, 0))], 0
    split = xs[0].shape[0] // tm
    return [pl.BlockSpec((tm, xs[0].shape[1]), lambda i: (jnp.minimum(i, split - 1), 0)),
            pl.BlockSpec((tm, xs[1].shape[1]), lambda i: (jnp.maximum(i - split, 0), 0))], split


def _inproj_kernel(*refs, tm, nx, split):
    x_refs, (g_ref, w_ref, qg_ref, gm_ref, o_ref), rest = refs[:nx], refs[nx:nx + 5], refs[nx + 5:]
    dil_refs, ybuf, hbuf = rest[:-2], rest[-2], rest[-1]
    x = _select_rows(x_refs, split)
    ms = jnp.mean(x * x, axis=-1, keepdims=True)
    hbuf[...] = (x * lax.rsqrt(ms + NORM_EPS) * g_ref[...]).astype(BF16)
    normed = {0: 0, 1: 1, 3: 2, 4: 3}
    nseg = N_IN // SEG
    y_next = _dot(hbuf[...], w_ref[:, 0:SEG])
    for seg in range(nseg):
        y = y_next
        if seg + 1 < nseg:
            y_next = _dot(hbuf[...], w_ref[:, (seg + 1) * SEG:(seg + 2) * SEG])
        if seg in normed:
            ss = _dot((y * y).astype(BF16), gm_ref[...])
            r = normed[seg]
            y = y * lax.rsqrt(ss + NORM_EPS) * qg_ref[r:r + 1, :]
        if seg in DA_SEGS:
            k = DA_SEGS.index(seg)
            for j in range(SEG // 128):
                ybuf[k * (SEG // 128) + j] = y[:, j * 128:(j + 1) * 128]
        o_ref[:, seg * SEG:(seg + 1) * SEG] = y.astype(BF16)
        if seg == DA_SEGS[-1]:
            for d, ref in zip(DA_DILATED, dil_refs):
                for rho in range(d):
                    for j in range(DA_W // 128):
                        ref[:, rho * DA_W + j * 128:rho * DA_W + (j + 1) * 128] = \
                            ybuf[j, pl.ds(rho, tm // d, stride=d), :].astype(BF16)


def _in_proj(xs, ln_g, w_in, qg, gm, tm):
    n = sum(x.shape[0] for x in xs)
    row = lambda i: (i, 0)
    const = lambda i: (0, 0)
    x_specs, split = _row_specs(xs, tm)
    return pl.pallas_call(
        functools.partial(_inproj_kernel, tm=tm, nx=len(xs), split=split),
        grid=(n // tm,),
        in_specs=x_specs + [
            pl.BlockSpec((1, D_MODEL), const),
            pl.BlockSpec((D_MODEL, N_IN), const, pipeline_mode=pl.Buffered(1)),
            pl.BlockSpec((4, SEG), const),
            pl.BlockSpec((SEG, SEG), const),
        ],
        out_specs=[pl.BlockSpec((tm, N_IN), row)] + [pl.BlockSpec((tm // d, d * DA_W), row) for d in DA_DILATED],
        out_shape=[jax.ShapeDtypeStruct((n, N_IN), BF16)]
        + [jax.ShapeDtypeStruct((n // d, d * DA_W), BF16) for d in DA_DILATED],
        scratch_shapes=[pltpu.VMEM((DA_W // 128, tm, 128), F32),
                        pltpu.VMEM((tm, D_MODEL), BF16)],
        compiler_params=_cparams("parallel"),
        name="in_proj",
    )(*xs, ln_g, w_in, qg, gm)


NA_RB = 8
NA_TOK = NA_RB * GRID_W
NA_GR = 4
NA_GQ = NA_GR * GRID_W
NA_KR = NA_GR + NA_WIN_ROWS
NA_GK = NA_KR * GRID_W
NA_NDR = 2 * NA_WIN_ROWS - 1


def _na_bias_table(rpb):
    cq = np.arange(GRID_W)
    ck = np.arange(GRID_W)
    c0 = np.clip(cq - NA_WIN_COLS // 2, 0, GRID_W - NA_WIN_COLS)
    col_ok = (ck[None, :] >= c0[:, None]) & (ck[None, :] < c0[:, None] + NA_WIN_COLS)
    dc = np.clip(ck[None, :] - cq[:, None], -(NA_WIN_COLS - 1), NA_WIN_COLS - 1) + NA_WIN_COLS - 1
    col_onehot = (dc[:, :, None] == np.arange(2 * NA_WIN_COLS - 1)).astype(np.float32)
    jq = np.arange(NA_GR)
    half = NA_WIN_ROWS // 2
    dq = np.stack([jq, half + jq, NA_WIN_ROWS + jq])
    dw0 = np.stack([0 * jq, jq, half + 0 * jq])
    w = np.arange(NA_KR)
    row_ok = (w[None, None, :] >= dw0[:, :, None]) & (w[None, None, :] < dw0[:, :, None] + NA_WIN_ROWS)
    dr = w[None, None, :] - dq[:, :, None] + NA_WIN_ROWS - 1
    row_onehot = ((dr[..., None] == np.arange(NA_NDR)) & row_ok[..., None]).astype(np.float32)
    t = jnp.einsum('hab,qkb->haqk', rpb.astype(F32), col_onehot, precision=lax.Precision.HIGHEST)
    bias = jnp.einsum('cjwa,haqk->chjqwk', row_onehot, t, precision=lax.Precision.HIGHEST)
    ok = row_ok[:, None, :, None, :, None] & col_ok[None, None, None, :, None, :]
    return jnp.where(ok, bias * LOG2_E, NEG_INF).reshape(3, NA_HEADS, NA_GQ, NA_GK)


def _na_kernel(q_ref, kp_ref, kc_ref, kn_ref, vp_ref, vc_ref, vn_ref, bias0_ref, bias1_ref, o_ref, kbuf, vbuf, *,
               rows):
    i = pl.program_id(1)
    kbuf[0:NA_TOK] = kp_ref[...]
    kbuf[NA_TOK:2 * NA_TOK] = kc_ref[...]
    kbuf[2 * NA_TOK:3 * NA_TOK] = kn_ref[...]
    vbuf[0:NA_TOK] = vp_ref[...]
    vbuf[NA_TOK:2 * NA_TOK] = vc_ref[...]
    vbuf[2 * NA_TOK:3 * NA_TOK] = vn_ref[...]
    low = lax.broadcasted_iota(jnp.int32, (NA_GQ, 128), 1) < HEAD_DIM
    bias_refs = (bias0_ref, bias1_ref)

    def scores(g, a):
        r = i * NA_RB + g * NA_GR
        kstart = jnp.clip(r - NA_WIN_ROWS // 2, 0, rows - NA_KR)
        off = pl.multiple_of((kstart - i * NA_RB + NA_RB) * GRID_W, GRID_W)
        cols = slice(a * 128, (a + 1) * 128)
        qa = q_ref[g * NA_GQ:(g + 1) * NA_GQ, cols]
        q2 = jnp.concatenate([jnp.where(low, qa, jnp.zeros_like(qa)), jnp.where(low, jnp.zeros_like(qa), qa)], axis=0)
        bias = jnp.concatenate([bias_refs[g][0, 2 * a], bias_refs[g][0, 2 * a + 1]], axis=0)
        return _dot_nt(q2, kbuf[pl.ds(off, NA_GK), cols]) + bias, vbuf[pl.ds(off, NA_GK), cols]

    items = [(g, a) for g in range(NA_RB // NA_GR) for a in range(2)]
    nxt = scores(*items[0])
    for n, (g, a) in enumerate(items):
        s, va = nxt
        if n + 1 < len(items):
            nxt = scores(*items[n + 1])
        m = jnp.max(s, axis=-1, keepdims=True)
        p = jnp.exp2(s - m)
        l = jnp.sum(p, axis=-1, keepdims=True)
        pv = _dot(p.astype(BF16), va) / l
        o_ref[g * NA_GQ:(g + 1) * NA_GQ, a * 128:(a + 1) * 128] = \
            jnp.where(low, pv[0:NA_GQ], pv[NA_GQ:]).astype(BF16)


def _na_call(proj, bias, b, t, row_base, out):
    rows = t // GRID_W
    nrb = rows // NA_RB
    base = row_base // NA_TOK
    assert rows >= NA_KR and rows % NA_RB == 0

    def spec(col, shift):
        def imap(bi, i):
            return (base + bi * nrb + jnp.clip(i + shift, 0, nrb - 1), col)
        return pl.BlockSpec((NA_TOK, SEG), imap)

    bias_block = (1, NA_HEADS, NA_GQ, NA_GK)
    bias0 = pl.BlockSpec(bias_block, lambda bi, i: (jnp.where(i == 0, 0, 1), 0, 0, 0))
    bias1 = pl.BlockSpec(bias_block, lambda bi, i: (jnp.where(i == nrb - 1, 2, 1), 0, 0, 0))
    in_specs = [spec(0, 0), spec(1, -1), spec(1, 0), spec(1, 1), spec(2, -1), spec(2, 0), spec(2, 1), bias0, bias1]
    body, alias_specs, out_shape, aliases = _into(functools.partial(_na_kernel, rows=rows), len(in_specs), [out])
    return pl.pallas_call(
        body,
        grid=(b, nrb),
        in_specs=in_specs + alias_specs,
        out_specs=[pl.BlockSpec((NA_TOK, SEG), lambda bi, i: (base + bi * nrb + i, 0))],
        out_shape=out_shape,
        input_output_aliases=aliases,
        scratch_shapes=[pltpu.VMEM((3 * NA_TOK, SEG), BF16), pltpu.VMEM((3 * NA_TOK, SEG), BF16)],
        compiler_params=_cparams("parallel", "parallel"),
        name="na",
    )(proj, proj, proj, proj, proj, proj, proj, bias, bias, out)[0]


def _t5_bucket(rel):
    nb = N_BUCKETS // 2
    max_exact = nb // 2
    ret = np.where(rel > 0, nb, 0)
    n = np.abs(rel)
    large = max_exact + (np.log(np.maximum(n, 1) / max_exact)
                         / np.log(MAX_DISTANCE / max_exact) * (nb - max_exact)).astype(np.int32)
    large = np.minimum(large, nb - 1)
    return (ret + np.where(n < max_exact, n, large)).astype(np.int32)


def _da_bias_table(t5_bias, dilation, qb):
    kb = qb + 2 * DA_HALO
    rel = np.arange(kb)[None, :] - np.arange(qb)[:, None] - DA_HALO
    ok = np.abs(rel) <= DA_HALO
    onehot = (_t5_bucket(rel * dilation)[:, :, None] == np.arange(N_BUCKETS)).astype(np.float32)
    bias = jnp.einsum('qkb,bh->hqk', onehot, t5_bias.astype(F32), precision=lax.Precision.HIGHEST)
    j = np.arange(kb)
    edge = np.stack([np.ones(kb, bool), j >= DA_HALO, j < qb + DA_HALO, (j >= DA_HALO) & (j < qb + DA_HALO)])
    return jnp.where(ok[None, None] & edge[:, None, None, :], (bias * LOG2_E)[None], NEG_INF)


def _da_kernel(*refs, qb, nb, d, length, prev_dils):
    final = bool(prev_dils)
    (main_ref, left_ref, right_ref, bias_ref), refs = refs[:4], refs[4:]
    if final:
        npv = 2 * len(prev_dils)
        prev_refs, o_ref, buf, und = refs[:npv], refs[npv], refs[npv + 1], refs[npv + 2:]
    else:
        o_ref, lse_ref, buf = refs
    c = pl.program_id(1)
    tr = qb * nb
    kb = qb + 2 * DA_HALO
    buf[0:DA_HALO] = left_ref[...]
    buf[DA_HALO:DA_HALO + tr] = main_ref[...]
    buf[DA_HALO + tr:tr + 2 * DA_HALO] = right_ref[...]
    if final:
        for k, pd in enumerate(prev_dils):
            for part in range(2):
                src, dst = prev_refs[2 * k + part], und[2 * k + part]
                for rho in range(pd):
                    for j in range(SEG // 128):
                        dst[j, pl.ds(rho, tr // pd, stride=pd), :] = \
                            src[:, rho * SEG + j * 128:rho * SEG + (j + 1) * 128]
    low = lax.broadcasted_iota(jnp.int32, (qb, 128), 1) < HEAD_DIM
    nblk = length // qb

    def scores(blk, rho, a):
        krows = slice(blk * qb, blk * qb + kb)
        gblk = c * nb + blk
        var = jnp.where(gblk == 0, 1, 0) + jnp.where(gblk == nblk - 1, 2, 0)
        base = rho * DA_W + a * 128
        qa = main_ref[blk * qb:(blk + 1) * qb, base:base + 128]
        q2 = jnp.concatenate([jnp.where(low, qa, jnp.zeros_like(qa)), jnp.where(low, jnp.zeros_like(qa), qa)], axis=0)
        bias = jnp.concatenate([bias_ref[var, 2 * a], bias_ref[var, 2 * a + 1]], axis=0)
        s = _dot_nt(q2, buf[krows, base + SEG:base + SEG + 128]) + bias
        return s, buf[krows, base + 2 * SEG:base + 2 * SEG + 128]

    items = [(blk, rho, a) for blk in range(nb) for rho in range(d) for a in range(2)]
    nxt = scores(*items[0])
    for n, (blk, rho, a) in enumerate(items):
        s, va = nxt
        if n + 1 < len(items):
            nxt = scores(*items[n + 1])
        rows = slice(blk * qb, (blk + 1) * qb)
        m = jnp.max(s, axis=-1, keepdims=True)
        p = jnp.exp2(s - m)
        l = jnp.sum(p, axis=-1, keepdims=True)
        pv = _dot(p.astype(BF16), va) / l
        lse2 = jnp.broadcast_to(m + jnp.log2(l), (2 * qb, 128))
        o = jnp.where(low, pv[0:qb], pv[qb:])
        lse = jnp.where(low, lse2[0:qb], lse2[qb:])
        cols = slice(rho * SEG + a * 128, rho * SEG + (a + 1) * 128)
        if final:
            mx = lse
            for k in range(len(prev_dils)):
                mx = jnp.maximum(mx, und[2 * k + 1][a, rows, :])
            wsum = jnp.exp2(lse - mx)
            mix = wsum * o
            for k in range(len(prev_dils)):
                wk = jnp.exp2(und[2 * k + 1][a, rows, :] - mx)
                wsum = wsum + wk
                mix = mix + wk * und[2 * k][a, rows, :]
            o_ref[rows, cols] = (mix / wsum).astype(BF16)
        else:
            o_ref[rows, cols] = o
            lse_ref[rows, cols] = lse


DA_QB = {1: 128, 4: 128, 16: 128}
DA_BLOCKS = {1: 8, 4: 4, 16: 1}


def _da_call(view, colblk, bias, b, t, row_base, dilation, prev=None, prev_dils=(), out=None):
    d = dilation
    length = t // d
    qb = min(DA_QB[d], length)
    nb = min(DA_BLOCKS[d], length // qb)
    tr = qb * nb
    nt = length // tr
    kb = qb + 2 * DA_HALO
    ubase = row_base // d
    hpt = tr // DA_HALO
    nhalo = view.shape[0] // DA_HALO
    final = prev is not None

    def halo(right):
        def imap(bi, c):
            blk = (ubase + bi * length) // DA_HALO + ((c + 1) * hpt if right else c * hpt - 1)
            return (jnp.clip(blk, 0, nhalo - 1), colblk)
        return pl.BlockSpec((DA_HALO, d * DA_W), imap)

    in_specs = [pl.BlockSpec((tr, d * DA_W), lambda bi, c: ((ubase + bi * length) // tr + c, colblk)),
                halo(False), halo(True),
                pl.BlockSpec((4, DA_HEADS, qb, kb), lambda bi, c: (0, 0, 0, 0))]
    args = [view] * 3 + [bias]
    out_block = pl.BlockSpec((tr, d * SEG), lambda bi, c: (bi * nt + c, 0))
    scratch = [pltpu.VMEM((tr + 2 * DA_HALO, d * DA_W), BF16)]
    if final:
        assert d == 1
        for pd in prev_dils:
            in_specs += [pl.BlockSpec((tr // pd, pd * SEG), lambda bi, c: (bi * nt + c, 0))] * 2
            scratch += [pltpu.VMEM((SEG // 128, tr, 128), F32)] * 2
        args += list(prev)
        kern = functools.partial(_da_kernel, qb=qb, nb=nb, d=d, length=length, prev_dils=tuple(prev_dils))
        body, alias_specs, out_shape, aliases = _into(kern, len(in_specs), [out])
        return pl.pallas_call(
            body,
            grid=(b, nt),
            in_specs=in_specs + alias_specs,
            out_specs=[pl.BlockSpec((tr, SEG), lambda bi, c: ((ubase + bi * length) // tr + c, 0))],
            out_shape=out_shape,
            input_output_aliases=aliases,
            scratch_shapes=scratch,
            compiler_params=_cparams("parallel", "parallel"),
            name="da_d1",
        )(*args, out)[0]
    return pl.pallas_call(
        functools.partial(_da_kernel, qb=qb, nb=nb, d=d, length=length, prev_dils=()),
        grid=(b, nt),
        in_specs=in_specs,
        out_specs=[out_block, out_block],
        out_shape=[jax.ShapeDtypeStruct((b * t // d, d * SEG), F32)] * 2,
        scratch_shapes=scratch,
        compiler_params=_cparams("parallel", "parallel"),
        name="da_d%d" % d,
    )(*args)


HG_LEVELS = (32, 16, 8, 4, 2, 1)
HG_NSEC = 2 + len(HG_LEVELS)
HG_W = HG_HEADS * HG_DK
HG_PAIR = 2 * HG_DK


def _hg_constants(rev):
    c = HG_CHUNK
    t = np.arange(c)[:, None]
    j = np.arange(c)[None, :]
    secs = []
    if not rev:
        secs.append(j <= t)
        secs.append(j > t)
    else:
        secs.append(j >= t)
        secs.append(j < t)
    masks = [np.eye(c, dtype=bool)]
    s = np.arange(c)[None, :]
    for m in HG_LEVELS:
        p0 = (t // (2 * m)) * (2 * m)
        upper = (t & m) != 0
        if not rev:
            mid = p0 + m - 1
            sec = np.where(upper, (j > mid) & (j <= t), (j > t) & (j <= mid))
            pair = ((t // (2 * m)) == (s // (2 * m))) & upper & ((s & m) == 0)
        else:
            mid = p0 + m
            sec = np.where(upper, (j >= mid) & (j < t), (j >= t) & (j < mid))
            pair = ((t // (2 * m)) == (s // (2 * m))) & (~upper) & ((s & m) != 0)
        secs.append(sec)
        masks.append(pair)
    w = np.tile(np.concatenate(secs, axis=0).astype(np.float32), (1, 2))
    masks = np.tile(np.stack(masks).astype(np.float32), (1, 1, 2))
    return jnp.asarray(w, BF16), jnp.asarray(masks)


def _bdiag(h0, h1):
    zero = jnp.zeros_like(h0)
    return jnp.concatenate([jnp.concatenate([h0, zero], axis=1), jnp.concatenate([zero, h1], axis=1)], axis=0)


def _hg_pair_operand(a, p):
    return _bdiag(a[:, (2 * p) * HG_DK:(2 * p + 1) * HG_DK], a[:, (2 * p + 1) * HG_DK:(2 * p + 2) * HG_DK])


def _hg_prepare(rev, z, x, v, lb, w_ref, mask_ref):
    c_sz = HG_CHUNK
    z2 = z * LOG2_E
    log2_lb = jnp.log2(lb)
    log2_sig = jnp.minimum(z2, 0.0) - jnp.log2(1.0 + jnp.exp2(-jnp.abs(z2)))
    cc = jnp.log2(1.0 - lb) + log2_sig
    log2f = jnp.maximum(log2_lb, cc) + jnp.log2(1.0 + jnp.exp2(-jnp.abs(log2_lb - cc)))
    kk = (1.0 - lb) * jnp.exp2(log2_sig - z2)
    q = x / (1.0 + jnp.exp2(x * -LOG2_E))
    hi = log2f.astype(BF16)
    lo = (log2f - hi.astype(F32)).astype(BF16)
    dec = jnp.exp2(_dot(w_ref[...], jnp.concatenate([hi, lo], axis=0)))
    sub = lax.broadcasted_iota(jnp.int32, (1, 8, HG_W), 1)
    ys = []
    for lvl, m in enumerate(HG_LEVELS):
        if m >= 8:
            sel = jnp.concatenate(
                [(q if (((8 * r) & m) != 0) != rev else kk)[8 * r:8 * r + 8] for r in range(c_sz // 8)], axis=0)
        else:
            is_q = ((sub & m) == 0) if rev else ((sub & m) != 0)
            sel = jnp.where(is_q, q.reshape(c_sz // 8, 8, HG_W), kk.reshape(c_sz // 8, 8, HG_W)).reshape(c_sz, HG_W)
        ys.append((sel * dec[(2 + lvl) * c_sz:(3 + lvl) * c_sz]).astype(BF16))
    q16, k16 = q.astype(BF16), kk.astype(BF16)
    kd = (kk * dec[c_sz:2 * c_sz]).astype(BF16)
    intra = []
    for p in range(HG_HEADS // 2):
        cols = slice(p * HG_PAIR, (p + 1) * HG_PAIR)
        a2 = mask_ref[0] * _dot_nt(q16[:, cols], _hg_pair_operand(k16, p))
        for lvl in range(len(HG_LEVELS)):
            a2 = a2 + mask_ref[1 + lvl] * _dot_nt(ys[lvl][:, cols], _hg_pair_operand(ys[lvl], p))
        intra.append(_dot(a2.astype(BF16), _hg_pair_operand(v, p)))
    gain = [_dot_tn(v[:, h * HG_DK:(h + 1) * HG_DK], kd[:, h * HG_DK:(h + 1) * HG_DK]) for h in range(HG_HEADS)]
    edge = 0 if rev else c_sz - 1
    return dict(intra=intra, gain=gain, qd=(q * dec[0:c_sz]).astype(BF16), decay=dec[edge:edge + 1])


def _hg_finish(prep, st_ref):
    outs = []
    for p in range(HG_HEADS // 2):
        cols = slice(p * HG_PAIR, (p + 1) * HG_PAIR)
        st = _bdiag(st_ref[2 * p].astype(BF16), st_ref[2 * p + 1].astype(BF16))
        outs.append(prep["intra"][p] + _dot_nt(prep["qd"][:, cols], st))
        for h in (2 * p, 2 * p + 1):
            st_ref[h] = st_ref[h] * prep["decay"][:, h * HG_DK:(h + 1) * HG_DK] + prep["gain"][h]
    return jnp.concatenate(outs, axis=1)


def _hg_kernel(cqf_ref, cff_ref, cif_ref, cqb_ref, cfb_ref, cib_ref, lb_ref, wf_ref, wb_ref, mf_ref, mb_ref,
               of_ref, ob_ref, st_ref, *, nchunks):
    @pl.when(pl.program_id(1) == 0)
    def _():
        st_ref[...] = jnp.zeros_like(st_ref)

    def prepare(step):
        rf = slice(step * HG_CHUNK, (step + 1) * HG_CHUNK)
        rb = slice((nchunks - 1 - step) * HG_CHUNK, (nchunks - step) * HG_CHUNK)
        fwd = _hg_prepare(False, cff_ref[rf, :].astype(F32), cqf_ref[rf, :].astype(F32), cif_ref[rf, :],
                          lb_ref[0:1, :], wf_ref, mf_ref)
        bwd = _hg_prepare(True, cfb_ref[rb, :].astype(F32), cqb_ref[rb, :].astype(F32), cib_ref[rb, :],
                          lb_ref[1:2, :], wb_ref, mb_ref)
        return rf, fwd, rb, bwd

    def finish(rf, fwd, rb, bwd):
        of_ref[rf, :] = _hg_finish(fwd, st_ref.at[0]).astype(BF16)
        ob_ref[rb, :] = _hg_finish(bwd, st_ref.at[1]).astype(BF16)

    pending = prepare(0)
    for step in range(1, nchunks):
        nxt = prepare(step)
        finish(*pending)
        pending = nxt
    finish(*pending)


def _hg_call(proj, lb, consts_f, consts_b, b, t, row_base, outs):
    tb = min(512, t)
    nt = t // tb
    base = row_base // tb

    def pspec(col, rev):
        return pl.BlockSpec((tb, HG_W), lambda bi, i: (base + bi * nt + (nt - 1 - i if rev else i), col))

    def const(a):
        return pl.BlockSpec(a.shape, lambda bi, i: (0,) * a.ndim)

    (wf, mf), (wb, mb) = consts_f, consts_b
    in_specs = [pspec(3, False), pspec(4, False), pspec(6, False), pspec(3, True), pspec(5, True), pspec(6, True),
                const(lb), const(wf), const(wb), const(mf), const(mb)]
    body, alias_specs, out_shape, aliases = _into(functools.partial(_hg_kernel, nchunks=tb // HG_CHUNK),
                                                  len(in_specs), list(outs))
    return pl.pallas_call(
        body,
        grid=(b, nt),
        in_specs=in_specs + alias_specs,
        out_specs=[pspec(0, False), pspec(0, True)],
        out_shape=out_shape,
        input_output_aliases=aliases,
        scratch_shapes=[pltpu.VMEM((2, HG_HEADS, HG_DK, HG_DK), F32)],
        compiler_params=_cparams("parallel", "arbitrary"),
        name="hgrn",
    )(proj, proj, proj, proj, proj, proj, lb, wf, wb, mf, mb, *outs)


def _outproj_kernel(oa_ref, ob_ref, cf_ref, cb_ref, cg_ref, ng_ref, w_ref, *rest, split):
    x_refs, (x1_ref, mix) = rest[:-2], rest[-2:]
    tot = cf_ref[...].astype(F32) + cb_ref[...].astype(F32)
    gate = cg_ref[...].astype(F32)
    parts = []
    for h in range(HG_HEADS):
        th = tot[:, h * HG_DK:(h + 1) * HG_DK]
        ms = jnp.mean(th * th, axis=-1, keepdims=True)
        parts.append(th * lax.rsqrt(ms + NORM_EPS) * ng_ref[...])
    oc = (jnp.concatenate(parts, axis=1) * (gate / (1.0 + jnp.exp(-gate)))).astype(BF16)
    mix[:, 0:SEG] = oa_ref[...]
    mix[:, SEG:2 * SEG] = ob_ref[...]
    mix[:, 2 * SEG:] = oc
    x1_ref[...] = _select_rows(x_refs, split) + _dot(mix[...], w_ref[...])


def _out_proj(oa, ob, ocf, ocb, proj, norm_g, xs, w_out, tm):
    n = oa.shape[0]
    row = lambda i: (i, 0)
    const = lambda i: (0, 0)
    x_specs, split = _row_specs(xs, tm)
    return pl.pallas_call(
        functools.partial(_outproj_kernel, split=split),
        grid=(n // tm,),
        in_specs=[pl.BlockSpec((tm, SEG), row), pl.BlockSpec((tm, SEG), row),
                  pl.BlockSpec((tm, HG_W), row), pl.BlockSpec((tm, HG_W), row),
                  pl.BlockSpec((tm, HG_W), lambda i: (i, N_IN // HG_W - 1)),
                  pl.BlockSpec((1, HG_DK), const),
                  pl.BlockSpec((D_MODEL, D_MODEL), const)] + x_specs,
        out_specs=pl.BlockSpec((tm, D_MODEL), row),
        out_shape=jax.ShapeDtypeStruct((n, D_MODEL), F32),
        scratch_shapes=[pltpu.VMEM((tm, D_MODEL), BF16)],
        compiler_params=_cparams("parallel"),
        name="out_proj",
    )(oa, ob, ocf, ocb, proj, norm_g, w_out, *xs)


FF_HALO = 16
FF_ROWS = 1024


def _ffn_kernel(x_ref, xp_ref, xn_ref, g_ref, wg_ref, wu_ref, cw_ref, cb_ref, wdn_ref, o_ref, hbuf, abuf, *, tm, row_base,
                starts, ends):
    row0 = row_base + pl.program_id(0) * tm
    at_start = functools.reduce(jnp.logical_or, [row0 == s for s in starts])
    at_end = functools.reduce(jnp.logical_or, [row0 + tm == e for e in ends])
    keep_prev = jnp.where(at_start, 0.0, 1.0)
    keep_next = jnp.where(at_end, 0.0, 1.0)
    def normed(ref):
        x = ref[...]
        ms = jnp.mean(x * x, axis=-1, keepdims=True)
        return (x * lax.rsqrt(ms + NORM_EPS) * g_ref[...]).astype(BF16)

    hbuf[0:tm] = normed(x_ref)
    hbuf[tm:tm + FF_HALO] = normed(xp_ref)
    hbuf[tm + FF_HALO:tm + 2 * FF_HALO] = normed(xn_ref)
    first = lax.broadcasted_iota(jnp.int32, (8, FF_SEG), 0) == 0
    last = lax.broadcasted_iota(jnp.int32, (8, FF_SEG), 0) == 7
    c_tanh = math.sqrt(2.0 / math.pi)
    nseg = D_FF // FF_SEG

    def up_proj(j):
        w = jnp.concatenate([wg_ref[:, j * FF_SEG:(j + 1) * FF_SEG], wu_ref[:, j * FF_SEG:(j + 1) * FF_SEG]], axis=1)
        return _dot(hbuf[...], w)

    y_next = up_proj(0)
    for j in range(nseg):
        cols = slice(j * FF_SEG, (j + 1) * FF_SEG)
        y = y_next
        if j + 1 < nseg:
            y_next = up_proj(j + 1)
        g, up = y[0:tm, 0:FF_SEG], y[0:tm, FF_SEG:]
        g_prev = y[tm + FF_HALO - 1:tm + FF_HALO, 0:FF_SEG] * keep_prev
        g_next = y[tm + FF_HALO:tm + FF_HALO + 1, 0:FF_SEG] * keep_next
        cw0, cw1, cw2 = cw_ref[0:1, cols], cw_ref[1:2, cols], cw_ref[2:3, cols]
        gate = pltpu.roll(g, 1, 0) * cw0 + g * cw1 + pltpu.roll(g, tm - 1, 0) * cw2 + cb_ref[:, cols]
        top = gate[0:8] + jnp.where(first, (g_prev - g[tm - 1:tm]) * cw0, 0.0)
        bot = gate[tm - 8:tm] + jnp.where(last, (g_next - g[0:1]) * cw2, 0.0)
        gate = jnp.concatenate([top, gate[8:tm - 8], bot], axis=0)
        inner = gate * (c_tanh + (c_tanh * 0.044715) * (gate * gate))
        act = (gate * up) * (1.0 + jnp.tanh(inner))
        abuf[:, cols] = act.astype(BF16)
    o_ref[...] = x_ref[...] + _dot(abuf[...], wdn_ref[...])


def _ffn(x1, ln_g, w_up, conv_w, conv_b, w_down, tm, starts, ends, row_base=0, nrows=None):
    n = x1.shape[0]
    nrows = n if nrows is None else nrows
    hpb = tm // FF_HALO
    nh = n // FF_HALO
    t0 = row_base // tm
    row = lambda i: (t0 + i, 0)
    const = lambda i: (0, 0)
    once = pl.Buffered(1)
    return pl.pallas_call(
        functools.partial(_ffn_kernel, tm=tm, row_base=row_base, starts=starts, ends=ends),
        grid=(nrows // tm,),
        in_specs=[pl.BlockSpec((tm, D_MODEL), row),
                  pl.BlockSpec((FF_HALO, D_MODEL), lambda i: (jnp.maximum((t0 + i) * hpb - 1, 0), 0)),
                  pl.BlockSpec((FF_HALO, D_MODEL), lambda i: (jnp.minimum((t0 + i + 1) * hpb, nh - 1), 0)),
                  pl.BlockSpec((1, D_MODEL), const),
                  pl.BlockSpec((D_MODEL, D_FF), lambda i: (0, 0), pipeline_mode=once),
                  pl.BlockSpec((D_MODEL, D_FF), lambda i: (0, 1), pipeline_mode=once),
                  pl.BlockSpec((3, D_FF), const), pl.BlockSpec((1, D_FF), const),
                  pl.BlockSpec((D_FF, D_MODEL), const, pipeline_mode=once)],
        out_specs=pl.BlockSpec((tm, D_MODEL), lambda i: (i, 0)),
        out_shape=jax.ShapeDtypeStruct((nrows, D_MODEL), F32),
        scratch_shapes=[pltpu.VMEM((tm + 2 * FF_HALO, D_MODEL), BF16), pltpu.VMEM((tm, D_FF), BF16)],
        compiler_params=_cparams("parallel"),
        name="ffn",
    )(x1, x1, x1, ln_g, w_up, w_up, conv_w, conv_b, w_down)


def _lower_bounds(lb_logits):
    p = jax.nn.softmax(lb_logits.astype(F32), axis=1)
    c = jnp.cumsum(p, axis=1)
    return c - c[:, :1]


def _row_tile(groups, tm=512):
    while any(t % tm for _, t, _ in groups):
        tm //= 2
    return tm


def kernel(x_prompt, x_sample, ln_mix_g, w_in, na_q_g, na_k_g, na_rpb, da_q_g, da_k_g, t5_bias, hg_lb_logits,
           hg_norm_g, w_out, ln_ffn_g, w_up, conv_w, conv_b, w_down):
    depth = w_in.shape[0]
    groups = []
    base = 0
    for xg in (x_prompt, x_sample):
        b, t, _ = xg.shape
        groups.append((b, t, base))
        base += b * t
    starts = tuple(rb + i * t for b, t, rb in groups for i in range(b))
    ends = tuple(rb + (i + 1) * t for b, t, rb in groups for i in range(b))
    tm = _row_tile(groups)
    tm_ffn = _row_tile(groups, FF_ROWS)
    n = base
    xs = [x_prompt.reshape(-1, D_MODEL), x_sample.reshape(-1, D_MODEL)]

    lb = _lower_bounds(hg_lb_logits)
    scale = HEAD_DIM ** -0.5
    tile4 = lambda g: jnp.tile(g.astype(F32), NA_HEADS)
    gm = jnp.asarray(np.kron(np.eye(SEG // HEAD_DIM), np.ones((HEAD_DIM, HEAD_DIM))) / HEAD_DIM, BF16)
    hg_fwd_consts = _hg_constants(False)
    hg_bwd_consts = _hg_constants(True)
    da_bias = {}

    def da_bias_for(d, qb):
        if (d, qb) not in da_bias:
            da_bias[(d, qb)] = _da_bias_table(t5_bias, d, qb)
        return da_bias[(d, qb)]

    for l in range(depth):
        qg = jnp.stack([tile4(na_q_g[l]) * (scale * LOG2_E), tile4(na_k_g[l]),
                        tile4(da_q_g[l]) * (scale * LOG2_E), tile4(da_k_g[l])])
        proj, *dil_views = _in_proj(xs, ln_mix_g[l].reshape(1, -1), w_in[l].astype(BF16), qg, gm, tm_ffn)
        na_bias = _na_bias_table(na_rpb[l])
        if l == 0:
            oa = jnp.zeros((n, SEG), BF16)
            ob = jnp.zeros((n, SEG), BF16)
            ocf = jnp.zeros((n, HG_W), BF16)
            ocb = jnp.zeros((n, HG_W), BF16)
        for b, t, rb in groups:
            oa = _na_call(proj, na_bias, b, t, rb, oa)
            prev = []
            for d, view in zip(DA_DILATED, dil_views):
                prev += _da_call(view, 0, da_bias_for(d, min(DA_QB[d], t // d)), b, t, rb, d)
            ob = _da_call(proj, DA_SEGS[0] * SEG // DA_W, da_bias_for(1, min(DA_QB[1], t)), b, t, rb, 1,
                          prev=prev, prev_dils=DA_DILATED, out=ob)
            ocf, ocb = _hg_call(proj, lb[:, l], hg_fwd_consts, hg_bwd_consts, b, t, rb, (ocf, ocb))
        x1 = _out_proj(oa, ob, ocf, ocb, proj, hg_norm_g[l].reshape(1, -1).astype(F32), xs, w_out[l].astype(BF16), tm)
        ln_g = ln_ffn_g[l].reshape(1, -1)
        wu, wd = w_up[l].astype(BF16), w_down[l].astype(BF16) * 0.5
        cw, cb = conv_w[l], conv_b[l].reshape(1, -1)
        if l + 1 < depth:
            xs = [_ffn(x1, ln_g, wu, cw, cb, wd, tm_ffn, starts, ends)]
    return tuple(_ffn(x1, ln_g, wu, cw, cb, wd, tm_ffn, starts, ends, rb, b * t).reshape(xg.shape)
                 for (b, t, rb), xg in zip(groups, (x_prompt, x_sample)))
```

```python
import functools
import math

import numpy as np
import jax
import jax.numpy as jnp
from jax import lax
from jax.experimental import pallas as pl
from jax.experimental.pallas import tpu as pltpu

F32 = jnp.float32
BF16 = jnp.bfloat16

D_MODEL = 1024
GRID_W = 64
HEAD_DIM = 64
NA_HEADS = 4
NA_WIN_ROWS = 8
NA_WIN_COLS = 16
DA_HEADS = 4
DA_CONFIGS = ((128, 1), (512, 4), (2048, 16))
DA_HALO = 64
N_BUCKETS = 32
MAX_DISTANCE = 1024
HG_HEADS = 4
HG_DK = 128
HG_CHUNK = 64
N_IN = 4096
D_FF = 2816
NORM_EPS = 1e-6
NEG_INF = -1e30
LOG2_E = math.log2(math.e)
SEG = 256
FF_SEG = 256
VMEM_LIMIT = 56 * 1024 * 1024


def _cparams(*sem):
    return pltpu.CompilerParams(dimension_semantics=sem, vmem_limit_bytes=VMEM_LIMIT)


def _into(kernel, n_in, outs):
    def body(*refs):
        return kernel(*refs[:n_in], *refs[n_in + len(outs):])
    specs = [pl.BlockSpec(memory_space=pl.ANY)] * len(outs)
    shapes = [jax.ShapeDtypeStruct(o.shape, o.dtype) for o in outs]
    return body, specs, shapes, {n_in + k: k for k in range(len(outs))}


def _dot(a, b):
    return jnp.dot(a, b, preferred_element_type=F32)


def _dot_nt(a, b):
    return lax.dot_general(a, b, (((1,), (1,)), ((), ())), preferred_element_type=F32)


def _dot_tn(a, b):
    return lax.dot_general(a, b, (((0,), (0,)), ((), ())), preferred_element_type=F32)


DA_SEGS = (3, 4, 5)
DA_W = len(DA_SEGS) * SEG
DA_DILATED = tuple(d for _, d in DA_CONFIGS if d > 1)


def _select_rows(x_refs, split):
    if len(x_refs) == 1:
        return x_refs[0][...]
    return jnp.where(pl.program_id(0) < split, x_refs[0][...], x_refs[1][...])


def _row_specs(xs, tm):
    if len(xs) == 1:
        return [pl.BlockSpec((tm, xs[0].shape[1]), lambda i: (i, 0))], 0
    split = xs[0].shape[0] // tm
    return [pl.BlockSpec((tm, xs[0].shape[1]), lambda i: (jnp.minimum(i, split - 1), 0)),
            pl.BlockSpec((tm, xs[1].shape[1]), lambda i: (jnp.maximum(i - split, 0), 0))], split


def _inproj_kernel(*refs, tm, nx, split):
    x_refs, (g_ref, w_ref, qg_ref, gm_ref, o_ref), rest = refs[:nx], refs[nx:nx + 5], refs[nx + 5:]
    dil_refs, ybuf, hbuf = rest[:-2], rest[-2], rest[-1]
    x = _select_rows(x_refs, split)
    ms = jnp.mean(x * x, axis=-1, keepdims=True)
    hbuf[...] = (x * lax.rsqrt(ms + NORM_EPS) * g_ref[...]).astype(BF16)
    normed = {0: 0, 1: 1, 3: 2, 4: 3}
    nseg = N_IN // SEG

    def project(pair):
        return _dot(hbuf[...], w_ref[:, 2 * pair * SEG:2 * (pair + 1) * SEG])

    y_next = project(0)
    for seg in range(nseg):
        if seg % 2 == 0:
            y2 = y_next
            if seg + 2 < nseg:
                y_next = project(seg // 2 + 1)
        y = y2[:, (seg % 2) * SEG:(seg % 2 + 1) * SEG]
        if seg in normed:
            ss = _dot((y * y).astype(BF16), gm_ref[...])
            r = normed[seg]
            y = y * lax.rsqrt(ss + NORM_EPS) * qg_ref[r:r + 1, :]
        if seg in DA_SEGS:
            k = DA_SEGS.index(seg)
            for j in range(SEG // 128):
                ybuf[k * (SEG // 128) + j] = y[:, j * 128:(j + 1) * 128]
        o_ref[:, seg * SEG:(seg + 1) * SEG] = y.astype(BF16)
        if seg == DA_SEGS[-1]:
            for d, ref in zip(DA_DILATED, dil_refs):
                for rho in range(d):
                    for j in range(DA_W // 128):
                        ref[:, rho * DA_W + j * 128:rho * DA_W + (j + 1) * 128] = \
                            ybuf[j, pl.ds(rho, tm // d, stride=d), :].astype(BF16)


def _in_proj(xs, ln_g, w_in, qg, gm, tm):
    n = sum(x.shape[0] for x in xs)
    row = lambda i: (i, 0)
    const = lambda i: (0, 0)
    x_specs, split = _row_specs(xs, tm)
    return pl.pallas_call(
        functools.partial(_inproj_kernel, tm=tm, nx=len(xs), split=split),
        grid=(n // tm,),
        in_specs=x_specs + [
            pl.BlockSpec((1, D_MODEL), const),
            pl.BlockSpec((D_MODEL, N_IN), const, pipeline_mode=pl.Buffered(1)),
            pl.BlockSpec((4, SEG), const),
            pl.BlockSpec((SEG, SEG), const),
        ],
        out_specs=[pl.BlockSpec((tm, N_IN), row)] + [pl.BlockSpec((tm // d, d * DA_W), row) for d in DA_DILATED],
        out_shape=[jax.ShapeDtypeStruct((n, N_IN), BF16)]
        + [jax.ShapeDtypeStruct((n // d, d * DA_W), BF16) for d in DA_DILATED],
        scratch_shapes=[pltpu.VMEM((DA_W // 128, tm, 128), F32),
                        pltpu.VMEM((tm, D_MODEL), BF16)],
        compiler_params=_cparams("parallel"),
        name="in_proj",
    )(*xs, ln_g, w_in, qg, gm)


NA_RB = 8
NA_TOK = NA_RB * GRID_W
NA_GR = 4
NA_GQ = NA_GR * GRID_W
NA_KR = NA_GR + NA_WIN_ROWS
NA_GK = NA_KR * GRID_W
NA_NDR = 2 * NA_WIN_ROWS - 1


def _na_bias_table(rpb):
    cq = np.arange(GRID_W)
    ck = np.arange(GRID_W)
    c0 = np.clip(cq - NA_WIN_COLS // 2, 0, GRID_W - NA_WIN_COLS)
    col_ok = (ck[None, :] >= c0[:, None]) & (ck[None, :] < c0[:, None] + NA_WIN_COLS)
    dc = np.clip(ck[None, :] - cq[:, None], -(NA_WIN_COLS - 1), NA_WIN_COLS - 1) + NA_WIN_COLS - 1
    col_onehot = (dc[:, :, None] == np.arange(2 * NA_WIN_COLS - 1)).astype(np.float32)
    jq = np.arange(NA_GR)
    half = NA_WIN_ROWS // 2
    dq = np.stack([jq, half + jq, NA_WIN_ROWS + jq])
    dw0 = np.stack([0 * jq, jq, half + 0 * jq])
    w = np.arange(NA_KR)
    row_ok = (w[None, None, :] >= dw0[:, :, None]) & (w[None, None, :] < dw0[:, :, None] + NA_WIN_ROWS)
    dr = w[None, None, :] - dq[:, :, None] + NA_WIN_ROWS - 1
    row_onehot = ((dr[..., None] == np.arange(NA_NDR)) & row_ok[..., None]).astype(np.float32)
    t = jnp.einsum('hab,qkb->haqk', rpb.astype(F32), col_onehot, precision=lax.Precision.HIGHEST)
    bias = jnp.einsum('cjwa,haqk->chjqwk', row_onehot, t, precision=lax.Precision.HIGHEST)
    ok = row_ok[:, None, :, None, :, None] & col_ok[None, None, None, :, None, :]
    return jnp.where(ok, bias * LOG2_E, NEG_INF).reshape(3, NA_HEADS, NA_GQ, NA_GK)


def _na_kernel(q_ref, kp_ref, kc_ref, kn_ref, vp_ref, vc_ref, vn_ref, bias0_ref, bias1_ref, o_ref, kbuf, vbuf, *,
               rows):
    i = pl.program_id(1)
    kbuf[0:NA_TOK] = kp_ref[...]
    kbuf[NA_TOK:2 * NA_TOK] = kc_ref[...]
    kbuf[2 * NA_TOK:3 * NA_TOK] = kn_ref[...]
    vbuf[0:NA_TOK] = vp_ref[...]
    vbuf[NA_TOK:2 * NA_TOK] = vc_ref[...]
    vbuf[2 * NA_TOK:3 * NA_TOK] = vn_ref[...]
    low = lax.broadcasted_iota(jnp.int32, (NA_GQ, 128), 1) < HEAD_DIM
    bias_refs = (bias0_ref, bias1_ref)

    def scores(g, a):
        r = i * NA_RB + g * NA_GR
        kstart = jnp.clip(r - NA_WIN_ROWS // 2, 0, rows - NA_KR)
        off = pl.multiple_of((kstart - i * NA_RB + NA_RB) * GRID_W, GRID_W)
        cols = slice(a * 128, (a + 1) * 128)
        qa = q_ref[g * NA_GQ:(g + 1) * NA_GQ, cols]
        q2 = jnp.concatenate([jnp.where(low, qa, jnp.zeros_like(qa)), jnp.where(low, jnp.zeros_like(qa), qa)], axis=0)
        bias = jnp.concatenate([bias_refs[g][0, 2 * a], bias_refs[g][0, 2 * a + 1]], axis=0)
        return _dot_nt(q2, kbuf[pl.ds(off, NA_GK), cols]) + bias, vbuf[pl.ds(off, NA_GK), cols]

    items = [(g, a) for g in range(NA_RB // NA_GR) for a in range(2)]
    nxt = scores(*items[0])
    for n, (g, a) in enumerate(items):
        s, va = nxt
        if n + 1 < len(items):
            nxt = scores(*items[n + 1])
        m = jnp.max(s, axis=-1, keepdims=True)
        p = jnp.exp2(s - m)
        l = jnp.sum(p, axis=-1, keepdims=True)
        pv = _dot(p.astype(BF16), va) / l
        o_ref[g * NA_GQ:(g + 1) * NA_GQ, a * 128:(a + 1) * 128] = \
            jnp.where(low, pv[0:NA_GQ], pv[NA_GQ:]).astype(BF16)


def _na_call(proj, bias, b, t, row_base, out):
    rows = t // GRID_W
    nrb = rows // NA_RB
    base = row_base // NA_TOK
    assert rows >= NA_KR and rows % NA_RB == 0

    def spec(col, shift):
        def imap(bi, i):
            return (base + bi * nrb + jnp.clip(i + shift, 0, nrb - 1), col)
        return pl.BlockSpec((NA_TOK, SEG), imap)

    bias_block = (1, NA_HEADS, NA_GQ, NA_GK)
    bias0 = pl.BlockSpec(bias_block, lambda bi, i: (jnp.where(i == 0, 0, 1), 0, 0, 0))
    bias1 = pl.BlockSpec(bias_block, lambda bi, i: (jnp.where(i == nrb - 1, 2, 1), 0, 0, 0))
    in_specs = [spec(0, 0), spec(1, -1), spec(1, 0), spec(1, 1), spec(2, -1), spec(2, 0), spec(2, 1), bias0, bias1]
    body, alias_specs, out_shape, aliases = _into(functools.partial(_na_kernel, rows=rows), len(in_specs), [out])
    return pl.pallas_call(
        body,
        grid=(b, nrb),
        in_specs=in_specs + alias_specs,
        out_specs=[pl.BlockSpec((NA_TOK, SEG), lambda bi, i: (base + bi * nrb + i, 0))],
        out_shape=out_shape,
        input_output_aliases=aliases,
        scratch_shapes=[pltpu.VMEM((3 * NA_TOK, SEG), BF16), pltpu.VMEM((3 * NA_TOK, SEG), BF16)],
        compiler_params=_cparams("parallel", "parallel"),
        name="na",
    )(proj, proj, proj, proj, proj, proj, proj, bias, bias, out)[0]


def _t5_bucket(rel):
    nb = N_BUCKETS // 2
    max_exact = nb // 2
    ret = np.where(rel > 0, nb, 0)
    n = np.abs(rel)
    large = max_exact + (np.log(np.maximum(n, 1) / max_exact)
                         / np.log(MAX_DISTANCE / max_exact) * (nb - max_exact)).astype(np.int32)
    large = np.minimum(large, nb - 1)
    return (ret + np.where(n < max_exact, n, large)).astype(np.int32)


def _da_bias_table(t5_bias, dilation, qb):
    kb = qb + 2 * DA_HALO
    rel = np.arange(kb)[None, :] - np.arange(qb)[:, None] - DA_HALO
    ok = np.abs(rel) <= DA_HALO
    onehot = (_t5_bucket(rel * dilation)[:, :, None] == np.arange(N_BUCKETS)).astype(np.float32)
    bias = jnp.einsum('qkb,bh->hqk', onehot, t5_bias.astype(F32), precision=lax.Precision.HIGHEST)
    j = np.arange(kb)
    edge = np.stack([np.ones(kb, bool), j >= DA_HALO, j < qb + DA_HALO, (j >= DA_HALO) & (j < qb + DA_HALO)])
    return jnp.where(ok[None, None] & edge[:, None, None, :], (bias * LOG2_E)[None], NEG_INF)


def _da_kernel(*refs, qb, nb, d, length, prev_dils):
    final = bool(prev_dils)
    (main_ref, left_ref, right_ref, bias_ref), refs = refs[:4], refs[4:]
    if final:
        npv = 2 * len(prev_dils)
        prev_refs, o_ref, buf, und = refs[:npv], refs[npv], refs[npv + 1], refs[npv + 2:]
    else:
        o_ref, lse_ref, buf = refs
    c = pl.program_id(1)
    tr = qb * nb
    kb = qb + 2 * DA_HALO
    buf[0:DA_HALO] = left_ref[...]
    buf[DA_HALO:DA_HALO + tr] = main_ref[...]
    buf[DA_HALO + tr:tr + 2 * DA_HALO] = right_ref[...]
    if final:
        for k, pd in enumerate(prev_dils):
            for part in range(2):
                src, dst = prev_refs[2 * k + part], und[2 * k + part]
                for rho in range(pd):
                    for j in range(SEG // 128):
                        dst[j, pl.ds(rho, tr // pd, stride=pd), :] = \
                            src[:, rho * SEG + j * 128:rho * SEG + (j + 1) * 128]
    low = lax.broadcasted_iota(jnp.int32, (qb, 128), 1) < HEAD_DIM
    nblk = length // qb

    def scores(blk, rho, a):
        krows = slice(blk * qb, blk * qb + kb)
        gblk = c * nb + blk
        var = jnp.where(gblk == 0, 1, 0) + jnp.where(gblk == nblk - 1, 2, 0)
        base = rho * DA_W + a * 128
        qa = main_ref[blk * qb:(blk + 1) * qb, base:base + 128]
        q2 = jnp.concatenate([jnp.where(low, qa, jnp.zeros_like(qa)), jnp.where(low, jnp.zeros_like(qa), qa)], axis=0)
        bias = jnp.concatenate([bias_ref[var, 2 * a], bias_ref[var, 2 * a + 1]], axis=0)
        s = _dot_nt(q2, buf[krows, base + SEG:base + SEG + 128]) + bias
        return s, buf[krows, base + 2 * SEG:base + 2 * SEG + 128]

    items = [(blk, rho, a) for blk in range(nb) for rho in range(d) for a in range(2)]
    nxt = scores(*items[0])
    for n, (blk, rho, a) in enumerate(items):
        s, va = nxt
        if n + 1 < len(items):
            nxt = scores(*items[n + 1])
        rows = slice(blk * qb, (blk + 1) * qb)
        m = jnp.max(s, axis=-1, keepdims=True)
        p = jnp.exp2(s - m)
        l = jnp.sum(p, axis=-1, keepdims=True)
        pv = _dot(p.astype(BF16), va) / l
        lse2 = jnp.broadcast_to(m + jnp.log2(l), (2 * qb, 128))
        o = jnp.where(low, pv[0:qb], pv[qb:])
        lse = jnp.where(low, lse2[0:qb], lse2[qb:])
        cols = slice(rho * SEG + a * 128, rho * SEG + (a + 1) * 128)
        if final:
            mx = lse
            for k in range(len(prev_dils)):
                mx = jnp.maximum(mx, und[2 * k + 1][a, rows, :])
            wsum = jnp.exp2(lse - mx)
            mix = wsum * o
            for k in range(len(prev_dils)):
                wk = jnp.exp2(und[2 * k + 1][a, rows, :] - mx)
                wsum = wsum + wk
                mix = mix + wk * und[2 * k][a, rows, :]
            o_ref[rows, cols] = (mix / wsum).astype(BF16)
        else:
            o_ref[rows, cols] = o
            lse_ref[rows, cols] = lse


DA_QB = {1: 128, 4: 128, 16: 128}
DA_BLOCKS = {1: 8, 4: 4, 16: 1}


def _da_call(view, colblk, bias, b, t, row_base, dilation, prev=None, prev_dils=(), out=None):
    d = dilation
    length = t // d
    qb = min(DA_QB[d], length)
    nb = min(DA_BLOCKS[d], length // qb)
    tr = qb * nb
    nt = length // tr
    kb = qb + 2 * DA_HALO
    ubase = row_base // d
    hpt = tr // DA_HALO
    nhalo = view.shape[0] // DA_HALO
    final = prev is not None

    def halo(right):
        def imap(bi, c):
            blk = (ubase + bi * length) // DA_HALO + ((c + 1) * hpt if right else c * hpt - 1)
            return (jnp.clip(blk, 0, nhalo - 1), colblk)
        return pl.BlockSpec((DA_HALO, d * DA_W), imap)

    in_specs = [pl.BlockSpec((tr, d * DA_W), lambda bi, c: ((ubase + bi * length) // tr + c, colblk)),
                halo(False), halo(True),
                pl.BlockSpec((4, DA_HEADS, qb, kb), lambda bi, c: (0, 0, 0, 0))]
    args = [view] * 3 + [bias]
    out_block = pl.BlockSpec((tr, d * SEG), lambda bi, c: (bi * nt + c, 0))
    scratch = [pltpu.VMEM((tr + 2 * DA_HALO, d * DA_W), BF16)]
    if final:
        assert d == 1
        for pd in prev_dils:
            in_specs += [pl.BlockSpec((tr // pd, pd * SEG), lambda bi, c: (bi * nt + c, 0))] * 2
            scratch += [pltpu.VMEM((SEG // 128, tr, 128), F32)] * 2
        args += list(prev)
        kern = functools.partial(_da_kernel, qb=qb, nb=nb, d=d, length=length, prev_dils=tuple(prev_dils))
        body, alias_specs, out_shape, aliases = _into(kern, len(in_specs), [out])
        return pl.pallas_call(
            body,
            grid=(b, nt),
            in_specs=in_specs + alias_specs,
            out_specs=[pl.BlockSpec((tr, SEG), lambda bi, c: ((ubase + bi * length) // tr + c, 0))],
            out_shape=out_shape,
            input_output_aliases=aliases,
            scratch_shapes=scratch,
            compiler_params=_cparams("parallel", "parallel"),
            name="da_d1",
        )(*args, out)[0]
    return pl.pallas_call(
        functools.partial(_da_kernel, qb=qb, nb=nb, d=d, length=length, prev_dils=()),
        grid=(b, nt),
        in_specs=in_specs,
        out_specs=[out_block, out_block],
        out_shape=[jax.ShapeDtypeStruct((b * t // d, d * SEG), F32)] * 2,
        scratch_shapes=scratch,
        compiler_params=_cparams("parallel", "parallel"),
        name="da_d%d" % d,
    )(*args)


HG_LEVELS = (32, 16, 8, 4, 2, 1)
HG_NSEC = 2 + len(HG_LEVELS)
HG_W = HG_HEADS * HG_DK
HG_PAIR = 2 * HG_DK


def _hg_constants(rev):
    c = HG_CHUNK
    t = np.arange(c)[:, None]
    j = np.arange(c)[None, :]
    secs = []
    if not rev:
        secs.append(j <= t)
        secs.append(j > t)
    else:
        secs.append(j >= t)
        secs.append(j < t)
    masks = [np.eye(c, dtype=bool)]
    s = np.arange(c)[None, :]
    for m in HG_LEVELS:
        p0 = (t // (2 * m)) * (2 * m)
        upper = (t & m) != 0
        if not rev:
            mid = p0 + m - 1
            sec = np.where(upper, (j > mid) & (j <= t), (j > t) & (j <= mid))
            pair = ((t // (2 * m)) == (s // (2 * m))) & upper & ((s & m) == 0)
        else:
            mid = p0 + m
            sec = np.where(upper, (j >= mid) & (j < t), (j >= t) & (j < mid))
            pair = ((t // (2 * m)) == (s // (2 * m))) & (~upper) & ((s & m) != 0)
        secs.append(sec)
        masks.append(pair)
    w = np.tile(np.concatenate(secs, axis=0).astype(np.float32), (1, 2))
    masks = np.tile(np.stack(masks).astype(np.float32), (1, 1, 2))
    return jnp.asarray(w, BF16), jnp.asarray(masks)


def _bdiag(h0, h1):
    zero = jnp.zeros_like(h0)
    return jnp.concatenate([jnp.concatenate([h0, zero], axis=1), jnp.concatenate([zero, h1], axis=1)], axis=0)


def _hg_pair_operand(a, p):
    return _bdiag(a[:, (2 * p) * HG_DK:(2 * p + 1) * HG_DK], a[:, (2 * p + 1) * HG_DK:(2 * p + 2) * HG_DK])


def _hg_prepare(rev, z, x, v, lb, w_ref, mask_ref):
    c_sz = HG_CHUNK
    z2 = z * LOG2_E
    log2_lb = jnp.log2(lb)
    log2_sig = jnp.minimum(z2, 0.0) - jnp.log2(1.0 + jnp.exp2(-jnp.abs(z2)))
    cc = jnp.log2(1.0 - lb) + log2_sig
    log2f = jnp.maximum(log2_lb, cc) + jnp.log2(1.0 + jnp.exp2(-jnp.abs(log2_lb - cc)))
    kk = (1.0 - lb) * jnp.exp2(log2_sig - z2)
    q = x / (1.0 + jnp.exp2(x * -LOG2_E))
    hi = log2f.astype(BF16)
    lo = (log2f - hi.astype(F32)).astype(BF16)
    dec = jnp.exp2(_dot(w_ref[...], jnp.concatenate([hi, lo], axis=0)))
    sub = lax.broadcasted_iota(jnp.int32, (1, 8, HG_W), 1)
    ys = []
    for lvl, m in enumerate(HG_LEVELS):
        if m >= 8:
            sel = jnp.concatenate(
                [(q if (((8 * r) & m) != 0) != rev else kk)[8 * r:8 * r + 8] for r in range(c_sz // 8)], axis=0)
        else:
            is_q = ((sub & m) == 0) if rev else ((sub & m) != 0)
            sel = jnp.where(is_q, q.reshape(c_sz // 8, 8, HG_W), kk.reshape(c_sz // 8, 8, HG_W)).reshape(c_sz, HG_W)
        ys.append((sel * dec[(2 + lvl) * c_sz:(3 + lvl) * c_sz]).astype(BF16))
    q16, k16 = q.astype(BF16), kk.astype(BF16)
    kd = (kk * dec[c_sz:2 * c_sz]).astype(BF16)
    intra = []
    for p in range(HG_HEADS // 2):
        cols = slice(p * HG_PAIR, (p + 1) * HG_PAIR)
        a2 = mask_ref[0] * _dot_nt(q16[:, cols], _hg_pair_operand(k16, p))
        for lvl in range(len(HG_LEVELS)):
            a2 = a2 + mask_ref[1 + lvl] * _dot_nt(ys[lvl][:, cols], _hg_pair_operand(ys[lvl], p))
        intra.append(_dot(a2.astype(BF16), _hg_pair_operand(v, p)))
    gain = [_dot_tn(v[:, h * HG_DK:(h + 1) * HG_DK], kd[:, h * HG_DK:(h + 1) * HG_DK]) for h in range(HG_HEADS)]
    edge = 0 if rev else c_sz - 1
    return dict(intra=intra, gain=gain, qd=(q * dec[0:c_sz]).astype(BF16), decay=dec[edge:edge + 1])


def _hg_finish(prep, st_ref):
    outs = []
    for p in range(HG_HEADS // 2):
        cols = slice(p * HG_PAIR, (p + 1) * HG_PAIR)
        st = _bdiag(st_ref[2 * p].astype(BF16), st_ref[2 * p + 1].astype(BF16))
        outs.append(prep["intra"][p] + _dot_nt(prep["qd"][:, cols], st))
        for h in (2 * p, 2 * p + 1):
            st_ref[h] = st_ref[h] * prep["decay"][:, h * HG_DK:(h + 1) * HG_DK] + prep["gain"][h]
    return jnp.concatenate(outs, axis=1)


def _hg_kernel(cqf_ref, cff_ref, cif_ref, cqb_ref, cfb_ref, cib_ref, lb_ref, wf_ref, wb_ref, mf_ref, mb_ref,
               of_ref, ob_ref, st_ref, *, nchunks):
    @pl.when(pl.program_id(1) == 0)
    def _():
        st_ref[...] = jnp.zeros_like(st_ref)

    def prepare(step):
        rf = slice(step * HG_CHUNK, (step + 1) * HG_CHUNK)
        rb = slice((nchunks - 1 - step) * HG_CHUNK, (nchunks - step) * HG_CHUNK)
        fwd = _hg_prepare(False, cff_ref[rf, :].astype(F32), cqf_ref[rf, :].astype(F32), cif_ref[rf, :],
                          lb_ref[0:1, :], wf_ref, mf_ref)
        bwd = _hg_prepare(True, cfb_ref[rb, :].astype(F32), cqb_ref[rb, :].astype(F32), cib_ref[rb, :],
                          lb_ref[1:2, :], wb_ref, mb_ref)
        return rf, fwd, rb, bwd

    def finish(rf, fwd, rb, bwd):
        of_ref[rf, :] = _hg_finish(fwd, st_ref.at[0]).astype(BF16)
        ob_ref[rb, :] = _hg_finish(bwd, st_ref.at[1]).astype(BF16)

    pending = prepare(0)
    for step in range(1, nchunks):
        nxt = prepare(step)
        finish(*pending)
        pending = nxt
    finish(*pending)


def _hg_call(proj, lb, consts_f, consts_b, b, t, row_base, outs):
    tb = min(1024, t)
    nt = t // tb
    base = row_base // tb

    def pspec(col, rev):
        return pl.BlockSpec((tb, HG_W), lambda bi, i: (base + bi * nt + (nt - 1 - i if rev else i), col))

    def const(a):
        return pl.BlockSpec(a.shape, lambda bi, i: (0,) * a.ndim)

    (wf, mf), (wb, mb) = consts_f, consts_b
    in_specs = [pspec(3, False), pspec(4, False), pspec(6, False), pspec(3, True), pspec(5, True), pspec(6, True),
                const(lb), const(wf), const(wb), const(mf), const(mb)]
    body, alias_specs, out_shape, aliases = _into(functools.partial(_hg_kernel, nchunks=tb // HG_CHUNK),
                                                  len(in_specs), list(outs))
    return pl.pallas_call(
        body,
        grid=(b, nt),
        in_specs=in_specs + alias_specs,
        out_specs=[pspec(0, False), pspec(0, True)],
        out_shape=out_shape,
        input_output_aliases=aliases,
        scratch_shapes=[pltpu.VMEM((2, HG_HEADS, HG_DK, HG_DK), F32)],
        compiler_params=_cparams("parallel", "arbitrary"),
        name="hgrn",
    )(proj, proj, proj, proj, proj, proj, lb, wf, wb, mf, mb, *outs)


def _outproj_kernel(oa_ref, ob_ref, cf_ref, cb_ref, cg_ref, ng_ref, w_ref, *rest, split):
    x_refs, (x1_ref, mix) = rest[:-2], rest[-2:]
    tot = cf_ref[...].astype(F32) + cb_ref[...].astype(F32)
    gate = cg_ref[...].astype(F32)
    parts = []
    for h in range(HG_HEADS):
        th = tot[:, h * HG_DK:(h + 1) * HG_DK]
        ms = jnp.mean(th * th, axis=-1, keepdims=True)
        parts.append(th * lax.rsqrt(ms + NORM_EPS) * ng_ref[...])
    oc = (jnp.concatenate(parts, axis=1) * (gate / (1.0 + jnp.exp(-gate)))).astype(BF16)
    mix[:, 0:SEG] = oa_ref[...]
    mix[:, SEG:2 * SEG] = ob_ref[...]
    mix[:, 2 * SEG:] = oc
    x1_ref[...] = _select_rows(x_refs, split) + _dot(mix[...], w_ref[...])


def _out_proj(oa, ob, ocf, ocb, proj, norm_g, xs, w_out, tm):
    n = oa.shape[0]
    row = lambda i: (i, 0)
    const = lambda i: (0, 0)
    x_specs, split = _row_specs(xs, tm)
    return pl.pallas_call(
        functools.partial(_outproj_kernel, split=split),
        grid=(n // tm,),
        in_specs=[pl.BlockSpec((tm, SEG), row), pl.BlockSpec((tm, SEG), row),
                  pl.BlockSpec((tm, HG_W), row), pl.BlockSpec((tm, HG_W), row),
                  pl.BlockSpec((tm, HG_W), lambda i: (i, N_IN // HG_W - 1)),
                  pl.BlockSpec((1, HG_DK), const),
                  pl.BlockSpec((D_MODEL, D_MODEL), const)] + x_specs,
        out_specs=pl.BlockSpec((tm, D_MODEL), row),
        out_shape=jax.ShapeDtypeStruct((n, D_MODEL), F32),
        scratch_shapes=[pltpu.VMEM((tm, D_MODEL), BF16)],
        compiler_params=_cparams("parallel"),
        name="out_proj",
    )(oa, ob, ocf, ocb, proj, norm_g, w_out, *xs)


FF_HALO = 16
FF_ROWS = 1024


def _ffn_kernel(x_ref, xp_ref, xn_ref, g_ref, wg_ref, wu_ref, cw_ref, cb_ref, wdn_ref, o_ref, hbuf, abuf, *, tm, row_base,
                starts, ends):
    row0 = row_base + pl.program_id(0) * tm
    at_start = functools.reduce(jnp.logical_or, [row0 == s for s in starts])
    at_end = functools.reduce(jnp.logical_or, [row0 + tm == e for e in ends])
    keep_prev = jnp.where(at_start, 0.0, 1.0)
    keep_next = jnp.where(at_end, 0.0, 1.0)
    def normed(ref):
        x = ref[...]
        ms = jnp.mean(x * x, axis=-1, keepdims=True)
        return (x * lax.rsqrt(ms + NORM_EPS) * g_ref[...]).astype(BF16)

    hbuf[0:tm] = normed(x_ref)
    hbuf[tm:tm + FF_HALO] = normed(xp_ref)
    hbuf[tm + FF_HALO:tm + 2 * FF_HALO] = normed(xn_ref)
    first = lax.broadcasted_iota(jnp.int32, (8, FF_SEG), 0) == 0
    last = lax.broadcasted_iota(jnp.int32, (8, FF_SEG), 0) == 7
    c_tanh = math.sqrt(2.0 / math.pi)
    nseg = D_FF // FF_SEG

    def up_proj(j):
        w = jnp.concatenate([wg_ref[:, j * FF_SEG:(j + 1) * FF_SEG], wu_ref[:, j * FF_SEG:(j + 1) * FF_SEG]], axis=1)
        return _dot(hbuf[...], w)

    y_next = up_proj(0)
    for j in range(nseg):
        cols = slice(j * FF_SEG, (j + 1) * FF_SEG)
        y = y_next
        if j + 1 < nseg:
            y_next = up_proj(j + 1)
        g, up = y[0:tm, 0:FF_SEG], y[0:tm, FF_SEG:]
        g_prev = y[tm + FF_HALO - 1:tm + FF_HALO, 0:FF_SEG] * keep_prev
        g_next = y[tm + FF_HALO:tm + FF_HALO + 1, 0:FF_SEG] * keep_next
        cw0, cw1, cw2 = cw_ref[0:1, cols], cw_ref[1:2, cols], cw_ref[2:3, cols]
        gate = pltpu.roll(g, 1, 0) * cw0 + g * cw1 + pltpu.roll(g, tm - 1, 0) * cw2 + cb_ref[:, cols]
        top = gate[0:8] + jnp.where(first, (g_prev - g[tm - 1:tm]) * cw0, 0.0)
        bot = gate[tm - 8:tm] + jnp.where(last, (g_next - g[0:1]) * cw2, 0.0)
        gate = jnp.concatenate([top, gate[8:tm - 8], bot], axis=0)
        inner = gate * (c_tanh + (c_tanh * 0.044715) * (gate * gate))
        act = (gate * up) * (1.0 + jnp.tanh(inner))
        abuf[:, cols] = act.astype(BF16)
    o_ref[...] = x_ref[...] + _dot(abuf[...], wdn_ref[...])


def _ffn(x1, ln_g, w_up, conv_w, conv_b, w_down, tm, starts, ends, row_base=0, nrows=None):
    n = x1.shape[0]
    nrows = n if nrows is None else nrows
    hpb = tm // FF_HALO
    nh = n // FF_HALO
    t0 = row_base // tm
    row = lambda i: (t0 + i, 0)
    const = lambda i: (0, 0)
    once = pl.Buffered(1)
    return pl.pallas_call(
        functools.partial(_ffn_kernel, tm=tm, row_base=row_base, starts=starts, ends=ends),
        grid=(nrows // tm,),
        in_specs=[pl.BlockSpec((tm, D_MODEL), row),
                  pl.BlockSpec((FF_HALO, D_MODEL), lambda i: (jnp.maximum((t0 + i) * hpb - 1, 0), 0)),
                  pl.BlockSpec((FF_HALO, D_MODEL), lambda i: (jnp.minimum((t0 + i + 1) * hpb, nh - 1), 0)),
                  pl.BlockSpec((1, D_MODEL), const),
                  pl.BlockSpec((D_MODEL, D_FF), lambda i: (0, 0), pipeline_mode=once),
                  pl.BlockSpec((D_MODEL, D_FF), lambda i: (0, 1), pipeline_mode=once),
                  pl.BlockSpec((3, D_FF), const), pl.BlockSpec((1, D_FF), const),
                  pl.BlockSpec((D_FF, D_MODEL), const, pipeline_mode=once)],
        out_specs=pl.BlockSpec((tm, D_MODEL), lambda i: (i, 0)),
        out_shape=jax.ShapeDtypeStruct((nrows, D_MODEL), F32),
        scratch_shapes=[pltpu.VMEM((tm + 2 * FF_HALO, D_MODEL), BF16), pltpu.VMEM((tm, D_FF), BF16)],
        compiler_params=_cparams("parallel"),
        name="ffn",
    )(x1, x1, x1, ln_g, w_up, w_up, conv_w, conv_b, w_down)


def _lower_bounds(lb_logits):
    p = jax.nn.softmax(lb_logits.astype(F32), axis=1)
    c = jnp.cumsum(p, axis=1)
    return c - c[:, :1]


def _row_tile(groups, tm=512):
    while any(t % tm for _, t, _ in groups):
        tm //= 2
    return tm


def kernel(x_prompt, x_sample, ln_mix_g, w_in, na_q_g, na_k_g, na_rpb, da_q_g, da_k_g, t5_bias, hg_lb_logits,
           hg_norm_g, w_out, ln_ffn_g, w_up, conv_w, conv_b, w_down):
    depth = w_in.shape[0]
    groups = []
    base = 0
    for xg in (x_prompt, x_sample):
        b, t, _ = xg.shape
        groups.append((b, t, base))
        base += b * t
    starts = tuple(rb + i * t for b, t, rb in groups for i in range(b))
    ends = tuple(rb + (i + 1) * t for b, t, rb in groups for i in range(b))
    tm = _row_tile(groups)
    tm_ffn = _row_tile(groups, FF_ROWS)
    n = base
    xs = [x_prompt.reshape(-1, D_MODEL), x_sample.reshape(-1, D_MODEL)]

    lb = _lower_bounds(hg_lb_logits)
    scale = HEAD_DIM ** -0.5
    tile4 = lambda g: jnp.tile(g.astype(F32), NA_HEADS)
    gm = jnp.asarray(np.kron(np.eye(SEG // HEAD_DIM), np.ones((HEAD_DIM, HEAD_DIM))) / HEAD_DIM, BF16)
    hg_fwd_consts = _hg_constants(False)
    hg_bwd_consts = _hg_constants(True)
    da_bias = {}

    def da_bias_for(d, qb):
        if (d, qb) not in da_bias:
            da_bias[(d, qb)] = _da_bias_table(t5_bias, d, qb)
        return da_bias[(d, qb)]

    for l in range(depth):
        qg = jnp.stack([tile4(na_q_g[l]) * (scale * LOG2_E), tile4(na_k_g[l]),
                        tile4(da_q_g[l]) * (scale * LOG2_E), tile4(da_k_g[l])])
        proj, *dil_views = _in_proj(xs, ln_mix_g[l].reshape(1, -1), w_in[l].astype(BF16), qg, gm, tm_ffn)
        na_bias = _na_bias_table(na_rpb[l])
        if l == 0:
            oa = jnp.zeros((n, SEG), BF16)
            ob = jnp.zeros((n, SEG), BF16)
            ocf = jnp.zeros((n, HG_W), BF16)
            ocb = jnp.zeros((n, HG_W), BF16)
        for b, t, rb in groups:
            oa = _na_call(proj, na_bias, b, t, rb, oa)
            prev = []
            for d, view in zip(DA_DILATED, dil_views):
                prev += _da_call(view, 0, da_bias_for(d, min(DA_QB[d], t // d)), b, t, rb, d)
            ob = _da_call(proj, DA_SEGS[0] * SEG // DA_W, da_bias_for(1, min(DA_QB[1], t)), b, t, rb, 1,
                          prev=prev, prev_dils=DA_DILATED, out=ob)
            ocf, ocb = _hg_call(proj, lb[:, l], hg_fwd_consts, hg_bwd_consts, b, t, rb, (ocf, ocb))
        x1 = _out_proj(oa, ob, ocf, ocb, proj, hg_norm_g[l].reshape(1, -1).astype(F32), xs, w_out[l].astype(BF16), tm)
        ln_g = ln_ffn_g[l].reshape(1, -1)
        wu, wd = w_up[l].astype(BF16), w_down[l].astype(BF16) * 0.5
        cw, cb = conv_w[l], conv_b[l].reshape(1, -1)
        if l + 1 < depth:
            xs = [_ffn(x1, ln_g, wu, cw, cb, wd, tm_ffn, starts, ends)]
    return tuple(_ffn(x1, ln_g, wu, cw, cb, wd, tm_ffn, starts, ends, rb, b * t).reshape(xg.shape)
                 for (b, t, rb), xg in zip(groups, (x_prompt, x_sample)))
```

```python
import functools
import math

import numpy as np
import jax
import jax.numpy as jnp
from jax import lax
from jax.experimental import pallas as pl
from jax.experimental.pallas import tpu as pltpu

F32 = jnp.float32
BF16 = jnp.bfloat16

D_MODEL = 1024
GRID_W = 64
HEAD_DIM = 64
NA_HEADS = 4
NA_WIN_ROWS = 8
NA_WIN_COLS = 16
DA_HEADS = 4
DA_CONFIGS = ((128, 1), (512, 4), (2048, 16))
DA_HALO = 64
N_BUCKETS = 32
MAX_DISTANCE = 1024
HG_HEADS = 4
HG_DK = 128
HG_CHUNK = 64
N_IN = 4096
D_FF = 2816
NORM_EPS = 1e-6
NEG_INF = -1e30
LOG2_E = math.log2(math.e)
SEG = 256
FF_SEG = 256
ROW_TILE = 1024
VMEM_LIMIT = 56 * 1024 * 1024


def _cparams(*sem):
    return pltpu.CompilerParams(dimension_semantics=sem, vmem_limit_bytes=VMEM_LIMIT)


def _into(kernel, n_in, outs):
    def body(*refs):
        return kernel(*refs[:n_in], *refs[n_in + len(outs):])
    specs = [pl.BlockSpec(memory_space=pl.ANY)] * len(outs)
    shapes = [jax.ShapeDtypeStruct(o.shape, o.dtype) for o in outs]
    return body, specs, shapes, {n_in + k: k for k in range(len(outs))}


def _dot(a, b):
    return jnp.dot(a, b, preferred_element_type=F32)


def _dot_nt(a, b):
    return lax.dot_general(a, b, (((1,), (1,)), ((), ())), preferred_element_type=F32)


def _dot_tn(a, b):
    return lax.dot_general(a, b, (((0,), (0,)), ((), ())), preferred_element_type=F32)


DA_SEGS = (3, 4, 5)
DA_W = len(DA_SEGS) * SEG
DA_DILATED = tuple(d for _, d in DA_CONFIGS if d > 1)


def _select_rows(x_refs, split):
    if len(x_refs) == 1:
        return x_refs[0][...]
    return jnp.where(pl.program_id(0) < split, x_refs[0][...], x_refs[1][...])


def _row_specs(xs, tm):
    if len(xs) == 1:
        return [pl.BlockSpec((tm, xs[0].shape[1]), lambda i: (i, 0))], 0
    split = xs[0].shape[0] // tm
    return [pl.BlockSpec((tm, xs[0].shape[1]), lambda i: (jnp.minimum(i, split - 1), 0)),
            pl.BlockSpec((tm, xs[1].shape[1]), lambda i: (jnp.maximum(i - split, 0), 0))], split


def _inproj_kernel(*refs, tm, nx, split):
    x_refs, (g_ref, w_ref, qg_ref, gm_ref, o_ref), rest = refs[:nx], refs[nx:nx + 5], refs[nx + 5:]
    dil_refs, ybuf, hbuf = rest[:-2], rest[-2], rest[-1]
    x = _select_rows(x_refs, split)
    ms = jnp.mean(x * x, axis=-1, keepdims=True)
    hbuf[...] = (x * lax.rsqrt(ms + NORM_EPS) * g_ref[...]).astype(BF16)
    normed = {0: 0, 1: 1, 3: 2, 4: 3}
    nseg = N_IN // SEG

    def project(pair):
        return _dot(hbuf[...], w_ref[:, 2 * pair * SEG:2 * (pair + 1) * SEG])

    y_next = project(0)
    for seg in range(nseg):
        if seg % 2 == 0:
            y2 = y_next
            if seg + 2 < nseg:
                y_next = project(seg // 2 + 1)
        y = y2[:, (seg % 2) * SEG:(seg % 2 + 1) * SEG]
        if seg in normed:
            ss = _dot((y * y).astype(BF16), gm_ref[...])
            r = normed[seg]
            y = y * lax.rsqrt(ss + NORM_EPS) * qg_ref[r:r + 1, :]
        if seg in DA_SEGS:
            k = DA_SEGS.index(seg)
            for j in range(SEG // 128):
                ybuf[k * (SEG // 128) + j] = y[:, j * 128:(j + 1) * 128]
        o_ref[:, seg * SEG:(seg + 1) * SEG] = y.astype(BF16)
        if seg == DA_SEGS[-1]:
            for d, ref in zip(DA_DILATED, dil_refs):
                for rho in range(d):
                    for j in range(DA_W // 128):
                        ref[:, rho * DA_W + j * 128:rho * DA_W + (j + 1) * 128] = \
                            ybuf[j, pl.ds(rho, tm // d, stride=d), :].astype(BF16)


def _in_proj(xs, ln_g, w_in, qg, gm, tm):
    n = sum(x.shape[0] for x in xs)
    row = lambda i: (i, 0)
    const = lambda i: (0, 0)
    x_specs, split = _row_specs(xs, tm)
    return pl.pallas_call(
        functools.partial(_inproj_kernel, tm=tm, nx=len(xs), split=split),
        grid=(n // tm,),
        in_specs=x_specs + [
            pl.BlockSpec((1, D_MODEL), const),
            pl.BlockSpec((D_MODEL, N_IN), const, pipeline_mode=pl.Buffered(1)),
            pl.BlockSpec((4, SEG), const),
            pl.BlockSpec((SEG, SEG), const),
        ],
        out_specs=[pl.BlockSpec((tm, N_IN), row)] + [pl.BlockSpec((tm // d, d * DA_W), row) for d in DA_DILATED],
        out_shape=[jax.ShapeDtypeStruct((n, N_IN), BF16)]
        + [jax.ShapeDtypeStruct((n // d, d * DA_W), BF16) for d in DA_DILATED],
        scratch_shapes=[pltpu.VMEM((DA_W // 128, tm, 128), F32),
                        pltpu.VMEM((tm, D_MODEL), BF16)],
        compiler_params=_cparams("parallel"),
        name="in_proj",
    )(*xs, ln_g, w_in, qg, gm)


NA_RB = 8
NA_TOK = NA_RB * GRID_W
NA_GR = 4
NA_GQ = NA_GR * GRID_W
NA_KR = NA_GR + NA_WIN_ROWS
NA_GK = NA_KR * GRID_W
NA_NDR = 2 * NA_WIN_ROWS - 1


def _na_bias_table(rpb):
    cq = np.arange(GRID_W)
    ck = np.arange(GRID_W)
    c0 = np.clip(cq - NA_WIN_COLS // 2, 0, GRID_W - NA_WIN_COLS)
    col_ok = (ck[None, :] >= c0[:, None]) & (ck[None, :] < c0[:, None] + NA_WIN_COLS)
    dc = np.clip(ck[None, :] - cq[:, None], -(NA_WIN_COLS - 1), NA_WIN_COLS - 1) + NA_WIN_COLS - 1
    col_onehot = (dc[:, :, None] == np.arange(2 * NA_WIN_COLS - 1)).astype(np.float32)
    jq = np.arange(NA_GR)
    half = NA_WIN_ROWS // 2
    dq = np.stack([jq, half + jq, NA_WIN_ROWS + jq])
    dw0 = np.stack([0 * jq, jq, half + 0 * jq])
    w = np.arange(NA_KR)
    row_ok = (w[None, None, :] >= dw0[:, :, None]) & (w[None, None, :] < dw0[:, :, None] + NA_WIN_ROWS)
    dr = w[None, None, :] - dq[:, :, None] + NA_WIN_ROWS - 1
    row_onehot = ((dr[..., None] == np.arange(NA_NDR)) & row_ok[..., None]).astype(np.float32)
    t = jnp.einsum('hab,qkb->haqk', rpb.astype(F32), col_onehot, precision=lax.Precision.HIGHEST)
    bias = jnp.einsum('cjwa,haqk->chjqwk', row_onehot, t, precision=lax.Precision.HIGHEST)
    ok = row_ok[:, None, :, None, :, None] & col_ok[None, None, None, :, None, :]
    return jnp.where(ok, bias * LOG2_E, NEG_INF).reshape(3, NA_HEADS, NA_GQ, NA_GK)


def _na_kernel(q_ref, kp_ref, kc_ref, kn_ref, vp_ref, vc_ref, vn_ref, bias0_ref, bias1_ref, o_ref, kbuf, vbuf, *,
               rows):
    i = pl.program_id(1)
    kbuf[0:NA_TOK] = kp_ref[...]
    kbuf[NA_TOK:2 * NA_TOK] = kc_ref[...]
    kbuf[2 * NA_TOK:3 * NA_TOK] = kn_ref[...]
    vbuf[0:NA_TOK] = vp_ref[...]
    vbuf[NA_TOK:2 * NA_TOK] = vc_ref[...]
    vbuf[2 * NA_TOK:3 * NA_TOK] = vn_ref[...]
    low = lax.broadcasted_iota(jnp.int32, (NA_GQ, 128), 1) < HEAD_DIM
    bias_refs = (bias0_ref, bias1_ref)

    def scores(g, a):
        r = i * NA_RB + g * NA_GR
        kstart = jnp.clip(r - NA_WIN_ROWS // 2, 0, rows - NA_KR)
        off = pl.multiple_of((kstart - i * NA_RB + NA_RB) * GRID_W, GRID_W)
        cols = slice(a * 128, (a + 1) * 128)
        qa = q_ref[g * NA_GQ:(g + 1) * NA_GQ, cols]
        q2 = jnp.concatenate([jnp.where(low, qa, jnp.zeros_like(qa)), jnp.where(low, jnp.zeros_like(qa), qa)], axis=0)
        bias = jnp.concatenate([bias_refs[g][0, 2 * a], bias_refs[g][0, 2 * a + 1]], axis=0)
        return _dot_nt(q2, kbuf[pl.ds(off, NA_GK), cols]) + bias, vbuf[pl.ds(off, NA_GK), cols]

    items = [(g, a) for g in range(NA_RB // NA_GR) for a in range(2)]
    nxt = scores(*items[0])
    for n, (g, a) in enumerate(items):
        s, va = nxt
        if n + 1 < len(items):
            nxt = scores(*items[n + 1])
        m = jnp.max(s, axis=-1, keepdims=True)
        p = jnp.exp2(s - m)
        l = jnp.sum(p, axis=-1, keepdims=True)
        pv = _dot(p.astype(BF16), va) / l
        o_ref[g * NA_GQ:(g + 1) * NA_GQ, a * 128:(a + 1) * 128] = \
            jnp.where(low, pv[0:NA_GQ], pv[NA_GQ:]).astype(BF16)


def _na_call(proj, bias, b, t, row_base, out):
    rows = t // GRID_W
    nrb = rows // NA_RB
    base = row_base // NA_TOK
    assert rows >= NA_KR and rows % NA_RB == 0

    def spec(col, shift):
        def imap(bi, i):
            return (base + bi * nrb + jnp.clip(i + shift, 0, nrb - 1), col)
        return pl.BlockSpec((NA_TOK, SEG), imap)

    bias_block = (1, NA_HEADS, NA_GQ, NA_GK)
    bias0 = pl.BlockSpec(bias_block, lambda bi, i: (jnp.where(i == 0, 0, 1), 0, 0, 0))
    bias1 = pl.BlockSpec(bias_block, lambda bi, i: (jnp.where(i == nrb - 1, 2, 1), 0, 0, 0))
    in_specs = [spec(0, 0), spec(1, -1), spec(1, 0), spec(1, 1), spec(2, -1), spec(2, 0), spec(2, 1), bias0, bias1]
    body, alias_specs, out_shape, aliases = _into(functools.partial(_na_kernel, rows=rows), len(in_specs), [out])
    return pl.pallas_call(
        body,
        grid=(b, nrb),
        in_specs=in_specs + alias_specs,
        out_specs=[pl.BlockSpec((NA_TOK, SEG), lambda bi, i: (base + bi * nrb + i, 0))],
        out_shape=out_shape,
        input_output_aliases=aliases,
        scratch_shapes=[pltpu.VMEM((3 * NA_TOK, SEG), BF16), pltpu.VMEM((3 * NA_TOK, SEG), BF16)],
        compiler_params=_cparams("parallel", "parallel"),
        name="na",
    )(proj, proj, proj, proj, proj, proj, proj, bias, bias, out)[0]


def _t5_bucket(rel):
    nb = N_BUCKETS // 2
    max_exact = nb // 2
    ret = np.where(rel > 0, nb, 0)
    n = np.abs(rel)
    large = max_exact + (np.log(np.maximum(n, 1) / max_exact)
                         / np.log(MAX_DISTANCE / max_exact) * (nb - max_exact)).astype(np.int32)
    large = np.minimum(large, nb - 1)
    return (ret + np.where(n < max_exact, n, large)).astype(np.int32)


def _da_bias_table(t5_bias, dilation, qb):
    kb = qb + 2 * DA_HALO
    rel = np.arange(kb)[None, :] - np.arange(qb)[:, None] - DA_HALO
    ok = np.abs(rel) <= DA_HALO
    onehot = (_t5_bucket(rel * dilation)[:, :, None] == np.arange(N_BUCKETS)).astype(np.float32)
    bias = jnp.einsum('qkb,bh->hqk', onehot, t5_bias.astype(F32), precision=lax.Precision.HIGHEST)
    j = np.arange(kb)
    edge = np.stack([np.ones(kb, bool), j >= DA_HALO, j < qb + DA_HALO, (j >= DA_HALO) & (j < qb + DA_HALO)])
    return jnp.where(ok[None, None] & edge[:, None, None, :], (bias * LOG2_E)[None], NEG_INF)


def _da_kernel(*refs, qb, nb, d, length, prev_dils):
    final = bool(prev_dils)
    (main_ref, left_ref, right_ref, bias_ref), refs = refs[:4], refs[4:]
    if final:
        npv = 2 * len(prev_dils)
        prev_refs, o_ref, buf, und = refs[:npv], refs[npv], refs[npv + 1], refs[npv + 2:]
    else:
        o_ref, lse_ref, buf = refs
    c = pl.program_id(1)
    tr = qb * nb
    kb = qb + 2 * DA_HALO
    buf[0:DA_HALO] = left_ref[...]
    buf[DA_HALO:DA_HALO + tr] = main_ref[...]
    buf[DA_HALO + tr:tr + 2 * DA_HALO] = right_ref[...]
    if final:
        for k, pd in enumerate(prev_dils):
            for part in range(2):
                src, dst = prev_refs[2 * k + part], und[2 * k + part]
                for rho in range(pd):
                    for j in range(SEG // 128):
                        dst[j, pl.ds(rho, tr // pd, stride=pd), :] = \
                            src[:, rho * SEG + j * 128:rho * SEG + (j + 1) * 128]
    low = lax.broadcasted_iota(jnp.int32, (qb, 128), 1) < HEAD_DIM
    nblk = length // qb

    def scores(blk, rho, a):
        krows = slice(blk * qb, blk * qb + kb)
        gblk = c * nb + blk
        var = jnp.where(gblk == 0, 1, 0) + jnp.where(gblk == nblk - 1, 2, 0)
        base = rho * DA_W + a * 128
        qa = main_ref[blk * qb:(blk + 1) * qb, base:base + 128]
        q2 = jnp.concatenate([jnp.where(low, qa, jnp.zeros_like(qa)), jnp.where(low, jnp.zeros_like(qa), qa)], axis=0)
        bias = jnp.concatenate([bias_ref[var, 2 * a], bias_ref[var, 2 * a + 1]], axis=0)
        s = _dot_nt(q2, buf[krows, base + SEG:base + SEG + 128]) + bias
        return s, buf[krows, base + 2 * SEG:base + 2 * SEG + 128]

    items = [(blk, rho, a) for blk in range(nb) for rho in range(d) for a in range(2)]
    nxt = scores(*items[0])
    for n, (blk, rho, a) in enumerate(items):
        s, va = nxt
        if n + 1 < len(items):
            nxt = scores(*items[n + 1])
        rows = slice(blk * qb, (blk + 1) * qb)
        m = jnp.max(s, axis=-1, keepdims=True)
        p = jnp.exp2(s - m)
        l = jnp.sum(p, axis=-1, keepdims=True)
        pv = _dot(p.astype(BF16), va) / l
        lse2 = jnp.broadcast_to(m + jnp.log2(l), (2 * qb, 128))
        o = jnp.where(low, pv[0:qb], pv[qb:])
        lse = jnp.where(low, lse2[0:qb], lse2[qb:])
        cols = slice(rho * SEG + a * 128, rho * SEG + (a + 1) * 128)
        if final:
            mx = lse
            for k in range(len(prev_dils)):
                mx = jnp.maximum(mx, und[2 * k + 1][a, rows, :])
            wsum = jnp.exp2(lse - mx)
            mix = wsum * o
            for k in range(len(prev_dils)):
                wk = jnp.exp2(und[2 * k + 1][a, rows, :] - mx)
                wsum = wsum + wk
                mix = mix + wk * und[2 * k][a, rows, :]
            o_ref[rows, cols] = (mix / wsum).astype(BF16)
        else:
            o_ref[rows, cols] = o
            lse_ref[rows, cols] = lse


DA_QB = {1: 128, 4: 128, 16: 128}
DA_BLOCKS = {1: 8, 4: 4, 16: 1}


def _da_call(view, colblk, bias, b, t, row_base, dilation, prev=None, prev_dils=(), out=None):
    d = dilation
    length = t // d
    qb = min(DA_QB[d], length)
    nb = min(DA_BLOCKS[d], length // qb)
    tr = qb * nb
    nt = length // tr
    kb = qb + 2 * DA_HALO
    ubase = row_base // d
    hpt = tr // DA_HALO
    nhalo = view.shape[0] // DA_HALO
    final = prev is not None

    def halo(right):
        def imap(bi, c):
            blk = (ubase + bi * length) // DA_HALO + ((c + 1) * hpt if right else c * hpt - 1)
            return (jnp.clip(blk, 0, nhalo - 1), colblk)
        return pl.BlockSpec((DA_HALO, d * DA_W), imap)

    in_specs = [pl.BlockSpec((tr, d * DA_W), lambda bi, c: ((ubase + bi * length) // tr + c, colblk)),
                halo(False), halo(True),
                pl.BlockSpec((4, DA_HEADS, qb, kb), lambda bi, c: (0, 0, 0, 0))]
    args = [view] * 3 + [bias]
    out_block = pl.BlockSpec((tr, d * SEG), lambda bi, c: (bi * nt + c, 0))
    scratch = [pltpu.VMEM((tr + 2 * DA_HALO, d * DA_W), BF16)]
    if final:
        assert d == 1
        for pd in prev_dils:
            in_specs += [pl.BlockSpec((tr // pd, pd * SEG), lambda bi, c: (bi * nt + c, 0))] * 2
            scratch += [pltpu.VMEM((SEG // 128, tr, 128), F32)] * 2
        args += list(prev)
        kern = functools.partial(_da_kernel, qb=qb, nb=nb, d=d, length=length, prev_dils=tuple(prev_dils))
        body, alias_specs, out_shape, aliases = _into(kern, len(in_specs), [out])
        return pl.pallas_call(
            body,
            grid=(b, nt),
            in_specs=in_specs + alias_specs,
            out_specs=[pl.BlockSpec((tr, SEG), lambda bi, c: ((ubase + bi * length) // tr + c, 0))],
            out_shape=out_shape,
            input_output_aliases=aliases,
            scratch_shapes=scratch,
            compiler_params=_cparams("parallel", "parallel"),
            name="da_d1",
        )(*args, out)[0]
    return pl.pallas_call(
        functools.partial(_da_kernel, qb=qb, nb=nb, d=d, length=length, prev_dils=()),
        grid=(b, nt),
        in_specs=in_specs,
        out_specs=[out_block, out_block],
        out_shape=[jax.ShapeDtypeStruct((b * t // d, d * SEG), F32)] * 2,
        scratch_shapes=scratch,
        compiler_params=_cparams("parallel", "parallel"),
        name="da_d%d" % d,
    )(*args)


HG_LEVELS = (32, 16, 8, 4, 2, 1)
HG_NSEC = 2 + len(HG_LEVELS)
HG_W = HG_HEADS * HG_DK
HG_PAIR = 2 * HG_DK


def _hg_constants(rev):
    c = HG_CHUNK
    t = np.arange(c)[:, None]
    j = np.arange(c)[None, :]
    secs = []
    if not rev:
        secs.append(j <= t)
        secs.append(j > t)
    else:
        secs.append(j >= t)
        secs.append(j < t)
    masks = [np.eye(c, dtype=bool)]
    s = np.arange(c)[None, :]
    for m in HG_LEVELS:
        p0 = (t // (2 * m)) * (2 * m)
        upper = (t & m) != 0
        if not rev:
            mid = p0 + m - 1
            sec = np.where(upper, (j > mid) & (j <= t), (j > t) & (j <= mid))
            pair = ((t // (2 * m)) == (s // (2 * m))) & upper & ((s & m) == 0)
        else:
            mid = p0 + m
            sec = np.where(upper, (j >= mid) & (j < t), (j >= t) & (j < mid))
            pair = ((t // (2 * m)) == (s // (2 * m))) & (~upper) & ((s & m) != 0)
        secs.append(sec)
        masks.append(pair)
    w = np.tile(np.concatenate(secs, axis=0).astype(np.float32), (1, 2))
    masks = np.tile(np.stack(masks).astype(np.float32), (1, 1, 2))
    return jnp.asarray(w, BF16), jnp.asarray(masks)


def _bdiag(h0, h1):
    zero = jnp.zeros_like(h0)
    return jnp.concatenate([jnp.concatenate([h0, zero], axis=1), jnp.concatenate([zero, h1], axis=1)], axis=0)


def _hg_pair_operand(a, p):
    return _bdiag(a[:, (2 * p) * HG_DK:(2 * p + 1) * HG_DK], a[:, (2 * p + 1) * HG_DK:(2 * p + 2) * HG_DK])


def _hg_prepare(rev, z, x, v, lb, w_ref, mask_ref):
    c_sz = HG_CHUNK
    z2 = z * LOG2_E
    log2_lb = jnp.log2(lb)
    log2_sig = jnp.minimum(z2, 0.0) - jnp.log2(1.0 + jnp.exp2(-jnp.abs(z2)))
    cc = jnp.log2(1.0 - lb) + log2_sig
    log2f = jnp.maximum(log2_lb, cc) + jnp.log2(1.0 + jnp.exp2(-jnp.abs(log2_lb - cc)))
    kk = (1.0 - lb) * jnp.exp2(log2_sig - z2)
    q = x / (1.0 + jnp.exp2(x * -LOG2_E))
    hi = log2f.astype(BF16)
    lo = (log2f - hi.astype(F32)).astype(BF16)
    dec = jnp.exp2(_dot(w_ref[...], jnp.concatenate([hi, lo], axis=0)))
    sub = lax.broadcasted_iota(jnp.int32, (1, 8, HG_W), 1)
    ys = []
    for lvl, m in enumerate(HG_LEVELS):
        if m >= 8:
            sel = jnp.concatenate(
                [(q if (((8 * r) & m) != 0) != rev else kk)[8 * r:8 * r + 8] for r in range(c_sz // 8)], axis=0)
        else:
            is_q = ((sub & m) == 0) if rev else ((sub & m) != 0)
            sel = jnp.where(is_q, q.reshape(c_sz // 8, 8, HG_W), kk.reshape(c_sz // 8, 8, HG_W)).reshape(c_sz, HG_W)
        ys.append((sel * dec[(2 + lvl) * c_sz:(3 + lvl) * c_sz]).astype(BF16))
    q16, k16 = q.astype(BF16), kk.astype(BF16)
    kd = (kk * dec[c_sz:2 * c_sz]).astype(BF16)
    intra = []
    for p in range(HG_HEADS // 2):
        cols = slice(p * HG_PAIR, (p + 1) * HG_PAIR)
        a2 = mask_ref[0] * _dot_nt(q16[:, cols], _hg_pair_operand(k16, p))
        for lvl in range(len(HG_LEVELS)):
            a2 = a2 + mask_ref[1 + lvl] * _dot_nt(ys[lvl][:, cols], _hg_pair_operand(ys[lvl], p))
        intra.append(_dot(a2.astype(BF16), _hg_pair_operand(v, p)))
    gain = [_dot_tn(v[:, h * HG_DK:(h + 1) * HG_DK], kd[:, h * HG_DK:(h + 1) * HG_DK]) for h in range(HG_HEADS)]
    edge = 0 if rev else c_sz - 1
    return dict(intra=intra, gain=gain, qd=(q * dec[0:c_sz]).astype(BF16), decay=dec[edge:edge + 1])


def _hg_finish(prep, st_ref):
    outs = []
    for p in range(HG_HEADS // 2):
        cols = slice(p * HG_PAIR, (p + 1) * HG_PAIR)
        st = _bdiag(st_ref[2 * p].astype(BF16), st_ref[2 * p + 1].astype(BF16))
        outs.append(prep["intra"][p] + _dot_nt(prep["qd"][:, cols], st))
        for h in (2 * p, 2 * p + 1):
            st_ref[h] = st_ref[h] * prep["decay"][:, h * HG_DK:(h + 1) * HG_DK] + prep["gain"][h]
    return jnp.concatenate(outs, axis=1)


def _hg_kernel(cqf_ref, cff_ref, cif_ref, cqb_ref, cfb_ref, cib_ref, lb_ref, wf_ref, wb_ref, mf_ref, mb_ref,
               of_ref, ob_ref, st_ref, *, nchunks):
    @pl.when(pl.program_id(1) == 0)
    def _():
        st_ref[...] = jnp.zeros_like(st_ref)

    def prepare(step):
        rf = slice(step * HG_CHUNK, (step + 1) * HG_CHUNK)
        rb = slice((nchunks - 1 - step) * HG_CHUNK, (nchunks - step) * HG_CHUNK)
        fwd = _hg_prepare(False, cff_ref[rf, :].astype(F32), cqf_ref[rf, :].astype(F32), cif_ref[rf, :],
                          lb_ref[0:1, :], wf_ref, mf_ref)
        bwd = _hg_prepare(True, cfb_ref[rb, :].astype(F32), cqb_ref[rb, :].astype(F32), cib_ref[rb, :],
                          lb_ref[1:2, :], wb_ref, mb_ref)
        return rf, fwd, rb, bwd

    def finish(rf, fwd, rb, bwd):
        of_ref[rf, :] = _hg_finish(fwd, st_ref.at[0]).astype(BF16)
        ob_ref[rb, :] = _hg_finish(bwd, st_ref.at[1]).astype(BF16)

    pending = prepare(0)
    for step in range(1, nchunks):
        nxt = prepare(step)
        finish(*pending)
        pending = nxt
    finish(*pending)


def _hg_call(proj, lb, consts_f, consts_b, b, t, row_base, outs):
    tb = min(1024, t)
    nt = t // tb
    base = row_base // tb

    def pspec(col, rev):
        return pl.BlockSpec((tb, HG_W), lambda bi, i: (base + bi * nt + (nt - 1 - i if rev else i), col))

    def const(a):
        return pl.BlockSpec(a.shape, lambda bi, i: (0,) * a.ndim)

    (wf, mf), (wb, mb) = consts_f, consts_b
    in_specs = [pspec(3, False), pspec(4, False), pspec(6, False), pspec(3, True), pspec(5, True), pspec(6, True),
                const(lb), const(wf), const(wb), const(mf), const(mb)]
    body, alias_specs, out_shape, aliases = _into(functools.partial(_hg_kernel, nchunks=tb // HG_CHUNK),
                                                  len(in_specs), list(outs))
    return pl.pallas_call(
        body,
        grid=(b, nt),
        in_specs=in_specs + alias_specs,
        out_specs=[pspec(0, False), pspec(0, True)],
        out_shape=out_shape,
        input_output_aliases=aliases,
        scratch_shapes=[pltpu.VMEM((2, HG_HEADS, HG_DK, HG_DK), F32)],
        compiler_params=_cparams("parallel", "arbitrary"),
        name="hgrn",
    )(proj, proj, proj, proj, proj, proj, lb, wf, wb, mf, mb, *outs)


def _outproj_kernel(oa_ref, ob_ref, cf_ref, cb_ref, cg_ref, ng_ref, w_ref, *rest, split):
    x_refs, (x1_ref, mix) = rest[:-2], rest[-2:]
    tot = cf_ref[...].astype(F32) + cb_ref[...].astype(F32)
    gate = cg_ref[...].astype(F32)
    parts = []
    for h in range(HG_HEADS):
        th = tot[:, h * HG_DK:(h + 1) * HG_DK]
        ms = jnp.mean(th * th, axis=-1, keepdims=True)
        parts.append(th * lax.rsqrt(ms + NORM_EPS) * ng_ref[...])
    oc = (jnp.concatenate(parts, axis=1) * (gate / (1.0 + jnp.exp(-gate)))).astype(BF16)
    mix[:, 0:SEG] = oa_ref[...]
    mix[:, SEG:2 * SEG] = ob_ref[...]
    mix[:, 2 * SEG:] = oc
    x1_ref[...] = _select_rows(x_refs, split) + _dot(mix[...], w_ref[...])


def _out_proj(oa, ob, ocf, ocb, proj, norm_g, xs, w_out, tm):
    n = oa.shape[0]
    row = lambda i: (i, 0)
    const = lambda i: (0, 0)
    x_specs, split = _row_specs(xs, tm)
    return pl.pallas_call(
        functools.partial(_outproj_kernel, split=split),
        grid=(n // tm,),
        in_specs=[pl.BlockSpec((tm, SEG), row), pl.BlockSpec((tm, SEG), row),
                  pl.BlockSpec((tm, HG_W), row), pl.BlockSpec((tm, HG_W), row),
                  pl.BlockSpec((tm, HG_W), lambda i: (i, N_IN // HG_W - 1)),
                  pl.BlockSpec((1, HG_DK), const),
                  pl.BlockSpec((D_MODEL, D_MODEL), const)] + x_specs,
        out_specs=pl.BlockSpec((tm, D_MODEL), row),
        out_shape=jax.ShapeDtypeStruct((n, D_MODEL), F32),
        scratch_shapes=[pltpu.VMEM((tm, D_MODEL), BF16)],
        compiler_params=_cparams("parallel"),
        name="out_proj",
    )(oa, ob, ocf, ocb, proj, norm_g, w_out, *xs)


FF_HALO = 16


def _ffn_kernel(x_ref, xp_ref, xn_ref, g_ref, wg_ref, wu_ref, cw_ref, cb_ref, wdn_ref, o_ref, hbuf, abuf, *, tm, row_base,
                starts, ends):
    row0 = row_base + pl.program_id(0) * tm
    at_start = functools.reduce(jnp.logical_or, [row0 == s for s in starts])
    at_end = functools.reduce(jnp.logical_or, [row0 + tm == e for e in ends])
    keep_prev = jnp.where(at_start, 0.0, 1.0)
    keep_next = jnp.where(at_end, 0.0, 1.0)
    def normed(ref):
        x = ref[...]
        ms = jnp.mean(x * x, axis=-1, keepdims=True)
        return (x * lax.rsqrt(ms + NORM_EPS) * g_ref[...]).astype(BF16)

    hbuf[0:tm] = normed(x_ref)
    hbuf[tm:tm + FF_HALO] = normed(xp_ref)
    hbuf[tm + FF_HALO:tm + 2 * FF_HALO] = normed(xn_ref)
    first = lax.broadcasted_iota(jnp.int32, (8, FF_SEG), 0) == 0
    last = lax.broadcasted_iota(jnp.int32, (8, FF_SEG), 0) == 7
    c_tanh = math.sqrt(2.0 / math.pi)
    nseg = D_FF // FF_SEG

    def up_proj(j):
        w = jnp.concatenate([wg_ref[:, j * FF_SEG:(j + 1) * FF_SEG], wu_ref[:, j * FF_SEG:(j + 1) * FF_SEG]], axis=1)
        return _dot(hbuf[...], w)

    y_next = up_proj(0)
    for j in range(nseg):
        cols = slice(j * FF_SEG, (j + 1) * FF_SEG)
        y = y_next
        if j + 1 < nseg:
            y_next = up_proj(j + 1)
        g, up = y[0:tm, 0:FF_SEG], y[0:tm, FF_SEG:]
        g_prev = y[tm + FF_HALO - 1:tm + FF_HALO, 0:FF_SEG] * keep_prev
        g_next = y[tm + FF_HALO:tm + FF_HALO + 1, 0:FF_SEG] * keep_next
        cw0, cw1, cw2 = cw_ref[0:1, cols], cw_ref[1:2, cols], cw_ref[2:3, cols]
        gate = pltpu.roll(g, 1, 0) * cw0 + g * cw1 + pltpu.roll(g, tm - 1, 0) * cw2 + cb_ref[:, cols]
        top = gate[0:8] + jnp.where(first, (g_prev - g[tm - 1:tm]) * cw0, 0.0)
        bot = gate[tm - 8:tm] + jnp.where(last, (g_next - g[0:1]) * cw2, 0.0)
        gate = jnp.concatenate([top, gate[8:tm - 8], bot], axis=0)
        inner = gate * (c_tanh + (c_tanh * 0.044715) * (gate * gate))
        act = (gate * up) * (1.0 + jnp.tanh(inner))
        abuf[:, cols] = act.astype(BF16)
    o_ref[...] = x_ref[...] + _dot(abuf[...], wdn_ref[...])


def _ffn(x1, ln_g, w_up, conv_w, conv_b, w_down, tm, starts, ends, row_base=0, nrows=None):
    n = x1.shape[0]
    nrows = n if nrows is None else nrows
    hpb = tm // FF_HALO
    nh = n // FF_HALO
    t0 = row_base // tm
    row = lambda i: (t0 + i, 0)
    const = lambda i: (0, 0)
    once = pl.Buffered(1)
    return pl.pallas_call(
        functools.partial(_ffn_kernel, tm=tm, row_base=row_base, starts=starts, ends=ends),
        grid=(nrows // tm,),
        in_specs=[pl.BlockSpec((tm, D_MODEL), row),
                  pl.BlockSpec((FF_HALO, D_MODEL), lambda i: (jnp.maximum((t0 + i) * hpb - 1, 0), 0)),
                  pl.BlockSpec((FF_HALO, D_MODEL), lambda i: (jnp.minimum((t0 + i + 1) * hpb, nh - 1), 0)),
                  pl.BlockSpec((1, D_MODEL), const),
                  pl.BlockSpec((D_MODEL, D_FF), lambda i: (0, 0), pipeline_mode=once),
                  pl.BlockSpec((D_MODEL, D_FF), lambda i: (0, 1), pipeline_mode=once),
                  pl.BlockSpec((3, D_FF), const), pl.BlockSpec((1, D_FF), const),
                  pl.BlockSpec((D_FF, D_MODEL), const, pipeline_mode=once)],
        out_specs=pl.BlockSpec((tm, D_MODEL), lambda i: (i, 0)),
        out_shape=jax.ShapeDtypeStruct((nrows, D_MODEL), F32),
        scratch_shapes=[pltpu.VMEM((tm + 2 * FF_HALO, D_MODEL), BF16), pltpu.VMEM((tm, D_FF), BF16)],
        compiler_params=_cparams("parallel"),
        name="ffn",
    )(x1, x1, x1, ln_g, w_up, w_up, conv_w, conv_b, w_down)


def _lower_bounds(lb_logits):
    p = jax.nn.softmax(lb_logits.astype(F32), axis=1)
    c = jnp.cumsum(p, axis=1)
    return c - c[:, :1]


def _row_tile(groups, tm=ROW_TILE):
    while any(t % tm for _, t, _ in groups):
        tm //= 2
    return tm


def kernel(x_prompt, x_sample, ln_mix_g, w_in, na_q_g, na_k_g, na_rpb, da_q_g, da_k_g, t5_bias, hg_lb_logits,
           hg_norm_g, w_out, ln_ffn_g, w_up, conv_w, conv_b, w_down):
    depth = w_in.shape[0]
    groups = []
    base = 0
    for xg in (x_prompt, x_sample):
        b, t, _ = xg.shape
        groups.append((b, t, base))
        base += b * t
    starts = tuple(rb + i * t for b, t, rb in groups for i in range(b))
    ends = tuple(rb + (i + 1) * t for b, t, rb in groups for i in range(b))
    tm = _row_tile(groups)
    n = base
    xs = [x_prompt.reshape(-1, D_MODEL), x_sample.reshape(-1, D_MODEL)]

    lb = _lower_bounds(hg_lb_logits)
    scale = HEAD_DIM ** -0.5
    tile4 = lambda g: jnp.tile(g.astype(F32), NA_HEADS)
    gm = jnp.asarray(np.kron(np.eye(SEG // HEAD_DIM), np.ones((HEAD_DIM, HEAD_DIM))) / HEAD_DIM, BF16)
    hg_fwd_consts = _hg_constants(False)
    hg_bwd_consts = _hg_constants(True)
    da_bias = {}

    def da_bias_for(d, qb):
        if (d, qb) not in da_bias:
            da_bias[(d, qb)] = _da_bias_table(t5_bias, d, qb)
        return da_bias[(d, qb)]

    for l in range(depth):
        qg = jnp.stack([tile4(na_q_g[l]) * (scale * LOG2_E), tile4(na_k_g[l]),
                        tile4(da_q_g[l]) * (scale * LOG2_E), tile4(da_k_g[l])])
        proj, *dil_views = _in_proj(xs, ln_mix_g[l].reshape(1, -1), w_in[l].astype(BF16), qg, gm, tm)
        na_bias = _na_bias_table(na_rpb[l])
        if l == 0:
            oa = jnp.zeros((n, SEG), BF16)
            ob = jnp.zeros((n, SEG), BF16)
            ocf = jnp.zeros((n, HG_W), BF16)
            ocb = jnp.zeros((n, HG_W), BF16)
        for b, t, rb in groups:
            oa = _na_call(proj, na_bias, b, t, rb, oa)
            prev = []
            for d, view in zip(DA_DILATED, dil_views):
                prev += _da_call(view, 0, da_bias_for(d, min(DA_QB[d], t // d)), b, t, rb, d)
            ob = _da_call(proj, DA_SEGS[0] * SEG // DA_W, da_bias_for(1, min(DA_QB[1], t)), b, t, rb, 1,
                          prev=prev, prev_dils=DA_DILATED, out=ob)
            ocf, ocb = _hg_call(proj, lb[:, l], hg_fwd_consts, hg_bwd_consts, b, t, rb, (ocf, ocb))
        x1 = _out_proj(oa, ob, ocf, ocb, proj, hg_norm_g[l].reshape(1, -1).astype(F32), xs, w_out[l].astype(BF16), tm)
        ln_g = ln_ffn_g[l].reshape(1, -1)
        wu, wd = w_up[l].astype(BF16), w_down[l].astype(BF16) * 0.5
        cw, cb = conv_w[l], conv_b[l].reshape(1, -1)
        if l + 1 < depth:
            xs = [_ffn(x1, ln_g, wu, cw, cb, wd, tm, starts, ends)]
    return tuple(_ffn(x1, ln_g, wu, cw, cb, wd, tm, starts, ends, rb, b * t).reshape(xg.shape)
                 for (b, t, rb), xg in zip(groups, (x_prompt, x_sample)))
```
